```python
import math
import jax
import jax.numpy as jnp
from jax import lax
import numpy as np

D_MODEL = 1024
BATCH = 4
SEQ = 4096
DEPTH = 1

GRID_W = 64
CTX_LEN = 256
NORM_EPS = 1e-6

N_HEADS = 8
N_KV_HEADS = 2
HEAD_DIM = 64
WINDOW = 128
BLOCK = 128
ROPE_BASE = 10000.0

HY_WIDTH = 512
HY_SHORT = 3
HY_EMB_BANDS = 16
HY_EMB_DIM = 1 + 2 * HY_EMB_BANDS
HY_FILTER_HIDDEN = 64
HY_FAST_DECAY = 0.3
HY_SLOW_DECAY = 1.5
HY_DECAY_TARGET = 1e-2

PEER_HEADS = 8
PEER_KEYS = 128
PEER_EXPERTS = PEER_KEYS * PEER_KEYS
PEER_TOPK = 16
PEER_DKEY = 128
PEER_CHUNK = 128

Q_W = N_HEADS * HEAD_DIM
KV_W = N_KV_HEADS * HEAD_DIM
HY_IN = 3 * HY_WIDTH
GATE_W = 2 * D_MODEL
OFF_K = Q_W
OFF_V = OFF_K + KV_W
OFF_HY = OFF_V + KV_W
OFF_G = OFF_HY + HY_IN
IN_W = OFF_G + GATE_W

kernel_name = 'hybrid_swa_hyena_peer_dit_block'


def _rmsnorm(x, g):
    xf = x.astype(jnp.float32)
    y = xf * lax.rsqrt(jnp.mean(xf * xf, axis=-1, keepdims=True) + NORM_EPS)
    return (y * g.astype(jnp.float32)).astype(x.dtype)


def _modulate(h, shift, scale):
    return h * (1 + scale) + shift


def _rope_1d(x, pos):
    f = x.shape[-1] // 2
    inv = ROPE_BASE ** (-jnp.arange(f, dtype=jnp.float32) / f)
    ang = pos.astype(jnp.float32)[:, None] * inv[None, :]
    cos = jnp.cos(ang)[:, None, :]
    sin = jnp.sin(ang)[:, None, :]
    xf = x.astype(jnp.float32)
    x1, x2 = xf[..., :f], xf[..., f:]
    return jnp.concatenate([x1 * cos - x2 * sin, x2 * cos + x1 * sin], axis=-1).astype(x.dtype)


def _rope_2d(x, row, col):
    half = x.shape[-1] // 2
    return jnp.concatenate([_rope_1d(x[..., :half], row), _rope_1d(x[..., half:], col)], axis=-1)


def _window_attention(q, k, v, k_ctx, v_ctx, sink):
    B, L, H, dh = q.shape
    C = k_ctx.shape[1]
    G = H // N_KV_HEADS
    nb = L // BLOCK
    scale = dh ** -0.5
    qb = q.reshape(B, nb, BLOCK, N_KV_HEADS, G, dh)
    pad = ((0, 0), (BLOCK, BLOCK), (0, 0), (0, 0))
    kp = jnp.pad(k, pad).reshape(B, nb + 2, BLOCK, N_KV_HEADS, dh)
    vp = jnp.pad(v, pad).reshape(B, nb + 2, BLOCK, N_KV_HEADS, dh)
    kw = jnp.concatenate([kp[:, :-2], kp[:, 1:-1], kp[:, 2:]], axis=2)
    vw = jnp.concatenate([vp[:, :-2], vp[:, 1:-1], vp[:, 2:]], axis=2)
    s_loc = jnp.einsum('bnqhgd,bnkhd->bnhgqk', qb, kw).astype(jnp.float32) * scale
    s_ctx = jnp.einsum('bnqhgd,bchd->bnhgqc', qb, k_ctx).astype(jnp.float32) * scale
    blk = jnp.arange(nb, dtype=jnp.int32)
    qpos = blk[:, None] * BLOCK + jnp.arange(BLOCK, dtype=jnp.int32)[None, :]
    kpos = (blk[:, None] - 1) * BLOCK + jnp.arange(3 * BLOCK, dtype=jnp.int32)[None, :]
    rel = kpos[:, None, :] - qpos[:, :, None]
    valid = (jnp.abs(rel) <= WINDOW) & (kpos[:, None, :] >= 0) & (kpos[:, None, :] < L)
    s_loc = jnp.where(valid[None, :, None, None], s_loc, -jnp.inf)
    s_sink = jnp.broadcast_to(sink.astype(jnp.float32).reshape(1, 1, N_KV_HEADS, G, 1, 1),
                              s_loc.shape[:-1] + (1,))
    prob = jax.nn.softmax(jnp.concatenate([s_loc, s_ctx, s_sink], axis=-1), axis=-1)
    p_loc = prob[..., :3 * BLOCK].astype(v.dtype)
    p_ctx = prob[..., 3 * BLOCK:3 * BLOCK + C].astype(v.dtype)
    o = (jnp.einsum('bnhgqk,bnkhd->bnqhgd', p_loc, vw)
         + jnp.einsum('bnhgqc,bchd->bnqhgd', p_ctx, v_ctx))
    return o.reshape(B, L, H * dh)


def _context_attention(q, k, v, sink):
    B, C, H, dh = q.shape
    G = H // N_KV_HEADS
    qg = q.reshape(B, C, N_KV_HEADS, G, dh)
    s = jnp.einsum('bqhgd,bkhd->bhgqk', qg, k).astype(jnp.float32) * (dh ** -0.5)
    s_sink = jnp.broadcast_to(sink.astype(jnp.float32).reshape(1, N_KV_HEADS, G, 1, 1), s.shape[:-1] + (1,))
    prob = jax.nn.softmax(jnp.concatenate([s, s_sink], axis=-1), axis=-1)[..., :C]
    o = jnp.einsum('bhgqk,bkhd->bqhgd', prob.astype(v.dtype), v)
    return o.reshape(B, C, H * dh)


def _hyena_filter(L, fw1, fb1, fw2, fb2, fw3, fb3, freq):
    fw1, fb1, fw2, fb2, fw3, fb3, freq = [a.astype(jnp.float32) for a in (fw1, fb1, fw2, fb2, fw3, fb3, freq)]
    t = jnp.arange(L, dtype=jnp.float32) / L
    bands = jnp.arange(1, HY_EMB_BANDS + 1, dtype=jnp.float32)
    ang = 2.0 * math.pi * t[:, None] * bands[None, :]
    z = jnp.concatenate([t[:, None], jnp.cos(ang), jnp.sin(ang)], axis=-1)
    h = jnp.sin(freq * (z @ fw1 + fb1))
    h = jnp.sin(freq * (h @ fw2 + fb2))
    h = h @ fw3 + fb3
    deltas = jnp.linspace(math.log(HY_DECAY_TARGET) / HY_SLOW_DECAY,
                          math.log(HY_DECAY_TARGET) / HY_FAST_DECAY, HY_WIDTH, dtype=jnp.float32)
    decay = jnp.exp(-t[:, None] * jnp.abs(deltas)[None, :])
    h_fwd = h[:, :HY_WIDTH] * decay
    h_bwd = h[:, HY_WIDTH:] * decay
    kern = jnp.concatenate([h_fwd, jnp.zeros((1, HY_WIDTH), jnp.float32), jnp.flip(h_bwd[:L - 1], axis=0)], axis=0)
    return kern / jnp.sum(jnp.abs(kern), axis=0, keepdims=True)


def _short_conv(u, w, b):
    up = jnp.pad(u, ((0, 0), (1, 1), (0, 0)))
    return up[:, :-2] * w[0] + up[:, 1:-1] * w[1] + up[:, 2:] * w[2] + b


def _fft_long_conv(u, kern, skip):
    L = u.shape[1]
    uf = jnp.fft.rfft(u.astype(jnp.float32), n=2 * L, axis=1)
    kf = jnp.fft.rfft(kern, n=2 * L, axis=0)
    y = jnp.fft.irfft(uf * kf[None], n=2 * L, axis=1)[:, :L]
    return (y + u.astype(jnp.float32) * skip.astype(jnp.float32)).astype(u.dtype)


def _hyena_branch(z, conv_w, conv_b, kern, skip):
    z = _short_conv(z, conv_w, conv_b)
    x0, x1, v = z[..., :HY_WIDTH], z[..., HY_WIDTH:2 * HY_WIDTH], z[..., 2 * HY_WIDTH:]
    v = _fft_long_conv(v * x1, kern, skip)
    return v * x0


def _branch_merge(p, y_attn, y_hy, w_o_attn, w_o_hy, w_out):
    gate = jax.nn.sigmoid(p[..., OFF_G:])
    merged = gate[..., :D_MODEL] * (y_attn @ w_o_attn) + gate[..., D_MODEL:] * (y_hy @ w_o_hy)
    return merged @ w_out


def _peer_route(h, w_q, sub_keys):
    T = h.shape[0]
    q = (h @ w_q).reshape(T, PEER_HEADS, 2, PEER_DKEY // 2)
    s = jnp.einsum('thpd,hpkd->thpk', q, sub_keys).astype(jnp.float32)
    s_top, i_top = lax.top_k(s, PEER_TOPK)
    cand = s_top[:, :, 0, :, None] + s_top[:, :, 1, None, :]
    cand_idx = i_top[:, :, 0, :, None] * PEER_KEYS + i_top[:, :, 1, None, :]
    best, pos = lax.top_k(cand.reshape(T, PEER_HEADS, PEER_TOPK * PEER_TOPK), PEER_TOPK)
    idx = jnp.take_along_axis(cand_idx.reshape(T, PEER_HEADS, PEER_TOPK * PEER_TOPK), pos, axis=-1)
    g = jax.nn.softmax(best, axis=-1)
    return idx.reshape(T, PEER_HEADS * PEER_TOPK), g.reshape(T, PEER_HEADS * PEER_TOPK)


def _peer_ffn(h, w_q, sub_keys, u_tab, v_tab):
    B, L, D = h.shape
    T = B * L
    ht = h.reshape(T, D)
    idx, g = _peer_route(ht, w_q, sub_keys)
    nc = T // PEER_CHUNK

    def chunk(args):
        hc, ic, gc = args
        u = u_tab[ic]
        a = jax.nn.gelu(jnp.einsum('tkd,td->tk', u, hc))
        return jnp.einsum('tk,tkd->td', (gc * a).astype(hc.dtype), v_tab[ic])

    out = lax.map(chunk, (ht.reshape(nc, PEER_CHUNK, D),
                          idx.reshape(nc, PEER_CHUNK, -1),
                          g.reshape(nc, PEER_CHUNK, -1)))
    return out.reshape(B, L, D)


def setup_inputs(seed: int = 0) -> dict:
    key = jax.random.key(seed)
    ks = jax.random.split(key, 28)

    def nrm(k, shape, scale):
        return jax.random.normal(k, shape, jnp.float32) * scale

    Dp = DEPTH
    return {
        'x': nrm(ks[0], (BATCH, SEQ, D_MODEL), 1.0),
        'c': nrm(ks[1], (BATCH, D_MODEL), 1.0),
        'ctx': nrm(ks[2], (BATCH, CTX_LEN, D_MODEL), 1.0),
        'c_ctx': nrm(ks[3], (D_MODEL,), 1.0),
        'w_mod': nrm(ks[4], (Dp, D_MODEL, 6 * D_MODEL), 0.5 * D_MODEL ** -0.5),
        'b_mod': nrm(ks[5], (Dp, 6 * D_MODEL), 0.02),
        'norm1_g': 1.0 + nrm(ks[6], (Dp, D_MODEL), 0.02),
        'w_in': nrm(ks[7], (Dp, D_MODEL, IN_W), D_MODEL ** -0.5),
        'attn_sink': nrm(ks[8], (Dp, N_HEADS), 0.5),
        'hy_conv_w': nrm(ks[9], (Dp, HY_SHORT, HY_IN), HY_SHORT ** -0.5),
        'hy_conv_b': nrm(ks[10], (Dp, HY_IN), 0.02),
        'hy_fw1': nrm(ks[11], (Dp, HY_EMB_DIM, HY_FILTER_HIDDEN), HY_EMB_DIM ** -0.5),
        'hy_fb1': nrm(ks[12], (Dp, HY_FILTER_HIDDEN), 0.1),
        'hy_fw2': nrm(ks[13], (Dp, HY_FILTER_HIDDEN, HY_FILTER_HIDDEN), HY_FILTER_HIDDEN ** -0.5),
        'hy_fb2': nrm(ks[14], (Dp, HY_FILTER_HIDDEN), 0.1),
        'hy_fw3': nrm(ks[15], (Dp, HY_FILTER_HIDDEN, 2 * HY_WIDTH), HY_FILTER_HIDDEN ** -0.5),
        'hy_fb3': nrm(ks[16], (Dp, 2 * HY_WIDTH), 0.02),
        'hy_freq': 1.0 + nrm(ks[17], (Dp, HY_FILTER_HIDDEN), 0.02),
        'hy_skip': nrm(ks[18], (Dp, HY_WIDTH), 0.5),
        'w_o_attn': nrm(ks[19], (Dp, Q_W, D_MODEL), Q_W ** -0.5),
        'w_o_hy': nrm(ks[20], (Dp, HY_WIDTH, D_MODEL), HY_WIDTH ** -0.5),
        'w_out': nrm(ks[21], (Dp, D_MODEL, D_MODEL), D_MODEL ** -0.5),
        'norm2_g': 1.0 + nrm(ks[22], (Dp, D_MODEL), 0.02),
        'peer_wq': nrm(ks[23], (Dp, D_MODEL, PEER_HEADS * PEER_DKEY), D_MODEL ** -0.5),
        'peer_keys': nrm(ks[24], (Dp, PEER_HEADS, 2, PEER_KEYS, PEER_DKEY // 2), (PEER_DKEY // 2) ** -0.5),
        'peer_u': nrm(ks[25], (Dp, PEER_EXPERTS, D_MODEL), D_MODEL ** -0.5),
        'peer_v': nrm(ks[26], (Dp, PEER_EXPERTS, D_MODEL), 0.2),
        'final_g': 1.0 + nrm(ks[27], (D_MODEL,), 0.02),
    }


def reference(x, c, ctx, c_ctx, w_mod, b_mod, norm1_g, w_in, attn_sink, hy_conv_w, hy_conv_b,
              hy_fw1, hy_fb1, hy_fw2, hy_fb2, hy_fw3, hy_fb3, hy_freq, hy_skip,
              w_o_attn, w_o_hy, w_out, norm2_g, peer_wq, peer_keys, peer_u, peer_v, final_g):
    B, L, _ = x.shape
    C = ctx.shape[1]
    rows = L // GRID_W
    row = jnp.repeat(jnp.arange(rows, dtype=jnp.int32), GRID_W)
    col = jnp.tile(jnp.arange(GRID_W, dtype=jnp.int32), rows)
    for li in range(DEPTH):
        last = li == DEPTH - 1
        hy = (hy_fw1[li], hy_fb1[li], hy_fw2[li], hy_fb2[li], hy_fw3[li], hy_fb3[li], hy_freq[li])
        mod = jax.nn.silu(c) @ w_mod[li] + b_mod[li]
        mod_c = jax.nn.silu(c_ctx) @ w_mod[li] + b_mod[li]
        sh1, sc1, g1, sh2, sc2, g2 = jnp.split(mod[:, None, :], 6, axis=-1)
        csh1, csc1, cg1, csh2, csc2, cg2 = jnp.split(mod_c[None, None, :], 6, axis=-1)

        hc = _modulate(_rmsnorm(ctx, norm1_g[li]), csh1, csc1)
        if last:
            pkv = hc @ w_in[li][:, OFF_K:OFF_HY]
            kc = pkv[..., :KV_W].reshape(B, C, N_KV_HEADS, HEAD_DIM)
            vc = pkv[..., KV_W:].reshape(B, C, N_KV_HEADS, HEAD_DIM)
        else:
            pc = hc @ w_in[li]
            qc = pc[..., :OFF_K].reshape(B, C, N_HEADS, HEAD_DIM)
            kc = pc[..., OFF_K:OFF_V].reshape(B, C, N_KV_HEADS, HEAD_DIM)
            vc = pc[..., OFF_V:OFF_HY].reshape(B, C, N_KV_HEADS, HEAD_DIM)
            yc_attn = _context_attention(qc, kc, vc, attn_sink[li])
            yc_hy = _hyena_branch(pc[..., OFF_HY:OFF_G], hy_conv_w[li], hy_conv_b[li],
                                  _hyena_filter(C, *hy), hy_skip[li])
            ctx_mid = ctx + cg1 * _branch_merge(pc, yc_attn, yc_hy, w_o_attn[li], w_o_hy[li], w_out[li])
            hc2 = _modulate(_rmsnorm(ctx_mid, norm2_g[li]), csh2, csc2)
            ctx_next = ctx_mid + cg2 * _peer_ffn(hc2, peer_wq[li], peer_keys[li], peer_u[li], peer_v[li])

        h = _modulate(_rmsnorm(x, norm1_g[li]), sh1, sc1)
        p = h @ w_in[li]
        q = _rope_2d(p[..., :OFF_K].reshape(B, L, N_HEADS, HEAD_DIM), row, col)
        k = _rope_2d(p[..., OFF_K:OFF_V].reshape(B, L, N_KV_HEADS, HEAD_DIM), row, col)
        v = p[..., OFF_V:OFF_HY].reshape(B, L, N_KV_HEADS, HEAD_DIM)
        y_attn = _window_attention(q, k, v, kc, vc, attn_sink[li])
        y_hy = _hyena_branch(p[..., OFF_HY:OFF_G], hy_conv_w[li], hy_conv_b[li],
                             _hyena_filter(L, *hy), hy_skip[li])
        x = x + g1 * _branch_merge(p, y_attn, y_hy, w_o_attn[li], w_o_hy[li], w_out[li])
        h2 = _modulate(_rmsnorm(x, norm2_g[li]), sh2, sc2)
        x = x + g2 * _peer_ffn(h2, peer_wq[li], peer_keys[li], peer_u[li], peer_v[li])
        if not last:
            ctx = ctx_next
    return _rmsnorm(x, final_g)
```

```python
import functools
import math

import jax
import jax.numpy as jnp
from jax import lax
from jax.experimental import pallas as pl
from jax.experimental.pallas import tpu as pltpu

F32 = jnp.float32
BF16 = jnp.bfloat16

D_MODEL = 1024
GRID_W = 64
NORM_EPS = 1e-6
N_HEADS = 8
N_KV_HEADS = 2
HEAD_DIM = 64
BLOCK = 128
ROPE_BASE = 10000.0
HY_WIDTH = 512
HY_EMB_BANDS = 16
HY_FILTER_HIDDEN = 64
HY_FAST_DECAY = 0.3
HY_SLOW_DECAY = 1.5
HY_DECAY_TARGET = 1e-2
PEER_HEADS = 8
PEER_KEYS = 128
PEER_TOPK = 16
PEER_DKEY = 128
Q_W = N_HEADS * HEAD_DIM
KV_W = N_KV_HEADS * HEAD_DIM
HY_IN = 3 * HY_WIDTH
OFF_K = Q_W
OFF_V = OFF_K + KV_W
OFF_HY = OFF_V + KV_W
OFF_G = OFF_HY + HY_IN

LANES = 128
VMEM_LIMIT = 56 * 1024 * 1024
NEG = -1e30
KV_DUP = 2 * KV_W


def _params(sem):
    return pltpu.CompilerParams(dimension_semantics=sem, vmem_limit_bytes=VMEM_LIMIT)


def _dot(a, b):
    return lax.dot_general(a, b, (((1,), (0,)), ((), ())), preferred_element_type=F32)


def _dot_nt(a, b):
    return lax.dot_general(a, b, (((1,), (1,)), ((), ())), preferred_element_type=F32)


def _dot_tn(a, b):
    return lax.dot_general(a, b, (((0,), (0,)), ((), ())), preferred_element_type=F32)


def _split(a):
    hi = a.astype(BF16)
    lo = (a - hi.astype(F32)).astype(BF16)
    return hi, lo


def _dot3(a, b, dot=_dot):
    ah, al = _split(a)
    bh, bl = _split(b)
    return dot(ah, bh) + dot(ah, bl) + dot(al, bh)


def _norm_mod(x, g, shift, scale):
    y = x * lax.rsqrt(jnp.mean(x * x, axis=-1, keepdims=True) + NORM_EPS) * g
    return y * (1.0 + scale) + shift


def _mod_kernel(c_ref, w_ref, b_ref, o_ref):
    c = c_ref[...]
    s = c * jax.nn.sigmoid(c)
    o_ref[...] = _dot3(s, w_ref[...]) + b_ref[...]


def _mod_call(c8, w_mod, b_mod):
    n = w_mod.shape[1] // D_MODEL
    return pl.pallas_call(
        _mod_kernel,
        grid=(n,),
        in_specs=[pl.BlockSpec((8, D_MODEL), lambda j: (0, 0)),
                  pl.BlockSpec((D_MODEL, D_MODEL), lambda j: (0, j)),
                  pl.BlockSpec((1, D_MODEL), lambda j: (0, j))],
        out_specs=pl.BlockSpec((8, D_MODEL), lambda j: (0, j)),
        out_shape=jax.ShapeDtypeStruct((8, w_mod.shape[1]), F32),
        compiler_params=_params(("arbitrary",)),
        name="mod",
    )(c8, w_mod, b_mod.reshape(1, -1))


ROT_W = Q_W + KV_DUP
CAT_W = Q_W + 2 * KV_DUP + HY_IN + ROT_W


def _inproj_kernel(x_ref, mod_ref, g_ref, w_ref, cos_ref, sin_ref, q_ref, k_ref, v_ref, hy_ref):
    m = mod_ref[0]
    h = _norm_mod(x_ref[0], g_ref[...], m[0:1], m[1:2]).astype(BF16)
    o_v = ROT_W
    o_hy = o_v + KV_DUP
    o_rot = o_hy + HY_IN
    p = _dot(h, w_ref[:, 0:ROT_W])
    pr = _dot(h, w_ref[:, o_rot:o_rot + ROT_W])
    qk = p * cos_ref[...] + pr * sin_ref[...]
    q_ref[0] = qk[:, :Q_W].astype(BF16)
    k_ref[0] = qk[:, Q_W:].astype(BF16)
    v_ref[0] = _dot(h, w_ref[:, o_v:o_hy]).astype(BF16)
    for j in range(HY_IN // HY_WIDTH):
        lo = o_hy + j * HY_WIDTH
        hy_ref[0, :, j * HY_WIDTH:(j + 1) * HY_WIDTH] = _dot(h, w_ref[:, lo:lo + HY_WIDTH]).astype(BF16)


def _inproj_call(x, mod3, g, w_cat, cos_t, sin_t, tm):
    B, L, D = x.shape
    return pl.pallas_call(
        _inproj_kernel,
        grid=(B, L // tm),
        in_specs=[pl.BlockSpec((1, tm, D), lambda b, i: (b, i, 0)),
                  pl.BlockSpec((1, 6, D), lambda b, i: (b, 0, 0)),
                  pl.BlockSpec((1, D), lambda b, i: (0, 0)),
                  pl.BlockSpec((D, CAT_W), lambda b, i: (0, 0)),
                  pl.BlockSpec((tm, ROT_W), lambda b, i: (i, 0)),
                  pl.BlockSpec((tm, ROT_W), lambda b, i: (i, 0))],
        out_specs=[pl.BlockSpec((1, tm, Q_W), lambda b, i: (b, i, 0)),
                   pl.BlockSpec((1, tm, KV_DUP), lambda b, i: (b, i, 0)),
                   pl.BlockSpec((1, tm, KV_DUP), lambda b, i: (b, i, 0)),
                   pl.BlockSpec((1, tm, HY_IN), lambda b, i: (b, i, 0))],
        out_shape=[jax.ShapeDtypeStruct((B, L, Q_W), BF16),
                   jax.ShapeDtypeStruct((B, L, KV_DUP), BF16),
                   jax.ShapeDtypeStruct((B, L, KV_DUP), BF16),
                   jax.ShapeDtypeStruct((B, L, HY_IN), BF16)],
        compiler_params=_params(("parallel", "arbitrary")),
        name="inproj",
    )(x, mod3, g, w_cat, cos_t, sin_t)


def _ctxproj_kernel(x_ref, mod_ref, g_ref, w_ref, k_ref, v_ref):
    m = mod_ref[0]
    h = _norm_mod(x_ref[0], g_ref[...], m[0:1], m[1:2]).astype(BF16)
    k_ref[0] = _dot(h, w_ref[:, :KV_DUP]).astype(BF16)
    v_ref[0] = _dot(h, w_ref[:, KV_DUP:]).astype(BF16)


def _ctxproj_call(ctx, mod3, g, w_kv):
    B, C, D = ctx.shape
    return pl.pallas_call(
        _ctxproj_kernel,
        grid=(B,),
        in_specs=[pl.BlockSpec((1, C, D), lambda b: (b, 0, 0)),
                  pl.BlockSpec((1, 6, D), lambda b: (4, 0, 0)),
                  pl.BlockSpec((1, D), lambda b: (0, 0)),
                  pl.BlockSpec((D, 2 * KV_DUP), lambda b: (0, 0))],
        out_specs=[pl.BlockSpec((1, C, KV_DUP), lambda b: (b, 0, 0)),
                   pl.BlockSpec((1, C, KV_DUP), lambda b: (b, 0, 0))],
        out_shape=[jax.ShapeDtypeStruct((B, C, KV_DUP), BF16),
                   jax.ShapeDtypeStruct((B, C, KV_DUP), BF16)],
        compiler_params=_params(("arbitrary",)),
        name="ctxproj",
    )(ctx, mod3, g, w_kv)


def _attn_kernel(q_ref, kp_ref, kc_ref, kn_ref, vp_ref, vc_ref, vn_ref, kx_ref, vx_ref, sink_ref, o_ref):
    n = pl.program_id(1)
    nb = pl.num_programs(1)
    q = q_ref[0]
    rows = (N_HEADS // N_KV_HEADS) * BLOCK
    lo = lax.broadcasted_iota(jnp.int32, (BLOCK, LANES), 1) < HEAD_DIM
    r = lax.broadcasted_iota(jnp.int32, (rows, BLOCK), 0) % BLOCK
    c = lax.broadcasted_iota(jnp.int32, (rows, BLOCK), 1)
    ok_prev = jnp.logical_and(c >= r, n > 0)
    ok_next = jnp.logical_and(c <= r, n < nb - 1)
    scale = HEAD_DIM ** -0.5
    zero = jnp.zeros((BLOCK, LANES), BF16)
    for g in range(N_KV_HEADS):
        sl = slice(g * LANES, (g + 1) * LANES)
        qa = q[:, 2 * g * LANES:(2 * g + 1) * LANES]
        qb = q[:, (2 * g + 1) * LANES:(2 * g + 2) * LANES]
        lhs = jnp.concatenate([jnp.where(lo, qa, zero), jnp.where(lo, zero, qa),
                               jnp.where(lo, qb, zero), jnp.where(lo, zero, qb)], axis=0)
        s_p = jnp.where(ok_prev, _dot_nt(lhs, kp_ref[0, :, sl]) * scale, NEG)
        s_c = _dot_nt(lhs, kc_ref[0, :, sl]) * scale
        s_n = jnp.where(ok_next, _dot_nt(lhs, kn_ref[0, :, sl]) * scale, NEG)
        s_x = _dot_nt(lhs, kx_ref[0, :, sl]) * scale
        sink = sink_ref[g][:, 0:1]
        m = jnp.maximum(jnp.maximum(jnp.max(s_p, axis=-1, keepdims=True), jnp.max(s_c, axis=-1, keepdims=True)),
                        jnp.maximum(jnp.max(s_n, axis=-1, keepdims=True), jnp.max(s_x, axis=-1, keepdims=True)))
        m = jnp.maximum(m, sink)
        p_p = jnp.exp(s_p - m)
        p_c = jnp.exp(s_c - m)
        p_n = jnp.exp(s_n - m)
        p_x = jnp.exp(s_x - m)
        den = (jnp.sum(p_p, axis=-1, keepdims=True) + jnp.sum(p_c, axis=-1, keepdims=True)
               + jnp.sum(p_n, axis=-1, keepdims=True) + jnp.sum(p_x, axis=-1, keepdims=True)
               + jnp.exp(sink - m))
        o = (_dot(p_p.astype(BF16), vp_ref[0, :, sl]) + _dot(p_c.astype(BF16), vc_ref[0, :, sl])
             + _dot(p_n.astype(BF16), vn_ref[0, :, sl]) + _dot(p_x.astype(BF16), vx_ref[0, :, sl]))
        o = o / den
        o_ref[0, :, 2 * g * LANES:(2 * g + 1) * LANES] = jnp.where(
            lo, o[0:BLOCK], o[BLOCK:2 * BLOCK]).astype(BF16)
        o_ref[0, :, (2 * g + 1) * LANES:(2 * g + 2) * LANES] = jnp.where(
            lo, o[2 * BLOCK:3 * BLOCK], o[3 * BLOCK:4 * BLOCK]).astype(BF16)


def _attn_call(q, k, v, kx, vx, sink_b):
    B, L, _ = q.shape
    C = kx.shape[1]
    nb = L // BLOCK
    kv = lambda f: pl.BlockSpec((1, BLOCK, KV_DUP), f)
    prev = lambda b, n: (b, jnp.maximum(n - 1, 0), 0)
    cur = lambda b, n: (b, n, 0)
    nxt = lambda b, n: (b, jnp.minimum(n + 1, nb - 1), 0)
    rows = (N_HEADS // N_KV_HEADS) * BLOCK
    return pl.pallas_call(
        _attn_kernel,
        grid=(B, nb),
        in_specs=[pl.BlockSpec((1, BLOCK, Q_W), cur),
                  kv(prev), kv(cur), kv(nxt), kv(prev), kv(cur), kv(nxt),
                  pl.BlockSpec((1, C, KV_DUP), lambda b, n: (b, 0, 0)),
                  pl.BlockSpec((1, C, KV_DUP), lambda b, n: (b, 0, 0)),
                  pl.BlockSpec((N_KV_HEADS, rows, LANES), lambda b, n: (0, 0, 0))],
        out_specs=pl.BlockSpec((1, BLOCK, Q_W), cur),
        out_shape=jax.ShapeDtypeStruct((B, L, Q_W), BF16),
        compiler_params=_params(("parallel", "arbitrary")),
        name="attn",
    )(q, k, k, k, v, v, v, kx, vx, sink_b)


def _filt_kernel(z_ref, w1_ref, b1_ref, fr_ref, w2_ref, b2_ref, w3f_ref, w3b_ref, b3f_ref, b3b_ref,
                 dl_ref, hs_ref, hd_ref):
    L = z_ref.shape[0]
    z = z_ref[...]
    fr = fr_ref[...]
    h = jnp.sin(fr * (_dot3(z, w1_ref[...]) + b1_ref[...]))
    h = jnp.sin(fr * (_dot3(h, w2_ref[...]) + b2_ref[...]))
    decay = jnp.exp(-z[:, 0:1] * dl_ref[...])
    hf = (_dot3(h, w3f_ref[...]) + b3f_ref[...]) * decay
    hb = (_dot3(h, w3b_ref[...]) + b3b_ref[...]) * decay
    row = lax.broadcasted_iota(jnp.int32, hb.shape, 0)
    hb = jnp.where(row < L - 1, hb, 0.0)
    norm = jnp.sum(jnp.abs(hf), axis=0, keepdims=True) + jnp.sum(jnp.abs(hb), axis=0, keepdims=True)
    inv = 1.0 / norm
    hf = hf * inv
    hbs = jnp.where(row >= 1, pltpu.roll(hb, 1, 0), 0.0) * inv
    hs_ref[...] = (hf + hbs).astype(BF16)
    hd_ref[...] = (hf - hbs).astype(BF16)


def _filt_call(zf, w1p, b1, fr, w2, b2, w3, b3, absdelta):
    L = zf.shape[0]
    nct = HY_WIDTH // LANES
    H = HY_FILTER_HIDDEN
    full = lambda shape: pl.BlockSpec(shape, lambda j: (0, 0))
    return pl.pallas_call(
        _filt_kernel,
        grid=(nct,),
        in_specs=[full((L, LANES)), full((LANES, H)), full((1, H)), full((1, H)), full((H, H)), full((1, H)),
                  pl.BlockSpec((H, LANES), lambda j: (0, j)),
                  pl.BlockSpec((H, LANES), lambda j: (0, nct + j)),
                  pl.BlockSpec((1, LANES), lambda j: (0, j)),
                  pl.BlockSpec((1, LANES), lambda j: (0, nct + j)),
                  pl.BlockSpec((1, LANES), lambda j: (0, j))],
        out_specs=[pl.BlockSpec((L, LANES), lambda j: (0, j)),
                   pl.BlockSpec((L, LANES), lambda j: (0, j))],
        out_shape=[jax.ShapeDtypeStruct((L, HY_WIDTH), BF16),
                   jax.ShapeDtypeStruct((L, HY_WIDTH), BF16)],
        compiler_params=_params(("arbitrary",)),
        name="filt",
    )(zf, w1p, b1, fr, w2, b2, w3, w3, b3, b3, absdelta)


def _sconv_kernel(x0_ref, x1_ref, v_ref, w0_ref, w1_ref, w2_ref, b0_ref, b1_ref, b2_ref, u_ref, g_ref):
    L = x0_ref.shape[1]
    row = lax.broadcasted_iota(jnp.int32, (L, LANES), 0)

    def conv(z_ref, w_ref, b_ref):
        z = z_ref[0].astype(F32)
        w = w_ref[...]
        zp = jnp.where(row >= 1, pltpu.roll(z, 1, 0), 0.0)
        zn = jnp.where(row < L - 1, pltpu.roll(z, L - 1, 0), 0.0)
        return zp * w[0:1] + z * w[1:2] + zn * w[2:3] + b_ref[...]

    g_ref[0] = conv(x0_ref, w0_ref, b0_ref).astype(BF16)
    u_ref[0] = (conv(v_ref, w2_ref, b2_ref) * conv(x1_ref, w1_ref, b1_ref)).astype(BF16)


def _sconv_call(hy, conv_w, conv_b):
    B, L, _ = hy.shape
    nct = HY_WIDTH // LANES
    zs = lambda part: pl.BlockSpec((1, L, LANES), lambda b, j: (b, 0, part * nct + j))
    ws = lambda part: pl.BlockSpec((3, LANES), lambda b, j: (0, part * nct + j))
    bs = lambda part: pl.BlockSpec((1, LANES), lambda b, j: (0, part * nct + j))
    out = pl.BlockSpec((1, L, LANES), lambda b, j: (b, 0, j))
    return pl.pallas_call(
        _sconv_kernel,
        grid=(B, nct),
        in_specs=[zs(0), zs(1), zs(2), ws(0), ws(1), ws(2), bs(0), bs(1), bs(2)],
        out_specs=[out, out],
        out_shape=[jax.ShapeDtypeStruct((B, L, HY_WIDTH), BF16),
                   jax.ShapeDtypeStruct((B, L, HY_WIDTH), BF16)],
        compiler_params=_params(("parallel", "arbitrary")),
        name="sconv",
    )(hy, hy, hy, conv_w, conv_w, conv_w, conv_b, conv_b, conv_b)


def _lconv_kernel(u_ref, hs_ref, hd_ref, f_ref, skip_ref, y_ref, rhs_ref, acc_ref):
    nbat, L, _ = u_ref.shape
    j = pl.program_id(1)
    nj = pl.num_programs(1)
    tk = f_ref.shape[1]
    W = nbat * LANES

    @pl.when(j == 0)
    def _():
        for b in range(nbat):
            rhs_ref[:, b * LANES:(b + 1) * LANES] = u_ref[b]
        rhs_ref[:, W:W + LANES] = hs_ref[...]
        rhs_ref[:, W + LANES:W + 2 * LANES] = hd_ref[...]
        acc_ref[...] = jnp.zeros_like(acc_ref)

    fp = f_ref[0]
    fq = f_ref[1]
    ap = _dot(fp, rhs_ref[:, 0:W + LANES])
    aq = _dot(fq, rhs_ref[:, 0:W])
    kq = _dot(fq, rhs_ref[:, W + LANES:W + 2 * LANES])
    kq0 = _dot(fq[0:16], rhs_ref[:, W:W + LANES])[0:1]
    first = jnp.logical_and(lax.broadcasted_iota(jnp.int32, (tk, LANES), 0) == 0, j == 0)
    kp = ap[:, W:W + LANES]
    kq = jnp.where(first, kq0, kq)
    wgt = jnp.where(first, 0.5 / L, 1.0 / L)
    yps = []
    yqs = []
    for b in range(nbat):
        up = ap[:, b * LANES:(b + 1) * LANES]
        uq = aq[:, b * LANES:(b + 1) * LANES]
        yp = jnp.where(first, up * kp, up * kp - uq * kq)
        yq = jnp.where(first, uq * kq, up * kq + uq * kp)
        yps.append((yp * wgt).astype(BF16))
        yqs.append((yq * wgt).astype(BF16))
    yp = jnp.concatenate(yps, axis=1)
    yq = jnp.concatenate(yqs, axis=1)
    acc_ref[...] += _dot_tn(fp, yp) + _dot_tn(fq, yq)

    @pl.when(j == nj - 1)
    def _():
        for b in range(nbat):
            y = acc_ref[:, b * LANES:(b + 1) * LANES] + u_ref[b].astype(F32) * skip_ref[...]
            y_ref[b] = y.astype(BF16)


def _lconv_call(u, hs, hd, ftab, skip, tk):
    B, L, _ = u.shape
    nct = HY_WIDTH // LANES
    return pl.pallas_call(
        _lconv_kernel,
        grid=(nct, L // tk),
        in_specs=[pl.BlockSpec((B, L, LANES), lambda c, j: (0, 0, c)),
                  pl.BlockSpec((L, LANES), lambda c, j: (0, c)),
                  pl.BlockSpec((L, LANES), lambda c, j: (0, c)),
                  pl.BlockSpec((2, tk, L), lambda c, j: (0, j, 0)),
                  pl.BlockSpec((1, LANES), lambda c, j: (0, c))],
        out_specs=pl.BlockSpec((B, L, LANES), lambda c, j: (0, 0, c)),
        out_shape=jax.ShapeDtypeStruct((B, L, HY_WIDTH), BF16),
        scratch_shapes=[pltpu.VMEM((L, (B + 2) * LANES), BF16),
                        pltpu.VMEM((L, B * LANES), F32)],
        compiler_params=_params(("parallel", "arbitrary")),
        name="lconv",
    )(u, hs, hd, ftab, skip)


def _merge_kernel(x_ref, mod_ref, g1_ref, g2_ref, ya_ref, yc_ref, x0_ref, wg_ref, woa_ref, woh_ref, wout_ref,
                  wq_ref, x1_ref, h2_ref, qp_ref):
    m = mod_ref[0]
    x = x_ref[0]
    h = _norm_mod(x, g1_ref[...], m[0:1], m[1:2]).astype(BF16)
    D = x.shape[-1]
    gate_a = jax.nn.sigmoid(_dot(h, wg_ref[:, :D]))
    gate_h = jax.nn.sigmoid(_dot(h, wg_ref[:, D:]))
    y_hy = yc_ref[0] * x0_ref[0]
    merged = gate_a * _dot(ya_ref[0], woa_ref[...]) + gate_h * _dot(y_hy, woh_ref[...])
    x1 = x + m[2:3] * _dot(merged.astype(BF16), wout_ref[...])
    x1_ref[0] = x1
    h2 = _norm_mod(x1, g2_ref[...], m[3:4], m[4:5]).astype(BF16)
    h2_ref[0] = h2
    qp_ref[0] = _dot(h2, wq_ref[...])


def _merge_call(x, mod3, g1, g2, ya, yc, x0c, wg, woa, woh, wout, wq, tm):
    B, L, D = x.shape
    tok = lambda w: pl.BlockSpec((1, tm, w), lambda b, i: (b, i, 0))
    full = lambda a: pl.BlockSpec(a.shape, lambda b, i: (0, 0))
    return pl.pallas_call(
        _merge_kernel,
        grid=(B, L // tm),
        in_specs=[tok(D), pl.BlockSpec((1, 6, D), lambda b, i: (b, 0, 0)), full(g1), full(g2),
                  tok(Q_W), tok(HY_WIDTH), tok(HY_WIDTH), full(wg), full(woa), full(woh), full(wout), full(wq)],
        out_specs=[tok(D), tok(D), tok(D)],
        out_shape=[jax.ShapeDtypeStruct((B, L, D), F32),
                   jax.ShapeDtypeStruct((B, L, D), BF16),
                   jax.ShapeDtypeStruct((B, L, D), F32)],
        compiler_params=_params(("parallel", "arbitrary")),
        name="merge",
    )(x, mod3, g1, g2, ya, yc, x0c, wg, woa, woh, wout, wq)


def _topk_rank(s):
    n, t = s.shape
    rowi = lax.broadcasted_iota(jnp.int32, (n, t), 0).astype(F32)
    topi = lax.broadcasted_iota(jnp.int32, (PEER_TOPK, t), 0)

    def body(it, carry):
        s, rank, vals = carry
        mx = jnp.max(s, axis=0, keepdims=True)
        first = jnp.min(jnp.where(s == mx, rowi, float(n)), axis=0, keepdims=True)
        sel = rowi == first
        return (jnp.where(sel, NEG, s), jnp.where(sel, it.astype(F32), rank), jnp.where(topi == it, mx, vals))

    init = (s, jnp.full((n, t), float(PEER_TOPK), F32), jnp.zeros((PEER_TOPK, t), F32))
    _, rank, vals = lax.fori_loop(0, PEER_TOPK, body, init)
    return rank, vals


def _cand_blocks():
    blocks = []
    for c in range(PEER_TOPK // 2):
        valid = PEER_TOPK // (c + 1)
        blocks.append((c, -(-valid // 8) * 8, valid))
    return blocks


def _route_kernel(qp_ref, kb_ref, r2_ref, e2_ref, c1_ref, e1_ref):
    K = PEER_TOPK
    t = qp_ref.shape[0]
    blocks = _cand_blocks()
    fis = []
    for c, rows, valid in blocks:
        ri = lax.broadcasted_iota(jnp.int32, (rows, t), 0)
        fis.append(jnp.where(ri < valid, ri * K + c, K * K))
    ci = lax.broadcasted_iota(jnp.int32, (8, t), 0)
    fis.append(ci + K // 2)
    fi = jnp.concatenate(fis, axis=0).astype(F32)
    ncand = fi.shape[0]

    for hh in range(PEER_HEADS):
        q = qp_ref[:, hh * PEER_DKEY:(hh + 1) * PEER_DKEY]
        st = _dot3(kb_ref[hh], q, _dot_nt)
        s1 = st[:PEER_KEYS]
        s2 = st[PEER_KEYS:]
        rank1, a = _topk_rank(s1)
        rank2, b = _topk_rank(s2)
        cands = []
        for c, rows, valid in blocks:
            cands.append(a[0:rows] + b[c:c + 1])
        cands.append(a[0:1] + b[K // 2:K])
        cand = jnp.where(fi < K * K, jnp.concatenate(cands, axis=0), NEG)

        def body(it, carry):
            cand, sel_all = carry
            mx = jnp.max(cand, axis=0, keepdims=True)
            first = jnp.min(jnp.where(cand == mx, fi, float(K * K)), axis=0, keepdims=True)
            sel = fi == first
            return jnp.where(sel, NEG, cand), jnp.where(sel, 1.0, sel_all)

        _, sel = lax.fori_loop(0, K, body, (cand, jnp.zeros((ncand, t), F32)))
        ea = jnp.exp(a - a[0:1])
        eb = jnp.exp(b - b[0:1])
        cnt = jnp.zeros((K, t), F32)
        zsum = jnp.zeros((1, t), F32)
        off = 0
        for c, rows, valid in blocks:
            blk = sel[off:off + rows]
            off += rows
            if rows < K:
                blk_full = jnp.concatenate([blk, jnp.zeros((K - rows, t), F32)], axis=0)
            else:
                blk_full = blk
            cnt = cnt + blk_full
            zsum = zsum + jnp.sum(blk * ea[0:rows], axis=0, keepdims=True) * eb[c:c + 1]
        tail = sel[off:off + 8]
        tcount = jnp.sum(tail, axis=0, keepdims=True)
        row0 = lax.broadcasted_iota(jnp.int32, (K, t), 0) == 0
        cnt = cnt + jnp.where(row0, tcount, 0.0)
        zsum = zsum + jnp.sum(tail * eb[K // 2:K], axis=0, keepdims=True)
        c1 = jnp.zeros((PEER_KEYS, t), F32)
        for r in range(K):
            c1 = jnp.where(rank1 == r, cnt[r:r + 1], c1)
        e1 = jnp.where(rank1 < K, jnp.exp(s1 - a[0:1]), 0.0) / zsum
        e2 = jnp.where(rank2 < K, jnp.exp(s2 - b[0:1]), 0.0)
        r2_ref[hh] = rank2
        e2_ref[hh] = e2
        c1_ref[hh] = c1
        e1_ref[hh] = e1


def _route_call(qp, kb, tn):
    T = qp.shape[0]
    out = pl.BlockSpec((PEER_HEADS, PEER_KEYS, tn), lambda i: (0, 0, i))
    shp = jax.ShapeDtypeStruct((PEER_HEADS, PEER_KEYS, T), F32)
    return pl.pallas_call(
        _route_kernel,
        grid=(T // tn,),
        in_specs=[pl.BlockSpec((tn, PEER_HEADS * PEER_DKEY), lambda i: (i, 0)),
                  pl.BlockSpec(kb.shape, lambda i: (0, 0, 0))],
        out_specs=[out, out, out, out],
        out_shape=[shp, shp, shp, shp],
        compiler_params=_params(("parallel",)),
        name="route",
    )(qp, kb)


def _gelu_tanh(x):
    return 0.5 * x * (1.0 + jnp.tanh(math.sqrt(2.0 / math.pi) * (x + 0.044715 * (x * x * x))))


def _peer_kernel(h2_ref, u_ref, vt_ref, r2_ref, e2_ref, c1_ref, e1_ref, x1_ref, g2_ref, fg_ref, o_ref, acc_ref):
    e = pl.program_id(1)
    ne = pl.num_programs(1)
    te = u_ref.shape[0]

    @pl.when(e == 0)
    def _():
        acc_ref[...] = jnp.zeros_like(acc_ref)

    at = _dot_nt(u_ref[...], h2_ref[...])
    act = _gelu_tanh(at)
    ws = []
    for ii in range(te // PEER_KEYS):
        g = None
        for hh in range(PEER_HEADS):
            c1 = c1_ref[hh, ii:ii + 1, :]
            e1 = e1_ref[hh, ii:ii + 1, :]
            term = jnp.where(r2_ref[hh] < c1, e2_ref[hh] * e1, 0.0)
            g = term if g is None else g + term
        ws.append((g * act[ii * PEER_KEYS:(ii + 1) * PEER_KEYS]).astype(BF16))
    wt = jnp.concatenate(ws, axis=0)
    acc_ref[...] += _dot(vt_ref[...], wt)

    @pl.when(e == ne - 1)
    def _():
        x2 = x1_ref[...] + g2_ref[0] * acc_ref[...].T
        y = x2 * lax.rsqrt(jnp.mean(x2 * x2, axis=-1, keepdims=True) + NORM_EPS) * fg_ref[...]
        o_ref[...] = y


def _peer_call(h2, u_tab, vt_tab, r2, e2, c1, e1, x1, g2rows, fg, tn, te, toks_per_batch):
    T, D = h2.shape
    E = u_tab.shape[0]
    rows = te // PEER_KEYS
    tab = pl.BlockSpec((PEER_HEADS, PEER_KEYS, tn), lambda i, e: (0, 0, i))
    sel = pl.BlockSpec((PEER_HEADS, rows, tn), lambda i, e: (0, e, i))
    return pl.pallas_call(
        _peer_kernel,
        grid=(T // tn, E // te),
        in_specs=[pl.BlockSpec((tn, D), lambda i, e: (i, 0)),
                  pl.BlockSpec((te, D), lambda i, e: (e, 0)),
                  pl.BlockSpec((D, te), lambda i, e: (0, e)),
                  tab, tab, sel, sel,
                  pl.BlockSpec((tn, D), lambda i, e: (i, 0)),
                  pl.BlockSpec((1, 1, D), lambda i, e: ((i * tn) // toks_per_batch, 0, 0)),
                  pl.BlockSpec((1, D), lambda i, e: (0, 0))],
        out_specs=pl.BlockSpec((tn, D), lambda i, e: (i, 0)),
        out_shape=jax.ShapeDtypeStruct((T, D), F32),
        scratch_shapes=[pltpu.VMEM((D, tn), F32)],
        compiler_params=_params(("parallel", "arbitrary")),
        name="peer",
    )(h2, u_tab, vt_tab, r2, e2, c1, e1, x1, g2rows, fg)


def _rope_tables(L):
    rows = L // GRID_W
    row = jnp.repeat(jnp.arange(rows, dtype=jnp.int32), GRID_W).astype(F32)
    col = jnp.tile(jnp.arange(GRID_W, dtype=jnp.int32), rows).astype(F32)
    f = HEAD_DIM // 4
    inv = ROPE_BASE ** (-jnp.arange(f, dtype=F32) / f)
    ang_r = row[:, None] * inv[None, :]
    ang_c = col[:, None] * inv[None, :]
    ang = jnp.concatenate([ang_r, ang_r, ang_c, ang_c], axis=-1)
    reps = ROT_W // HEAD_DIM
    return jnp.tile(jnp.cos(ang), (1, reps)), jnp.tile(jnp.sin(ang), (1, reps))


def _rot_cols(w):
    n = w.shape[1]
    f = HEAD_DIM // 4
    d = jnp.arange(n) % (2 * f)
    src = jnp.where(d < f, jnp.arange(n) + f, jnp.arange(n) - f)
    sign = jnp.where(d < f, -1.0, 1.0).astype(w.dtype)
    return w[:, src] * sign[None, :]


def _dup_cols(w):
    return jnp.concatenate([w[:, :HEAD_DIM], w[:, :HEAD_DIM], w[:, HEAD_DIM:], w[:, HEAD_DIM:]], axis=1)


def _dft_table(L):
    N = 2 * L
    s = 64
    kh = jnp.arange(L // s, dtype=jnp.int32)
    kl = jnp.arange(s, dtype=jnp.int32)
    n = jnp.arange(L, dtype=jnp.int32)
    ph1 = ((kh[:, None] * s * n[None, :]) % N).astype(F32) * (2.0 * math.pi / N)
    ph2 = ((kl[:, None] * n[None, :]) % N).astype(F32) * (2.0 * math.pi / N)
    c1, s1 = jnp.cos(ph1)[:, None, :], jnp.sin(ph1)[:, None, :]
    c2, s2 = jnp.cos(ph2)[None, :, :], jnp.sin(ph2)[None, :, :]
    cosm = (c1 * c2 - s1 * s2).reshape(L, L)
    sinm = (s1 * c2 + c1 * s2).reshape(L, L)
    nyq = jnp.where(n % 2 == 0, 1.0, -1.0).astype(F32)
    k = jnp.arange(L, dtype=jnp.int32)
    q = jnp.where(k[:, None] == 0, nyq[None, :], -sinm)
    return jnp.stack([cosm, q]).astype(BF16)


def _filter_features(L):
    t = jnp.arange(L, dtype=F32) / L
    bands = jnp.arange(1, HY_EMB_BANDS + 1, dtype=F32)
    ang = 2.0 * math.pi * t[:, None] * bands[None, :]
    z = jnp.concatenate([t[:, None], jnp.cos(ang), jnp.sin(ang)], axis=-1)
    return jnp.pad(z, ((0, 0), (0, LANES - z.shape[1])))


def _tile(n, pref):
    return pref if n % pref == 0 else n


def kernel(x, c, ctx, c_ctx, w_mod, b_mod, norm1_g, w_in, attn_sink, hy_conv_w, hy_conv_b, hy_fw1, hy_fb1, hy_fw2, hy_fb2, hy_fw3, hy_fb3, hy_freq, hy_skip, w_o_attn, w_o_hy, w_out, norm2_g, peer_wq, peer_keys, peer_u, peer_v, final_g):
    B, L, D = x.shape
    assert B == 4 and D == D_MODEL and w_mod.shape[0] == 1
    T = B * L
    li = 0

    c8 = jnp.concatenate([c, c_ctx[None, :], jnp.zeros((3, D), F32)], axis=0)
    mod3 = _mod_call(c8, w_mod[li], b_mod[li]).reshape(8, 6, D)
    g1 = norm1_g[li].reshape(1, D)
    g2 = norm2_g[li].reshape(1, D)

    w = w_in[li]
    wq, wk, wv = w[:, :OFF_K], _dup_cols(w[:, OFF_K:OFF_V]), _dup_cols(w[:, OFF_V:OFF_HY])
    w_cat = jnp.concatenate([wq, wk, wv, w[:, OFF_HY:OFF_G], _rot_cols(wq), _rot_cols(wk)], axis=1).astype(BF16)
    cos_t, sin_t = _rope_tables(L)
    q, k, v, hy = _inproj_call(x, mod3, g1, w_cat, cos_t, sin_t, _tile(L, 512))
    kx, vx = _ctxproj_call(ctx, mod3, g1, jnp.concatenate([wk, wv], axis=1).astype(BF16))

    gsz = N_HEADS // N_KV_HEADS
    sink_b = jnp.broadcast_to(
        jnp.repeat(attn_sink[li].astype(F32).reshape(N_KV_HEADS, gsz), BLOCK, axis=1)[:, :, None],
        (N_KV_HEADS, gsz * BLOCK, LANES))
    y_attn = _attn_call(q, k, v, kx, vx, sink_b)

    H = HY_FILTER_HIDDEN
    w1p = jnp.pad(hy_fw1[li], ((0, LANES - hy_fw1.shape[1]), (0, 0)))
    deltas = jnp.abs(jnp.linspace(math.log(HY_DECAY_TARGET) / HY_SLOW_DECAY,
                                  math.log(HY_DECAY_TARGET) / HY_FAST_DECAY, HY_WIDTH, dtype=F32)).reshape(1, -1)
    hs, hd = _filt_call(_filter_features(L), w1p, hy_fb1[li].reshape(1, H), hy_freq[li].reshape(1, H),
                        hy_fw2[li], hy_fb2[li].reshape(1, H), hy_fw3[li], hy_fb3[li].reshape(1, -1), deltas)
    u, x0c = _sconv_call(hy, hy_conv_w[li], hy_conv_b[li].reshape(1, -1))
    yc = _lconv_call(u, hs, hd, _dft_table(L), hy_skip[li].reshape(1, -1), _tile(L, 256))

    x1, h2, qp = _merge_call(x, mod3, g1, g2, y_attn, yc, x0c, w[:, OFF_G:].astype(BF16),
                             w_o_attn[li].astype(BF16), w_o_hy[li].astype(BF16), w_out[li].astype(BF16),
                             peer_wq[li].astype(BF16), _tile(L, 256))

    keys = peer_keys[li]
    zk = jnp.zeros_like(keys[:, 0])
    kb = jnp.concatenate([jnp.concatenate([keys[:, 0], zk], axis=2),
                          jnp.concatenate([zk, keys[:, 1]], axis=2)], axis=1)
    r2, e2, c1, e1 = _route_call(qp.reshape(T, D), kb, _tile(T, 256))

    out = _peer_call(h2.reshape(T, D), peer_u[li].astype(BF16), peer_v[li].T.astype(BF16), r2, e2, c1, e1,
                     x1.reshape(T, D), mod3[:B, 5:6, :], final_g.reshape(1, D),
                     _tile(T, 512), 1024, L)
    return out.reshape(B, L, D)
```

```python
import functools
import math

import jax
import jax.numpy as jnp
from jax import lax
from jax.experimental import pallas as pl
from jax.experimental.pallas import tpu as pltpu

F32 = jnp.float32
BF16 = jnp.bfloat16

D_MODEL = 1024
GRID_W = 64
NORM_EPS = 1e-6
N_HEADS = 8
N_KV_HEADS = 2
HEAD_DIM = 64
BLOCK = 128
ROPE_BASE = 10000.0
HY_WIDTH = 512
HY_EMB_BANDS = 16
HY_FILTER_HIDDEN = 64
HY_FAST_DECAY = 0.3
HY_SLOW_DECAY = 1.5
HY_DECAY_TARGET = 1e-2
PEER_HEADS = 8
PEER_KEYS = 128
PEER_TOPK = 16
PEER_DKEY = 128
Q_W = N_HEADS * HEAD_DIM
KV_W = N_KV_HEADS * HEAD_DIM
HY_IN = 3 * HY_WIDTH
OFF_K = Q_W
OFF_V = OFF_K + KV_W
OFF_HY = OFF_V + KV_W
OFF_G = OFF_HY + HY_IN

LANES = 128
VMEM_LIMIT = 56 * 1024 * 1024
NEG = -1e30
KV_DUP = 2 * KV_W


def _params(sem):
    return pltpu.CompilerParams(dimension_semantics=sem, vmem_limit_bytes=VMEM_LIMIT)


def _dot(a, b):
    return lax.dot_general(a, b, (((1,), (0,)), ((), ())), preferred_element_type=F32)


def _dot_nt(a, b):
    return lax.dot_general(a, b, (((1,), (1,)), ((), ())), preferred_element_type=F32)


def _dot_tn(a, b):
    return lax.dot_general(a, b, (((0,), (0,)), ((), ())), preferred_element_type=F32)


def _split(a):
    hi = a.astype(BF16)
    lo = (a - hi.astype(F32)).astype(BF16)
    return hi, lo


def _dot3(a, b, dot=_dot):
    ah, al = _split(a)
    bh, bl = _split(b)
    return dot(ah, bh) + dot(ah, bl) + dot(al, bh)


def _norm_mod(x, g, shift, scale):
    y = x * lax.rsqrt(jnp.mean(x * x, axis=-1, keepdims=True) + NORM_EPS) * g
    return y * (1.0 + scale) + shift


def _mod_kernel(c_ref, w_ref, b_ref, o_ref):
    c = c_ref[...]
    s = c * jax.nn.sigmoid(c)
    o_ref[...] = _dot3(s, w_ref[...]) + b_ref[...]


def _mod_call(c8, w_mod, b_mod):
    n = w_mod.shape[1] // D_MODEL
    return pl.pallas_call(
        _mod_kernel,
        grid=(n,),
        in_specs=[pl.BlockSpec((8, D_MODEL), lambda j: (0, 0)),
                  pl.BlockSpec((D_MODEL, D_MODEL), lambda j: (0, j)),
                  pl.BlockSpec((1, D_MODEL), lambda j: (0, j))],
        out_specs=pl.BlockSpec((8, D_MODEL), lambda j: (0, j)),
        out_shape=jax.ShapeDtypeStruct((8, w_mod.shape[1]), F32),
        compiler_params=_params(("arbitrary",)),
        name="mod",
    )(c8, w_mod, b_mod.reshape(1, -1))


ROT_W = Q_W + KV_DUP
CAT_W = Q_W + 2 * KV_DUP + HY_IN + ROT_W


def _inproj_kernel(x_ref, mod_ref, g_ref, w_ref, cos_ref, sin_ref, q_ref, k_ref, v_ref, hy_ref):
    m = mod_ref[0]
    h = _norm_mod(x_ref[0], g_ref[...], m[0:1], m[1:2]).astype(BF16)
    o_v = ROT_W
    o_hy = o_v + KV_DUP
    o_rot = o_hy + HY_IN
    p = _dot(h, w_ref[:, 0:ROT_W])
    pr = _dot(h, w_ref[:, o_rot:o_rot + ROT_W])
    qk = p * cos_ref[...] + pr * sin_ref[...]
    q_ref[0] = qk[:, :Q_W].astype(BF16)
    k_ref[0] = qk[:, Q_W:].astype(BF16)
    v_ref[0] = _dot(h, w_ref[:, o_v:o_hy]).astype(BF16)
    for j in range(HY_IN // HY_WIDTH):
        lo = o_hy + j * HY_WIDTH
        hy_ref[0, :, j * HY_WIDTH:(j + 1) * HY_WIDTH] = _dot(h, w_ref[:, lo:lo + HY_WIDTH]).astype(BF16)


def _inproj_call(x, mod3, g, w_cat, cos_t, sin_t, tm):
    B, L, D = x.shape
    return pl.pallas_call(
        _inproj_kernel,
        grid=(B, L // tm),
        in_specs=[pl.BlockSpec((1, tm, D), lambda b, i: (b, i, 0)),
                  pl.BlockSpec((1, 6, D), lambda b, i: (b, 0, 0)),
                  pl.BlockSpec((1, D), lambda b, i: (0, 0)),
                  pl.BlockSpec((D, CAT_W), lambda b, i: (0, 0)),
                  pl.BlockSpec((tm, ROT_W), lambda b, i: (i, 0)),
                  pl.BlockSpec((tm, ROT_W), lambda b, i: (i, 0))],
        out_specs=[pl.BlockSpec((1, tm, Q_W), lambda b, i: (b, i, 0)),
                   pl.BlockSpec((1, tm, KV_DUP), lambda b, i: (b, i, 0)),
                   pl.BlockSpec((1, tm, KV_DUP), lambda b, i: (b, i, 0)),
                   pl.BlockSpec((1, tm, HY_IN), lambda b, i: (b, i, 0))],
        out_shape=[jax.ShapeDtypeStruct((B, L, Q_W), BF16),
                   jax.ShapeDtypeStruct((B, L, KV_DUP), BF16),
                   jax.ShapeDtypeStruct((B, L, KV_DUP), BF16),
                   jax.ShapeDtypeStruct((B, L, HY_IN), BF16)],
        compiler_params=_params(("parallel", "arbitrary")),
        name="inproj",
    )(x, mod3, g, w_cat, cos_t, sin_t)


def _ctxproj_kernel(x_ref, mod_ref, g_ref, w_ref, k_ref, v_ref):
    m = mod_ref[0]
    h = _norm_mod(x_ref[0], g_ref[...], m[0:1], m[1:2]).astype(BF16)
    k_ref[0] = _dot(h, w_ref[:, :KV_DUP]).astype(BF16)
    v_ref[0] = _dot(h, w_ref[:, KV_DUP:]).astype(BF16)


def _ctxproj_call(ctx, mod3, g, w_kv):
    B, C, D = ctx.shape
    return pl.pallas_call(
        _ctxproj_kernel,
        grid=(B,),
        in_specs=[pl.BlockSpec((1, C, D), lambda b: (b, 0, 0)),
                  pl.BlockSpec((1, 6, D), lambda b: (4, 0, 0)),
                  pl.BlockSpec((1, D), lambda b: (0, 0)),
                  pl.BlockSpec((D, 2 * KV_DUP), lambda b: (0, 0))],
        out_specs=[pl.BlockSpec((1, C, KV_DUP), lambda b: (b, 0, 0)),
                   pl.BlockSpec((1, C, KV_DUP), lambda b: (b, 0, 0))],
        out_shape=[jax.ShapeDtypeStruct((B, C, KV_DUP), BF16),
                   jax.ShapeDtypeStruct((B, C, KV_DUP), BF16)],
        compiler_params=_params(("arbitrary",)),
        name="ctxproj",
    )(ctx, mod3, g, w_kv)


def _attn_kernel(q_ref, kp_ref, kc_ref, kn_ref, vp_ref, vc_ref, vn_ref, kx_ref, vx_ref, sink_ref, o_ref):
    n = pl.program_id(1)
    nb = pl.num_programs(1)
    q = q_ref[0]
    rows = (N_HEADS // N_KV_HEADS) * BLOCK
    lo = lax.broadcasted_iota(jnp.int32, (BLOCK, LANES), 1) < HEAD_DIM
    r = lax.broadcasted_iota(jnp.int32, (rows, BLOCK), 0) % BLOCK
    c = lax.broadcasted_iota(jnp.int32, (rows, BLOCK), 1)
    ok_prev = jnp.logical_and(c >= r, n > 0)
    ok_next = jnp.logical_and(c <= r, n < nb - 1)
    scale = HEAD_DIM ** -0.5
    zero = jnp.zeros((BLOCK, LANES), BF16)
    for g in range(N_KV_HEADS):
        sl = slice(g * LANES, (g + 1) * LANES)
        qa = q[:, 2 * g * LANES:(2 * g + 1) * LANES]
        qb = q[:, (2 * g + 1) * LANES:(2 * g + 2) * LANES]
        lhs = jnp.concatenate([jnp.where(lo, qa, zero), jnp.where(lo, zero, qa),
                               jnp.where(lo, qb, zero), jnp.where(lo, zero, qb)], axis=0)
        s_p = jnp.where(ok_prev, _dot_nt(lhs, kp_ref[0, :, sl]) * scale, NEG)
        s_c = _dot_nt(lhs, kc_ref[0, :, sl]) * scale
        s_n = jnp.where(ok_next, _dot_nt(lhs, kn_ref[0, :, sl]) * scale, NEG)
        s_x = _dot_nt(lhs, kx_ref[0, :, sl]) * scale
        sink = sink_ref[g][:, 0:1]
        m = jnp.maximum(jnp.maximum(jnp.max(s_p, axis=-1, keepdims=True), jnp.max(s_c, axis=-1, keepdims=True)),
                        jnp.maximum(jnp.max(s_n, axis=-1, keepdims=True), jnp.max(s_x, axis=-1, keepdims=True)))
        m = jnp.maximum(m, sink)
        p_p = jnp.exp(s_p - m)
        p_c = jnp.exp(s_c - m)
        p_n = jnp.exp(s_n - m)
        p_x = jnp.exp(s_x - m)
        den = (jnp.sum(p_p, axis=-1, keepdims=True) + jnp.sum(p_c, axis=-1, keepdims=True)
               + jnp.sum(p_n, axis=-1, keepdims=True) + jnp.sum(p_x, axis=-1, keepdims=True)
               + jnp.exp(sink - m))
        o = (_dot(p_p.astype(BF16), vp_ref[0, :, sl]) + _dot(p_c.astype(BF16), vc_ref[0, :, sl])
             + _dot(p_n.astype(BF16), vn_ref[0, :, sl]) + _dot(p_x.astype(BF16), vx_ref[0, :, sl]))
        o = o / den
        o_ref[0, :, 2 * g * LANES:(2 * g + 1) * LANES] = jnp.where(
            lo, o[0:BLOCK], o[BLOCK:2 * BLOCK]).astype(BF16)
        o_ref[0, :, (2 * g + 1) * LANES:(2 * g + 2) * LANES] = jnp.where(
            lo, o[2 * BLOCK:3 * BLOCK], o[3 * BLOCK:4 * BLOCK]).astype(BF16)


def _attn_call(q, k, v, kx, vx, sink_b):
    B, L, _ = q.shape
    C = kx.shape[1]
    nb = L // BLOCK
    kv = lambda f: pl.BlockSpec((1, BLOCK, KV_DUP), f)
    prev = lambda b, n: (b, jnp.maximum(n - 1, 0), 0)
    cur = lambda b, n: (b, n, 0)
    nxt = lambda b, n: (b, jnp.minimum(n + 1, nb - 1), 0)
    rows = (N_HEADS // N_KV_HEADS) * BLOCK
    return pl.pallas_call(
        _attn_kernel,
        grid=(B, nb),
        in_specs=[pl.BlockSpec((1, BLOCK, Q_W), cur),
                  kv(prev), kv(cur), kv(nxt), kv(prev), kv(cur), kv(nxt),
                  pl.BlockSpec((1, C, KV_DUP), lambda b, n: (b, 0, 0)),
                  pl.BlockSpec((1, C, KV_DUP), lambda b, n: (b, 0, 0)),
                  pl.BlockSpec((N_KV_HEADS, rows, LANES), lambda b, n: (0, 0, 0))],
        out_specs=pl.BlockSpec((1, BLOCK, Q_W), cur),
        out_shape=jax.ShapeDtypeStruct((B, L, Q_W), BF16),
        compiler_params=_params(("parallel", "arbitrary")),
        name="attn",
    )(q, k, k, k, v, v, v, kx, vx, sink_b)


def _filt_kernel(z_ref, w1_ref, b1_ref, fr_ref, w2_ref, b2_ref, w3f_ref, w3b_ref, b3f_ref, b3b_ref,
                 dl_ref, hs_ref, hd_ref):
    L = z_ref.shape[0]
    z = z_ref[...]
    fr = fr_ref[...]
    h = jnp.sin(fr * (_dot3(z, w1_ref[...]) + b1_ref[...]))
    h = jnp.sin(fr * (_dot3(h, w2_ref[...]) + b2_ref[...]))
    decay = jnp.exp(-z[:, 0:1] * dl_ref[...])
    hf = (_dot3(h, w3f_ref[...]) + b3f_ref[...]) * decay
    hb = (_dot3(h, w3b_ref[...]) + b3b_ref[...]) * decay
    row = lax.broadcasted_iota(jnp.int32, hb.shape, 0)
    hb = jnp.where(row < L - 1, hb, 0.0)
    norm = jnp.sum(jnp.abs(hf), axis=0, keepdims=True) + jnp.sum(jnp.abs(hb), axis=0, keepdims=True)
    inv = 1.0 / norm
    hf = hf * inv
    hbs = jnp.where(row >= 1, pltpu.roll(hb, 1, 0), 0.0) * inv
    hs_ref[...] = (hf + hbs).astype(BF16)
    hd_ref[...] = (hf - hbs).astype(BF16)


def _filt_call(zf, w1p, b1, fr, w2, b2, w3, b3, absdelta):
    L = zf.shape[0]
    nct = HY_WIDTH // LANES
    H = HY_FILTER_HIDDEN
    full = lambda shape: pl.BlockSpec(shape, lambda j: (0, 0))
    return pl.pallas_call(
        _filt_kernel,
        grid=(nct,),
        in_specs=[full((L, LANES)), full((LANES, H)), full((1, H)), full((1, H)), full((H, H)), full((1, H)),
                  pl.BlockSpec((H, LANES), lambda j: (0, j)),
                  pl.BlockSpec((H, LANES), lambda j: (0, nct + j)),
                  pl.BlockSpec((1, LANES), lambda j: (0, j)),
                  pl.BlockSpec((1, LANES), lambda j: (0, nct + j)),
                  pl.BlockSpec((1, LANES), lambda j: (0, j))],
        out_specs=[pl.BlockSpec((L, LANES), lambda j: (0, j)),
                   pl.BlockSpec((L, LANES), lambda j: (0, j))],
        out_shape=[jax.ShapeDtypeStruct((L, HY_WIDTH), BF16),
                   jax.ShapeDtypeStruct((L, HY_WIDTH), BF16)],
        compiler_params=_params(("arbitrary",)),
        name="filt",
    )(zf, w1p, b1, fr, w2, b2, w3, w3, b3, b3, absdelta)


def _sconv_kernel(x0_ref, x1_ref, v_ref, w0_ref, w1_ref, w2_ref, b0_ref, b1_ref, b2_ref, u_ref, g_ref):
    L = x0_ref.shape[1]
    row = lax.broadcasted_iota(jnp.int32, (L, LANES), 0)

    def conv(z_ref, w_ref, b_ref):
        z = z_ref[0].astype(F32)
        w = w_ref[...]
        zp = jnp.where(row >= 1, pltpu.roll(z, 1, 0), 0.0)
        zn = jnp.where(row < L - 1, pltpu.roll(z, L - 1, 0), 0.0)
        return zp * w[0:1] + z * w[1:2] + zn * w[2:3] + b_ref[...]

    g_ref[0] = conv(x0_ref, w0_ref, b0_ref).astype(BF16)
    u_ref[0] = (conv(v_ref, w2_ref, b2_ref) * conv(x1_ref, w1_ref, b1_ref)).astype(BF16)


def _sconv_call(hy, conv_w, conv_b):
    B, L, _ = hy.shape
    nct = HY_WIDTH // LANES
    zs = lambda part: pl.BlockSpec((1, L, LANES), lambda b, j: (b, 0, part * nct + j))
    ws = lambda part: pl.BlockSpec((3, LANES), lambda b, j: (0, part * nct + j))
    bs = lambda part: pl.BlockSpec((1, LANES), lambda b, j: (0, part * nct + j))
    out = pl.BlockSpec((1, L, LANES), lambda b, j: (b, 0, j))
    return pl.pallas_call(
        _sconv_kernel,
        grid=(B, nct),
        in_specs=[zs(0), zs(1), zs(2), ws(0), ws(1), ws(2), bs(0), bs(1), bs(2)],
        out_specs=[out, out],
        out_shape=[jax.ShapeDtypeStruct((B, L, HY_WIDTH), BF16),
                   jax.ShapeDtypeStruct((B, L, HY_WIDTH), BF16)],
        compiler_params=_params(("parallel", "arbitrary")),
        name="sconv",
    )(hy, hy, hy, conv_w, conv_w, conv_w, conv_b, conv_b, conv_b)


def _lconv_kernel(u_ref, hs_ref, hd_ref, f_ref, skip_ref, y_ref, rhs_ref, acc_ref):
    nbat, L, _ = u_ref.shape
    j = pl.program_id(1)
    nj = pl.num_programs(1)
    tk = f_ref.shape[1]
    W = nbat * LANES

    @pl.when(j == 0)
    def _():
        for b in range(nbat):
            rhs_ref[:, b * LANES:(b + 1) * LANES] = u_ref[b]
        rhs_ref[:, W:W + LANES] = hs_ref[...]
        rhs_ref[:, W + LANES:W + 2 * LANES] = hd_ref[...]
        acc_ref[...] = jnp.zeros_like(acc_ref)

    fp = f_ref[0]
    fq = f_ref[1]
    ap = _dot(fp, rhs_ref[:, 0:W + LANES])
    aq = _dot(fq, rhs_ref[:, 0:W])
    kq = _dot(fq, rhs_ref[:, W + LANES:W + 2 * LANES])
    kq0 = _dot(fq[0:16], rhs_ref[:, W:W + LANES])[0:1]
    first = jnp.logical_and(lax.broadcasted_iota(jnp.int32, (tk, LANES), 0) == 0, j == 0)
    kp = ap[:, W:W + LANES]
    kq = jnp.where(first, kq0, kq)
    wgt = jnp.where(first, 0.5 / L, 1.0 / L)
    yps = []
    yqs = []
    for b in range(nbat):
        up = ap[:, b * LANES:(b + 1) * LANES]
        uq = aq[:, b * LANES:(b + 1) * LANES]
        yp = jnp.where(first, up * kp, up * kp - uq * kq)
        yq = jnp.where(first, uq * kq, up * kq + uq * kp)
        yps.append((yp * wgt).astype(BF16))
        yqs.append((yq * wgt).astype(BF16))
    yp = jnp.concatenate(yps, axis=1)
    yq = jnp.concatenate(yqs, axis=1)
    acc_ref[...] += _dot_tn(fp, yp) + _dot_tn(fq, yq)

    @pl.when(j == nj - 1)
    def _():
        for b in range(nbat):
            y = acc_ref[:, b * LANES:(b + 1) * LANES] + u_ref[b].astype(F32) * skip_ref[...]
            y_ref[b] = y.astype(BF16)


def _lconv_call(u, hs, hd, ftab, skip, tk):
    B, L, _ = u.shape
    nct = HY_WIDTH // LANES
    return pl.pallas_call(
        _lconv_kernel,
        grid=(nct, L // tk),
        in_specs=[pl.BlockSpec((B, L, LANES), lambda c, j: (0, 0, c)),
                  pl.BlockSpec((L, LANES), lambda c, j: (0, c)),
                  pl.BlockSpec((L, LANES), lambda c, j: (0, c)),
                  pl.BlockSpec((2, tk, L), lambda c, j: (0, j, 0)),
                  pl.BlockSpec((1, LANES), lambda c, j: (0, c))],
        out_specs=pl.BlockSpec((B, L, LANES), lambda c, j: (0, 0, c)),
        out_shape=jax.ShapeDtypeStruct((B, L, HY_WIDTH), BF16),
        scratch_shapes=[pltpu.VMEM((L, (B + 2) * LANES), BF16),
                        pltpu.VMEM((L, B * LANES), F32)],
        compiler_params=_params(("parallel", "arbitrary")),
        name="lconv",
    )(u, hs, hd, ftab, skip)


def _merge_kernel(x_ref, mod_ref, g1_ref, g2_ref, ya_ref, yc_ref, x0_ref, wg_ref, woa_ref, woh_ref, wout_ref,
                  wq_ref, x1_ref, h2t_ref, qp_ref):
    m = mod_ref[0]
    x = x_ref[0]
    h = _norm_mod(x, g1_ref[...], m[0:1], m[1:2]).astype(BF16)
    D = x.shape[-1]
    gate_a = jax.nn.sigmoid(_dot(h, wg_ref[:, :D]))
    gate_h = jax.nn.sigmoid(_dot(h, wg_ref[:, D:]))
    y_hy = yc_ref[0] * x0_ref[0]
    merged = gate_a * _dot(ya_ref[0], woa_ref[...]) + gate_h * _dot(y_hy, woh_ref[...])
    x1 = x + m[2:3] * _dot(merged.astype(BF16), wout_ref[...])
    x1_ref[0] = x1
    h2 = _norm_mod(x1, g2_ref[...], m[3:4], m[4:5])
    h2t_ref[...] = h2.T.astype(BF16)
    qp_ref[0] = _dot(h2.astype(BF16), wq_ref[...])


def _merge_call(x, mod3, g1, g2, ya, yc, x0c, wg, woa, woh, wout, wq, tm):
    B, L, D = x.shape
    tok = lambda w: pl.BlockSpec((1, tm, w), lambda b, i: (b, i, 0))
    full = lambda a: pl.BlockSpec(a.shape, lambda b, i: (0, 0))
    return pl.pallas_call(
        _merge_kernel,
        grid=(B, L // tm),
        in_specs=[tok(D), pl.BlockSpec((1, 6, D), lambda b, i: (b, 0, 0)), full(g1), full(g2),
                  tok(Q_W), tok(HY_WIDTH), tok(HY_WIDTH), full(wg), full(woa), full(woh), full(wout), full(wq)],
        out_specs=[tok(D), pl.BlockSpec((D, tm), lambda b, i: (0, b * (L // tm) + i)), tok(D)],
        out_shape=[jax.ShapeDtypeStruct((B, L, D), F32),
                   jax.ShapeDtypeStruct((D, B * L), BF16),
                   jax.ShapeDtypeStruct((B, L, D), F32)],
        compiler_params=_params(("parallel", "arbitrary")),
        name="merge",
    )(x, mod3, g1, g2, ya, yc, x0c, wg, woa, woh, wout, wq)


def _topk_rank(s):
    n, t = s.shape
    rowi = lax.broadcasted_iota(jnp.int32, (n, t), 0).astype(F32)
    topi = lax.broadcasted_iota(jnp.int32, (PEER_TOPK, t), 0)

    def body(it, carry):
        s, rank, vals = carry
        mx = jnp.max(s, axis=0, keepdims=True)
        first = jnp.min(jnp.where(s == mx, rowi, float(n)), axis=0, keepdims=True)
        sel = rowi == first
        return (jnp.where(sel, NEG, s), jnp.where(sel, it.astype(F32), rank), jnp.where(topi == it, mx, vals))

    init = (s, jnp.full((n, t), float(PEER_TOPK), F32), jnp.zeros((PEER_TOPK, t), F32))
    _, rank, vals = lax.fori_loop(0, PEER_TOPK, body, init)
    return rank, vals


def _cand_blocks():
    blocks = []
    for c in range(PEER_TOPK // 2):
        valid = PEER_TOPK // (c + 1)
        blocks.append((c, -(-valid // 8) * 8, valid))
    return blocks


def _route_kernel(qp_ref, kb_ref, r2_ref, e2_ref, c1_ref, e1_ref):
    K = PEER_TOPK
    t = qp_ref.shape[0]
    blocks = _cand_blocks()
    fis = []
    for c, rows, valid in blocks:
        ri = lax.broadcasted_iota(jnp.int32, (rows, t), 0)
        fis.append(jnp.where(ri < valid, ri * K + c, K * K))
    ci = lax.broadcasted_iota(jnp.int32, (8, t), 0)
    fis.append(ci + K // 2)
    fi = jnp.concatenate(fis, axis=0).astype(F32)
    ncand = fi.shape[0]

    for hh in range(PEER_HEADS):
        q = qp_ref[:, hh * PEER_DKEY:(hh + 1) * PEER_DKEY]
        st = _dot3(kb_ref[hh], q, _dot_nt)
        s1 = st[:PEER_KEYS]
        s2 = st[PEER_KEYS:]
        rank1, a = _topk_rank(s1)
        rank2, b = _topk_rank(s2)
        cands = []
        for c, rows, valid in blocks:
            cands.append(a[0:rows] + b[c:c + 1])
        cands.append(a[0:1] + b[K // 2:K])
        cand = jnp.where(fi < K * K, jnp.concatenate(cands, axis=0), NEG)

        def body(it, carry):
            cand, sel_all = carry
            mx = jnp.max(cand, axis=0, keepdims=True)
            first = jnp.min(jnp.where(cand == mx, fi, float(K * K)), axis=0, keepdims=True)
            sel = fi == first
            return jnp.where(sel, NEG, cand), jnp.where(sel, 1.0, sel_all)

        _, sel = lax.fori_loop(0, K, body, (cand, jnp.zeros((ncand, t), F32)))
        ea = jnp.exp(a - a[0:1])
        eb = jnp.exp(b - b[0:1])
        cnt = jnp.zeros((K, t), F32)
        zsum = jnp.zeros((1, t), F32)
        off = 0
        for c, rows, valid in blocks:
            blk = sel[off:off + rows]
            off += rows
            if rows < K:
                blk_full = jnp.concatenate([blk, jnp.zeros((K - rows, t), F32)], axis=0)
            else:
                blk_full = blk
            cnt = cnt + blk_full
            zsum = zsum + jnp.sum(blk * ea[0:rows], axis=0, keepdims=True) * eb[c:c + 1]
        tail = sel[off:off + 8]
        tcount = jnp.sum(tail, axis=0, keepdims=True)
        row0 = lax.broadcasted_iota(jnp.int32, (K, t), 0) == 0
        cnt = cnt + jnp.where(row0, tcount, 0.0)
        zsum = zsum + jnp.sum(tail * eb[K // 2:K], axis=0, keepdims=True)
        c1 = jnp.zeros((PEER_KEYS, t), F32)
        for r in range(K):
            c1 = jnp.where(rank1 == r, cnt[r:r + 1], c1)
        e1 = jnp.where(rank1 < K, jnp.exp(s1 - a[0:1]), 0.0) * (0.5 / zsum)
        e2 = jnp.where(rank2 < K, jnp.exp(s2 - b[0:1]), 0.0)
        r2_ref[hh] = rank2.astype(BF16)
        e2_ref[hh] = e2.astype(BF16)
        c1_ref[hh] = c1
        e1_ref[hh] = e1


def _route_call(qp, kb, tn):
    T = qp.shape[0]
    out = pl.BlockSpec((PEER_HEADS, PEER_KEYS, tn), lambda i: (0, 0, i))
    shp = jax.ShapeDtypeStruct((PEER_HEADS, PEER_KEYS, T), F32)
    shp16 = jax.ShapeDtypeStruct((PEER_HEADS, PEER_KEYS, T), BF16)
    return pl.pallas_call(
        _route_kernel,
        grid=(T // tn,),
        in_specs=[pl.BlockSpec((tn, PEER_HEADS * PEER_DKEY), lambda i: (i, 0)),
                  pl.BlockSpec(kb.shape, lambda i: (0, 0, 0))],
        out_specs=[out, out, out, out],
        out_shape=[shp16, shp16, shp, shp],
        compiler_params=_params(("parallel",)),
        name="route",
    )(qp, kb)


PEER_SUB = 256
ROWS16 = 16


def _gelu_half(x):
    k = math.sqrt(2.0 / math.pi)
    return x + x * jnp.tanh(x * (k + (k * 0.044715) * (x * x)))


def _peer_kernel(h2_ref, u_ref, vt_ref, r2_ref, e2_ref, c1_ref, e1_ref, x1_ref, g2_ref, fg_ref, o_ref, acc_ref,
                 w_ref, at_ref):
    e = pl.program_id(1)
    ne = pl.num_programs(1)
    te = u_ref.shape[0]

    @pl.when(e == 0)
    def _():
        acc_ref[...] = jnp.zeros_like(acc_ref)

    zero = jnp.zeros((), BF16)
    nsub = h2_ref.shape[1] // PEER_SUB
    toks = [slice(th * PEER_SUB, (th + 1) * PEER_SUB) for th in range(nsub)]

    def gates(th):
        tok = toks[th]
        for ii in range(te // PEER_KEYS):
            gs = [None] * (PEER_KEYS // ROWS16)
            for hh in range(PEER_HEADS):
                c1 = jnp.broadcast_to(c1_ref[hh, ii:ii + 1, tok], (ROWS16, PEER_SUB)).astype(BF16)
                e1 = jnp.broadcast_to(e1_ref[hh, ii:ii + 1, tok], (ROWS16, PEER_SUB)).astype(BF16)
                for k in range(PEER_KEYS // ROWS16):
                    rows = slice(k * ROWS16, (k + 1) * ROWS16)
                    term = jnp.where(r2_ref[hh, rows, tok] < c1, e2_ref[hh, rows, tok], zero) * e1
                    gs[k] = term if gs[k] is None else gs[k] + term
            for k in range(PEER_KEYS // ROWS16):
                lo = ii * PEER_KEYS + k * ROWS16
                w_ref[th, lo:lo + ROWS16, :] = gs[k]

    def project(th):
        at_ref[th] = _dot(u_ref[...], h2_ref[:, toks[th]])

    def activate(th):
        w_ref[th] = w_ref[th] * _gelu_half(at_ref[th]).astype(BF16)

    def combine(th):
        acc_ref[:, toks[th]] += _dot(vt_ref[...], w_ref[th])

    gates(0)
    project(0)
    for th in range(nsub):
        if th + 1 < nsub:
            gates(th + 1)
        activate(th)
        if th + 1 < nsub:
            project(th + 1)
        combine(th)

    @pl.when(e == ne - 1)
    def _():
        x2 = x1_ref[...] + g2_ref[0] * acc_ref[...].T
        y = x2 * lax.rsqrt(jnp.mean(x2 * x2, axis=-1, keepdims=True) + NORM_EPS) * fg_ref[...]
        o_ref[...] = y


def _peer_call(h2, u_tab, vt_tab, r2, e2, c1, e1, x1, g2rows, fg, tn, te, toks_per_batch):
    D, T = h2.shape
    E = u_tab.shape[0]
    rows = te // PEER_KEYS
    tab = pl.BlockSpec((PEER_HEADS, PEER_KEYS, tn), lambda i, e: (0, 0, i))
    sel = pl.BlockSpec((PEER_HEADS, rows, tn), lambda i, e: (0, e, i))
    return pl.pallas_call(
        _peer_kernel,
        grid=(T // tn, E // te),
        in_specs=[pl.BlockSpec((D, tn), lambda i, e: (0, i)),
                  pl.BlockSpec((te, D), lambda i, e: (e, 0)),
                  pl.BlockSpec((D, te), lambda i, e: (0, e)),
                  tab, tab, sel, sel,
                  pl.BlockSpec((tn, D), lambda i, e: (i, 0)),
                  pl.BlockSpec((1, 1, D), lambda i, e: ((i * tn) // toks_per_batch, 0, 0)),
                  pl.BlockSpec((1, D), lambda i, e: (0, 0))],
        out_specs=pl.BlockSpec((tn, D), lambda i, e: (i, 0)),
        out_shape=jax.ShapeDtypeStruct((T, D), F32),
        scratch_shapes=[pltpu.VMEM((D, tn), F32), pltpu.VMEM((tn // PEER_SUB, te, PEER_SUB), BF16),
                        pltpu.VMEM((tn // PEER_SUB, te, PEER_SUB), F32)],
        compiler_params=_params(("parallel", "arbitrary")),
        name="peer",
    )(h2, u_tab, vt_tab, r2, e2, c1, e1, x1, g2rows, fg)


def _rope_tables(L):
    rows = L // GRID_W
    row = jnp.repeat(jnp.arange(rows, dtype=jnp.int32), GRID_W).astype(F32)
    col = jnp.tile(jnp.arange(GRID_W, dtype=jnp.int32), rows).astype(F32)
    f = HEAD_DIM // 4
    inv = ROPE_BASE ** (-jnp.arange(f, dtype=F32) / f)
    ang_r = row[:, None] * inv[None, :]
    ang_c = col[:, None] * inv[None, :]
    ang = jnp.concatenate([ang_r, ang_r, ang_c, ang_c], axis=-1)
    reps = ROT_W // HEAD_DIM
    return jnp.tile(jnp.cos(ang), (1, reps)), jnp.tile(jnp.sin(ang), (1, reps))


def _rot_cols(w):
    n = w.shape[1]
    f = HEAD_DIM // 4
    d = jnp.arange(n) % (2 * f)
    src = jnp.where(d < f, jnp.arange(n) + f, jnp.arange(n) - f)
    sign = jnp.where(d < f, -1.0, 1.0).astype(w.dtype)
    return w[:, src] * sign[None, :]


def _dup_cols(w):
    return jnp.concatenate([w[:, :HEAD_DIM], w[:, :HEAD_DIM], w[:, HEAD_DIM:], w[:, HEAD_DIM:]], axis=1)


def _dft_table(L):
    N = 2 * L
    s = 64
    kh = jnp.arange(L // s, dtype=jnp.int32)
    kl = jnp.arange(s, dtype=jnp.int32)
    n = jnp.arange(L, dtype=jnp.int32)
    ph1 = ((kh[:, None] * s * n[None, :]) % N).astype(F32) * (2.0 * math.pi / N)
    ph2 = ((kl[:, None] * n[None, :]) % N).astype(F32) * (2.0 * math.pi / N)
    c1, s1 = jnp.cos(ph1)[:, None, :], jnp.sin(ph1)[:, None, :]
    c2, s2 = jnp.cos(ph2)[None, :, :], jnp.sin(ph2)[None, :, :]
    cosm = (c1 * c2 - s1 * s2).reshape(L, L)
    sinm = (s1 * c2 + c1 * s2).reshape(L, L)
    nyq = jnp.where(n % 2 == 0, 1.0, -1.0).astype(F32)
    k = jnp.arange(L, dtype=jnp.int32)
    q = jnp.where(k[:, None] == 0, nyq[None, :], -sinm)
    return jnp.stack([cosm, q]).astype(BF16)


def _filter_features(L):
    t = jnp.arange(L, dtype=F32) / L
    bands = jnp.arange(1, HY_EMB_BANDS + 1, dtype=F32)
    ang = 2.0 * math.pi * t[:, None] * bands[None, :]
    z = jnp.concatenate([t[:, None], jnp.cos(ang), jnp.sin(ang)], axis=-1)
    return jnp.pad(z, ((0, 0), (0, LANES - z.shape[1])))


def _tile(n, pref):
    return pref if n % pref == 0 else n


def kernel(x, c, ctx, c_ctx, w_mod, b_mod, norm1_g, w_in, attn_sink, hy_conv_w, hy_conv_b, hy_fw1, hy_fb1, hy_fw2, hy_fb2, hy_fw3, hy_fb3, hy_freq, hy_skip, w_o_attn, w_o_hy, w_out, norm2_g, peer_wq, peer_keys, peer_u, peer_v, final_g):
    B, L, D = x.shape
    assert B == 4 and D == D_MODEL and w_mod.shape[0] == 1
    T = B * L
    li = 0

    c8 = jnp.concatenate([c, c_ctx[None, :], jnp.zeros((3, D), F32)], axis=0)
    mod3 = _mod_call(c8, w_mod[li], b_mod[li]).reshape(8, 6, D)
    g1 = norm1_g[li].reshape(1, D)
    g2 = norm2_g[li].reshape(1, D)

    w = w_in[li]
    wq, wk, wv = w[:, :OFF_K], _dup_cols(w[:, OFF_K:OFF_V]), _dup_cols(w[:, OFF_V:OFF_HY])
    w_cat = jnp.concatenate([wq, wk, wv, w[:, OFF_HY:OFF_G], _rot_cols(wq), _rot_cols(wk)], axis=1).astype(BF16)
    cos_t, sin_t = _rope_tables(L)
    q, k, v, hy = _inproj_call(x, mod3, g1, w_cat, cos_t, sin_t, _tile(L, 512))
    kx, vx = _ctxproj_call(ctx, mod3, g1, jnp.concatenate([wk, wv], axis=1).astype(BF16))

    gsz = N_HEADS // N_KV_HEADS
    sink_b = jnp.broadcast_to(
        jnp.repeat(attn_sink[li].astype(F32).reshape(N_KV_HEADS, gsz), BLOCK, axis=1)[:, :, None],
        (N_KV_HEADS, gsz * BLOCK, LANES))
    y_attn = _attn_call(q, k, v, kx, vx, sink_b)

    H = HY_FILTER_HIDDEN
    w1p = jnp.pad(hy_fw1[li], ((0, LANES - hy_fw1.shape[1]), (0, 0)))
    deltas = jnp.abs(jnp.linspace(math.log(HY_DECAY_TARGET) / HY_SLOW_DECAY,
                                  math.log(HY_DECAY_TARGET) / HY_FAST_DECAY, HY_WIDTH, dtype=F32)).reshape(1, -1)
    hs, hd = _filt_call(_filter_features(L), w1p, hy_fb1[li].reshape(1, H), hy_freq[li].reshape(1, H),
                        hy_fw2[li], hy_fb2[li].reshape(1, H), hy_fw3[li], hy_fb3[li].reshape(1, -1), deltas)
    u, x0c = _sconv_call(hy, hy_conv_w[li], hy_conv_b[li].reshape(1, -1))
    yc = _lconv_call(u, hs, hd, _dft_table(L), hy_skip[li].reshape(1, -1), _tile(L, 256))

    x1, h2t, qp = _merge_call(x, mod3, g1, g2, y_attn, yc, x0c, w[:, OFF_G:].astype(BF16),
                             w_o_attn[li].astype(BF16), w_o_hy[li].astype(BF16), w_out[li].astype(BF16),
                             peer_wq[li].astype(BF16), _tile(L, 256))

    keys = peer_keys[li]
    zk = jnp.zeros_like(keys[:, 0])
    kb = jnp.concatenate([jnp.concatenate([keys[:, 0], zk], axis=2),
                          jnp.concatenate([zk, keys[:, 1]], axis=2)], axis=1)
    r2, e2, c1, e1 = _route_call(qp.reshape(T, D), kb, _tile(T, 256))

    out = _peer_call(h2t, peer_u[li].astype(BF16), peer_v[li].T.astype(BF16), r2, e2, c1, e1,
                     x1.reshape(T, D), mod3[:B, 5:6, :], final_g.reshape(1, D),
                     _tile(T, 1024), 1024, L)
    return out.reshape(B, L, D)
```

```python
import functools
import math

import jax
import jax.numpy as jnp
from jax import lax
from jax.experimental import pallas as pl
from jax.experimental.pallas import tpu as pltpu

F32 = jnp.float32
BF16 = jnp.bfloat16

D_MODEL = 1024
GRID_W = 64
NORM_EPS = 1e-6
N_HEADS = 8
N_KV_HEADS = 2
HEAD_DIM = 64
BLOCK = 128
ROPE_BASE = 10000.0
HY_WIDTH = 512
HY_EMB_BANDS = 16
HY_FILTER_HIDDEN = 64
HY_FAST_DECAY = 0.3
HY_SLOW_DECAY = 1.5
HY_DECAY_TARGET = 1e-2
PEER_HEADS = 8
PEER_KEYS = 128
PEER_TOPK = 16
PEER_DKEY = 128
Q_W = N_HEADS * HEAD_DIM
KV_W = N_KV_HEADS * HEAD_DIM
HY_IN = 3 * HY_WIDTH
OFF_K = Q_W
OFF_V = OFF_K + KV_W
OFF_HY = OFF_V + KV_W
OFF_G = OFF_HY + HY_IN

LANES = 128
VMEM_LIMIT = 56 * 1024 * 1024
NEG = -1e30
KV_DUP = 2 * KV_W


def _params(sem):
    return pltpu.CompilerParams(dimension_semantics=sem, vmem_limit_bytes=VMEM_LIMIT)


def _dot(a, b):
    return lax.dot_general(a, b, (((1,), (0,)), ((), ())), preferred_element_type=F32)


def _dot_nt(a, b):
    return lax.dot_general(a, b, (((1,), (1,)), ((), ())), preferred_element_type=F32)


def _dot_tn(a, b):
    return lax.dot_general(a, b, (((0,), (0,)), ((), ())), preferred_element_type=F32)


def _split(a):
    hi = a.astype(BF16)
    lo = (a - hi.astype(F32)).astype(BF16)
    return hi, lo


def _dot3(a, b, dot=_dot):
    ah, al = _split(a)
    bh, bl = _split(b)
    return dot(ah, bh) + dot(ah, bl) + dot(al, bh)


def _norm_mod(x, g, shift, scale):
    y = x * lax.rsqrt(jnp.mean(x * x, axis=-1, keepdims=True) + NORM_EPS) * g
    return y * (1.0 + scale) + shift


def _mod_kernel(c_ref, w_ref, b_ref, o_ref):
    c = c_ref[...]
    s = c * jax.nn.sigmoid(c)
    o_ref[...] = _dot3(s, w_ref[...]) + b_ref[...]


def _mod_call(c8, w_mod, b_mod):
    n = w_mod.shape[1] // D_MODEL
    return pl.pallas_call(
        _mod_kernel,
        grid=(n,),
        in_specs=[pl.BlockSpec((8, D_MODEL), lambda j: (0, 0)),
                  pl.BlockSpec((D_MODEL, D_MODEL), lambda j: (0, j)),
                  pl.BlockSpec((1, D_MODEL), lambda j: (0, j))],
        out_specs=pl.BlockSpec((8, D_MODEL), lambda j: (0, j)),
        out_shape=jax.ShapeDtypeStruct((8, w_mod.shape[1]), F32),
        compiler_params=_params(("arbitrary",)),
        name="mod",
    )(c8, w_mod, b_mod.reshape(1, -1))


ROT_W = Q_W + KV_DUP
CAT_W = Q_W + 2 * KV_DUP + HY_IN + ROT_W


def _inproj_kernel(x_ref, mod_ref, g_ref, w_ref, cos_ref, sin_ref, q_ref, k_ref, v_ref, hy_ref):
    m = mod_ref[0]
    h = _norm_mod(x_ref[0], g_ref[...], m[0:1], m[1:2]).astype(BF16)
    o_v = ROT_W
    o_hy = o_v + KV_DUP
    o_rot = o_hy + HY_IN
    p = _dot(h, w_ref[:, 0:ROT_W])
    pr = _dot(h, w_ref[:, o_rot:o_rot + ROT_W])
    qk = p * cos_ref[...] + pr * sin_ref[...]
    q_ref[0] = qk[:, :Q_W].astype(BF16)
    k_ref[0] = qk[:, Q_W:].astype(BF16)
    v_ref[0] = _dot(h, w_ref[:, o_v:o_hy]).astype(BF16)
    for j in range(HY_IN // HY_WIDTH):
        lo = o_hy + j * HY_WIDTH
        hy_ref[0, :, j * HY_WIDTH:(j + 1) * HY_WIDTH] = _dot(h, w_ref[:, lo:lo + HY_WIDTH]).astype(BF16)


def _inproj_call(x, mod3, g, w_cat, cos_t, sin_t, tm):
    B, L, D = x.shape
    return pl.pallas_call(
        _inproj_kernel,
        grid=(B, L // tm),
        in_specs=[pl.BlockSpec((1, tm, D), lambda b, i: (b, i, 0)),
                  pl.BlockSpec((1, 6, D), lambda b, i: (b, 0, 0)),
                  pl.BlockSpec((1, D), lambda b, i: (0, 0)),
                  pl.BlockSpec((D, CAT_W), lambda b, i: (0, 0)),
                  pl.BlockSpec((tm, ROT_W), lambda b, i: (i, 0)),
                  pl.BlockSpec((tm, ROT_W), lambda b, i: (i, 0))],
        out_specs=[pl.BlockSpec((1, tm, Q_W), lambda b, i: (b, i, 0)),
                   pl.BlockSpec((1, tm, KV_DUP), lambda b, i: (b, i, 0)),
                   pl.BlockSpec((1, tm, KV_DUP), lambda b, i: (b, i, 0)),
                   pl.BlockSpec((1, tm, HY_IN), lambda b, i: (b, i, 0))],
        out_shape=[jax.ShapeDtypeStruct((B, L, Q_W), BF16),
                   jax.ShapeDtypeStruct((B, L, KV_DUP), BF16),
                   jax.ShapeDtypeStruct((B, L, KV_DUP), BF16),
                   jax.ShapeDtypeStruct((B, L, HY_IN), BF16)],
        compiler_params=_params(("parallel", "arbitrary")),
        name="inproj",
    )(x, mod3, g, w_cat, cos_t, sin_t)


def _ctxproj_kernel(x_ref, mod_ref, g_ref, w_ref, k_ref, v_ref):
    m = mod_ref[0]
    h = _norm_mod(x_ref[0], g_ref[...], m[0:1], m[1:2]).astype(BF16)
    k_ref[0] = _dot(h, w_ref[:, :KV_DUP]).astype(BF16)
    v_ref[0] = _dot(h, w_ref[:, KV_DUP:]).astype(BF16)


def _ctxproj_call(ctx, mod3, g, w_kv):
    B, C, D = ctx.shape
    return pl.pallas_call(
        _ctxproj_kernel,
        grid=(B,),
        in_specs=[pl.BlockSpec((1, C, D), lambda b: (b, 0, 0)),
                  pl.BlockSpec((1, 6, D), lambda b: (4, 0, 0)),
                  pl.BlockSpec((1, D), lambda b: (0, 0)),
                  pl.BlockSpec((D, 2 * KV_DUP), lambda b: (0, 0))],
        out_specs=[pl.BlockSpec((1, C, KV_DUP), lambda b: (b, 0, 0)),
                   pl.BlockSpec((1, C, KV_DUP), lambda b: (b, 0, 0))],
        out_shape=[jax.ShapeDtypeStruct((B, C, KV_DUP), BF16),
                   jax.ShapeDtypeStruct((B, C, KV_DUP), BF16)],
        compiler_params=_params(("arbitrary",)),
        name="ctxproj",
    )(ctx, mod3, g, w_kv)


def _attn_kernel(q_ref, kp_ref, kc_ref, kn_ref, vp_ref, vc_ref, vn_ref, kx_ref, vx_ref, sink_ref, o_ref):
    n = pl.program_id(1)
    nb = pl.num_programs(1)
    q = q_ref[0]
    rows = (N_HEADS // N_KV_HEADS) * BLOCK
    lo = lax.broadcasted_iota(jnp.int32, (BLOCK, LANES), 1) < HEAD_DIM
    r = lax.broadcasted_iota(jnp.int32, (rows, BLOCK), 0) % BLOCK
    c = lax.broadcasted_iota(jnp.int32, (rows, BLOCK), 1)
    ok_prev = jnp.logical_and(c >= r, n > 0)
    ok_next = jnp.logical_and(c <= r, n < nb - 1)
    scale = HEAD_DIM ** -0.5
    zero = jnp.zeros((BLOCK, LANES), BF16)
    for g in range(N_KV_HEADS):
        sl = slice(g * LANES, (g + 1) * LANES)
        qa = q[:, 2 * g * LANES:(2 * g + 1) * LANES]
        qb = q[:, (2 * g + 1) * LANES:(2 * g + 2) * LANES]
        lhs = jnp.concatenate([jnp.where(lo, qa, zero), jnp.where(lo, zero, qa),
                               jnp.where(lo, qb, zero), jnp.where(lo, zero, qb)], axis=0)
        s_p = jnp.where(ok_prev, _dot_nt(lhs, kp_ref[0, :, sl]) * scale, NEG)
        s_c = _dot_nt(lhs, kc_ref[0, :, sl]) * scale
        s_n = jnp.where(ok_next, _dot_nt(lhs, kn_ref[0, :, sl]) * scale, NEG)
        s_x = _dot_nt(lhs, kx_ref[0, :, sl]) * scale
        sink = sink_ref[g][:, 0:1]
        m = jnp.maximum(jnp.maximum(jnp.max(s_p, axis=-1, keepdims=True), jnp.max(s_c, axis=-1, keepdims=True)),
                        jnp.maximum(jnp.max(s_n, axis=-1, keepdims=True), jnp.max(s_x, axis=-1, keepdims=True)))
        m = jnp.maximum(m, sink)
        p_p = jnp.exp(s_p - m)
        p_c = jnp.exp(s_c - m)
        p_n = jnp.exp(s_n - m)
        p_x = jnp.exp(s_x - m)
        den = (jnp.sum(p_p, axis=-1, keepdims=True) + jnp.sum(p_c, axis=-1, keepdims=True)
               + jnp.sum(p_n, axis=-1, keepdims=True) + jnp.sum(p_x, axis=-1, keepdims=True)
               + jnp.exp(sink - m))
        o = (_dot(p_p.astype(BF16), vp_ref[0, :, sl]) + _dot(p_c.astype(BF16), vc_ref[0, :, sl])
             + _dot(p_n.astype(BF16), vn_ref[0, :, sl]) + _dot(p_x.astype(BF16), vx_ref[0, :, sl]))
        o = o / den
        o_ref[0, :, 2 * g * LANES:(2 * g + 1) * LANES] = jnp.where(
            lo, o[0:BLOCK], o[BLOCK:2 * BLOCK]).astype(BF16)
        o_ref[0, :, (2 * g + 1) * LANES:(2 * g + 2) * LANES] = jnp.where(
            lo, o[2 * BLOCK:3 * BLOCK], o[3 * BLOCK:4 * BLOCK]).astype(BF16)


def _attn_call(q, k, v, kx, vx, sink_b):
    B, L, _ = q.shape
    C = kx.shape[1]
    nb = L // BLOCK
    kv = lambda f: pl.BlockSpec((1, BLOCK, KV_DUP), f)
    prev = lambda b, n: (b, jnp.maximum(n - 1, 0), 0)
    cur = lambda b, n: (b, n, 0)
    nxt = lambda b, n: (b, jnp.minimum(n + 1, nb - 1), 0)
    rows = (N_HEADS // N_KV_HEADS) * BLOCK
    return pl.pallas_call(
        _attn_kernel,
        grid=(B, nb),
        in_specs=[pl.BlockSpec((1, BLOCK, Q_W), cur),
                  kv(prev), kv(cur), kv(nxt), kv(prev), kv(cur), kv(nxt),
                  pl.BlockSpec((1, C, KV_DUP), lambda b, n: (b, 0, 0)),
                  pl.BlockSpec((1, C, KV_DUP), lambda b, n: (b, 0, 0)),
                  pl.BlockSpec((N_KV_HEADS, rows, LANES), lambda b, n: (0, 0, 0))],
        out_specs=pl.BlockSpec((1, BLOCK, Q_W), cur),
        out_shape=jax.ShapeDtypeStruct((B, L, Q_W), BF16),
        compiler_params=_params(("parallel", "arbitrary")),
        name="attn",
    )(q, k, k, k, v, v, v, kx, vx, sink_b)


def _filt_kernel(z_ref, w1_ref, b1_ref, fr_ref, w2_ref, b2_ref, w3f_ref, w3b_ref, b3f_ref, b3b_ref,
                 dl_ref, hs_ref, hd_ref):
    L = z_ref.shape[0]
    z = z_ref[...]
    fr = fr_ref[...]
    h = jnp.sin(fr * (_dot3(z, w1_ref[...]) + b1_ref[...]))
    h = jnp.sin(fr * (_dot3(h, w2_ref[...]) + b2_ref[...]))
    decay = jnp.exp(-z[:, 0:1] * dl_ref[...])
    hf = (_dot3(h, w3f_ref[...]) + b3f_ref[...]) * decay
    hb = (_dot3(h, w3b_ref[...]) + b3b_ref[...]) * decay
    row = lax.broadcasted_iota(jnp.int32, hb.shape, 0)
    hb = jnp.where(row < L - 1, hb, 0.0)
    norm = jnp.sum(jnp.abs(hf), axis=0, keepdims=True) + jnp.sum(jnp.abs(hb), axis=0, keepdims=True)
    inv = 1.0 / norm
    hf = hf * inv
    hbs = jnp.where(row >= 1, pltpu.roll(hb, 1, 0), 0.0) * inv
    hs_ref[...] = (hf + hbs).astype(BF16)
    hd_ref[...] = (hf - hbs).astype(BF16)


def _filt_call(zf, w1p, b1, fr, w2, b2, w3, b3, absdelta):
    L = zf.shape[0]
    nct = HY_WIDTH // LANES
    H = HY_FILTER_HIDDEN
    full = lambda shape: pl.BlockSpec(shape, lambda j: (0, 0))
    return pl.pallas_call(
        _filt_kernel,
        grid=(nct,),
        in_specs=[full((L, LANES)), full((LANES, H)), full((1, H)), full((1, H)), full((H, H)), full((1, H)),
                  pl.BlockSpec((H, LANES), lambda j: (0, j)),
                  pl.BlockSpec((H, LANES), lambda j: (0, nct + j)),
                  pl.BlockSpec((1, LANES), lambda j: (0, j)),
                  pl.BlockSpec((1, LANES), lambda j: (0, nct + j)),
                  pl.BlockSpec((1, LANES), lambda j: (0, j))],
        out_specs=[pl.BlockSpec((L, LANES), lambda j: (0, j)),
                   pl.BlockSpec((L, LANES), lambda j: (0, j))],
        out_shape=[jax.ShapeDtypeStruct((L, HY_WIDTH), BF16),
                   jax.ShapeDtypeStruct((L, HY_WIDTH), BF16)],
        compiler_params=_params(("arbitrary",)),
        name="filt",
    )(zf, w1p, b1, fr, w2, b2, w3, w3, b3, b3, absdelta)


def _sconv_kernel(x0_ref, x1_ref, v_ref, w0_ref, w1_ref, w2_ref, b0_ref, b1_ref, b2_ref, u_ref, g_ref):
    L = x0_ref.shape[1]
    row = lax.broadcasted_iota(jnp.int32, (L, LANES), 0)

    def conv(z_ref, w_ref, b_ref):
        z = z_ref[0].astype(F32)
        w = w_ref[...]
        zp = jnp.where(row >= 1, pltpu.roll(z, 1, 0), 0.0)
        zn = jnp.where(row < L - 1, pltpu.roll(z, L - 1, 0), 0.0)
        return zp * w[0:1] + z * w[1:2] + zn * w[2:3] + b_ref[...]

    g_ref[0] = conv(x0_ref, w0_ref, b0_ref).astype(BF16)
    u_ref[0] = (conv(v_ref, w2_ref, b2_ref) * conv(x1_ref, w1_ref, b1_ref)).astype(BF16)


def _sconv_call(hy, conv_w, conv_b):
    B, L, _ = hy.shape
    nct = HY_WIDTH // LANES
    zs = lambda part: pl.BlockSpec((1, L, LANES), lambda b, j: (b, 0, part * nct + j))
    ws = lambda part: pl.BlockSpec((3, LANES), lambda b, j: (0, part * nct + j))
    bs = lambda part: pl.BlockSpec((1, LANES), lambda b, j: (0, part * nct + j))
    out = pl.BlockSpec((1, L, LANES), lambda b, j: (b, 0, j))
    return pl.pallas_call(
        _sconv_kernel,
        grid=(B, nct),
        in_specs=[zs(0), zs(1), zs(2), ws(0), ws(1), ws(2), bs(0), bs(1), bs(2)],
        out_specs=[out, out],
        out_shape=[jax.ShapeDtypeStruct((B, L, HY_WIDTH), BF16),
                   jax.ShapeDtypeStruct((B, L, HY_WIDTH), BF16)],
        compiler_params=_params(("parallel", "arbitrary")),
        name="sconv",
    )(hy, hy, hy, conv_w, conv_w, conv_w, conv_b, conv_b, conv_b)


def _lconv_kernel(u_ref, hs_ref, hd_ref, f_ref, skip_ref, y_ref, rhs_ref, acc_ref):
    nbat, L, _ = u_ref.shape
    j = pl.program_id(1)
    nj = pl.num_programs(1)
    tk = f_ref.shape[1]
    W = nbat * LANES

    @pl.when(j == 0)
    def _():
        for b in range(nbat):
            rhs_ref[:, b * LANES:(b + 1) * LANES] = u_ref[b]
        rhs_ref[:, W:W + LANES] = hs_ref[...]
        rhs_ref[:, W + LANES:W + 2 * LANES] = hd_ref[...]
        acc_ref[...] = jnp.zeros_like(acc_ref)

    fp = f_ref[0]
    fq = f_ref[1]
    ap = _dot(fp, rhs_ref[:, 0:W + LANES])
    aq = _dot(fq, rhs_ref[:, 0:W])
    kq = _dot(fq, rhs_ref[:, W + LANES:W + 2 * LANES])
    kq0 = _dot(fq[0:16], rhs_ref[:, W:W + LANES])[0:1]
    first = jnp.logical_and(lax.broadcasted_iota(jnp.int32, (tk, LANES), 0) == 0, j == 0)
    kp = ap[:, W:W + LANES]
    kq = jnp.where(first, kq0, kq)
    wgt = jnp.where(first, 0.5 / L, 1.0 / L)
    yps = []
    yqs = []
    for b in range(nbat):
        up = ap[:, b * LANES:(b + 1) * LANES]
        uq = aq[:, b * LANES:(b + 1) * LANES]
        yp = jnp.where(first, up * kp, up * kp - uq * kq)
        yq = jnp.where(first, uq * kq, up * kq + uq * kp)
        yps.append((yp * wgt).astype(BF16))
        yqs.append((yq * wgt).astype(BF16))
    yp = jnp.concatenate(yps, axis=1)
    yq = jnp.concatenate(yqs, axis=1)
    acc_ref[...] += _dot_tn(fp, yp) + _dot_tn(fq, yq)

    @pl.when(j == nj - 1)
    def _():
        for b in range(nbat):
            y = acc_ref[:, b * LANES:(b + 1) * LANES] + u_ref[b].astype(F32) * skip_ref[...]
            y_ref[b] = y.astype(BF16)


def _lconv_call(u, hs, hd, ftab, skip, tk):
    B, L, _ = u.shape
    nct = HY_WIDTH // LANES
    return pl.pallas_call(
        _lconv_kernel,
        grid=(nct, L // tk),
        in_specs=[pl.BlockSpec((B, L, LANES), lambda c, j: (0, 0, c)),
                  pl.BlockSpec((L, LANES), lambda c, j: (0, c)),
                  pl.BlockSpec((L, LANES), lambda c, j: (0, c)),
                  pl.BlockSpec((2, tk, L), lambda c, j: (0, j, 0)),
                  pl.BlockSpec((1, LANES), lambda c, j: (0, c))],
        out_specs=pl.BlockSpec((B, L, LANES), lambda c, j: (0, 0, c)),
        out_shape=jax.ShapeDtypeStruct((B, L, HY_WIDTH), BF16),
        scratch_shapes=[pltpu.VMEM((L, (B + 2) * LANES), BF16),
                        pltpu.VMEM((L, B * LANES), F32)],
        compiler_params=_params(("parallel", "arbitrary")),
        name="lconv",
    )(u, hs, hd, ftab, skip)


def _merge_kernel(x_ref, mod_ref, g1_ref, g2_ref, ya_ref, yc_ref, x0_ref, wg_ref, woa_ref, woh_ref, wout_ref,
                  wq_ref, x1_ref, h2t_ref, qp_ref):
    m = mod_ref[0]
    x = x_ref[0]
    h = _norm_mod(x, g1_ref[...], m[0:1], m[1:2]).astype(BF16)
    D = x.shape[-1]
    gate_a = jax.nn.sigmoid(_dot(h, wg_ref[:, :D]))
    gate_h = jax.nn.sigmoid(_dot(h, wg_ref[:, D:]))
    y_hy = yc_ref[0] * x0_ref[0]
    merged = gate_a * _dot(ya_ref[0], woa_ref[...]) + gate_h * _dot(y_hy, woh_ref[...])
    x1 = x + m[2:3] * _dot(merged.astype(BF16), wout_ref[...])
    x1_ref[0] = x1
    h2 = _norm_mod(x1, g2_ref[...], m[3:4], m[4:5])
    h2t_ref[...] = h2.T.astype(BF16)
    qp_ref[0] = _dot(h2.astype(BF16), wq_ref[...])


def _merge_call(x, mod3, g1, g2, ya, yc, x0c, wg, woa, woh, wout, wq, tm):
    B, L, D = x.shape
    tok = lambda w: pl.BlockSpec((1, tm, w), lambda b, i: (b, i, 0))
    full = lambda a: pl.BlockSpec(a.shape, lambda b, i: (0, 0))
    return pl.pallas_call(
        _merge_kernel,
        grid=(B, L // tm),
        in_specs=[tok(D), pl.BlockSpec((1, 6, D), lambda b, i: (b, 0, 0)), full(g1), full(g2),
                  tok(Q_W), tok(HY_WIDTH), tok(HY_WIDTH), full(wg), full(woa), full(woh), full(wout), full(wq)],
        out_specs=[tok(D), pl.BlockSpec((D, tm), lambda b, i: (0, b * (L // tm) + i)), tok(D)],
        out_shape=[jax.ShapeDtypeStruct((B, L, D), F32),
                   jax.ShapeDtypeStruct((D, B * L), BF16),
                   jax.ShapeDtypeStruct((B, L, D), F32)],
        compiler_params=_params(("parallel", "arbitrary")),
        name="merge",
    )(x, mod3, g1, g2, ya, yc, x0c, wg, woa, woh, wout, wq)


def _topk_rank(s):
    n, t = s.shape
    rowi = lax.broadcasted_iota(jnp.int32, (n, t), 0).astype(F32)
    topi = lax.broadcasted_iota(jnp.int32, (PEER_TOPK, t), 0)

    def body(it, carry):
        s, rank, vals = carry
        mx = jnp.max(s, axis=0, keepdims=True)
        first = jnp.min(jnp.where(s == mx, rowi, float(n)), axis=0, keepdims=True)
        sel = rowi == first
        return (jnp.where(sel, NEG, s), jnp.where(sel, jnp.asarray(it, F32), rank), jnp.where(topi == it, mx, vals))

    init = (s, jnp.full((n, t), float(PEER_TOPK), F32), jnp.zeros((PEER_TOPK, t), F32))
    _, rank, vals = lax.fori_loop(0, PEER_TOPK, body, init)
    return rank, vals


def _cand_blocks():
    blocks = []
    for c in range(PEER_TOPK // 2):
        valid = PEER_TOPK // (c + 1)
        blocks.append((c, -(-valid // 8) * 8, valid))
    return blocks


def _cand_index(t):
    K = PEER_TOPK
    fis = []
    for c, rows, valid in _cand_blocks():
        ri = lax.broadcasted_iota(jnp.int32, (rows, t), 0)
        fis.append(jnp.where(ri < valid, ri * K + c, K * K))
    ci = lax.broadcasted_iota(jnp.int32, (8, t), 0)
    fis.append(ci + K // 2)
    return jnp.concatenate(fis, axis=0).astype(F32)


def _route_head_iterative(s1, s2, fi):
    K = PEER_TOPK
    t = s1.shape[1]
    blocks = _cand_blocks()
    ncand = fi.shape[0]
    rank1, a = _topk_rank(s1)
    rank2, b = _topk_rank(s2)
    cands = []
    for c, rows, valid in blocks:
        cands.append(a[0:rows] + b[c:c + 1])
    cands.append(a[0:1] + b[K // 2:K])
    cand = jnp.where(fi < K * K, jnp.concatenate(cands, axis=0), NEG)

    def body(it, carry):
        cand, sel_all = carry
        mx = jnp.max(cand, axis=0, keepdims=True)
        first = jnp.min(jnp.where(cand == mx, fi, float(K * K)), axis=0, keepdims=True)
        sel = fi == first
        return jnp.where(sel, NEG, cand), jnp.where(sel, 1.0, sel_all)

    _, sel = lax.fori_loop(0, K, body, (cand, jnp.zeros((ncand, t), F32)))
    ea = jnp.exp(a - a[0:1])
    eb = jnp.exp(b - b[0:1])
    cnt = jnp.zeros((K, t), F32)
    zsum = jnp.zeros((1, t), F32)
    off = 0
    for c, rows, valid in blocks:
        blk = sel[off:off + rows]
        off += rows
        if rows < K:
            blk_full = jnp.concatenate([blk, jnp.zeros((K - rows, t), F32)], axis=0)
        else:
            blk_full = blk
        cnt = cnt + blk_full
        zsum = zsum + jnp.sum(blk * ea[0:rows], axis=0, keepdims=True) * eb[c:c + 1]
    tail = sel[off:off + 8]
    tcount = jnp.sum(tail, axis=0, keepdims=True)
    row0 = lax.broadcasted_iota(jnp.int32, (K, t), 0) == 0
    cnt = cnt + jnp.where(row0, tcount, 0.0)
    zsum = zsum + jnp.sum(tail * eb[K // 2:K], axis=0, keepdims=True)
    c1 = jnp.zeros((PEER_KEYS, t), F32)
    for r in range(K):
        c1 = jnp.where(rank1 == r, cnt[r:r + 1], c1)
    e1 = jnp.where(rank1 < K, jnp.exp(s1 - a[0:1]), 0.0) * (0.5 / zsum)
    e2 = jnp.where(rank2 < K, jnp.exp(s2 - b[0:1]), 0.0)
    return rank2, e2, c1, e1


def _sort_network(n):
    def merge(lo, hi, r):
        step = r * 2
        if step < hi - lo:
            yield from merge(lo, hi, step)
            yield from merge(lo + r, hi, step)
            for i in range(lo + r, hi - r, step):
                yield (i, i + r)
        else:
            yield (lo, lo + r)

    def sort(lo, hi):
        if hi - lo >= 1:
            mid = lo + (hi - lo) // 2
            yield from sort(lo, mid)
            yield from sort(mid + 1, hi)
            yield from merge(lo, hi, 1)

    return list(sort(0, n - 1))


def _exchange(xs, i, j):
    hi = jnp.maximum(xs[i], xs[j])
    xs[j] = jnp.minimum(xs[i], xs[j])
    xs[i] = hi


def _bitonic_finish(xs):
    n = len(xs)
    d = n // 2
    while d >= 1:
        for i in range(n):
            if i & d == 0:
                _exchange(xs, i, i + d)
        d //= 2


def _merge_sublanes(xs):
    n = len(xs)
    for shift in (4, 2, 1):
        other = [pltpu.roll(x, shift, 0) for x in xs]
        xs = [jnp.maximum(xs[i], other[n - 1 - i]) for i in range(n)]
        _bitonic_finish(xs)
    return xs


def _top_sorted(s):
    xs = [s[8 * v:8 * v + 8] for v in range(s.shape[0] // 8)]
    for i, j in _sort_network(len(xs)):
        _exchange(xs, i, j)
    return _merge_sublanes(xs)


def _route_head_sorted(s1, s2):
    K = PEER_TOPK
    t = s1.shape[1]
    a = _top_sorted(s1)
    b = _top_sorted(s2)
    sub = lax.broadcasted_iota(jnp.int32, (8, t), 0)
    a8 = a[7]
    for r in range(6, -1, -1):
        a8 = jnp.where(sub == r, a[r], a8)
    main = [jnp.where((sub + 1) * (c + 1) <= K, a8 + b[c], NEG) for c in range(K)]
    top = _merge_sublanes(list(main))
    single = [a[8 + i] + b[0] for i in range(K // 2)] + [jnp.full((8, t), NEG, F32)] * (K // 2)
    top = [jnp.maximum(top[i], single[K - 1 - i]) for i in range(K)]
    _bitonic_finish(top)
    tau = top[K - 1]
    ea8 = jnp.exp(a8 - a[0])
    cnt8 = jnp.zeros((8, t), F32)
    z8 = jnp.zeros((8, t), F32)
    for c in range(K):
        hit = main[c] >= tau
        cnt8 = cnt8 + jnp.where(hit, 1.0, 0.0)
        z8 = z8 + jnp.where(hit, ea8 * jnp.exp(b[c] - b[0]), 0.0)
    cnt = [jnp.broadcast_to(cnt8[r:r + 1], (8, t)) for r in range(8)]
    zsum = jnp.sum(z8, axis=0, keepdims=True)
    total = jnp.sum(cnt8, axis=0, keepdims=True)
    for i in range(K // 2):
        hit = jnp.where(single[i] >= tau, 1.0, 0.0)
        cnt.append(hit)
        zsum = zsum + hit[0:1] * jnp.exp(a[8 + i][0:1] - a[0][0:1])
        total = total + hit[0:1]
    in1 = jnp.zeros((1, t), F32)
    in2 = jnp.zeros((1, t), F32)
    c1s, e1s, r2s, e2s = [], [], [], []
    scale = 0.5 / zsum
    for v in range(PEER_KEYS // 8):
        x1 = s1[8 * v:8 * v + 8]
        x2 = s2[8 * v:8 * v + 8]
        c1 = jnp.zeros((8, t), F32)
        r2 = jnp.full((8, t), float(K), F32)
        for r in range(K):
            c1 = jnp.where(x1 == a[r], cnt[r], c1)
            r2 = jnp.where(x2 == b[r], float(r), r2)
        top1 = x1 >= a[K - 1]
        top2 = x2 >= b[K - 1]
        in1 = in1 + jnp.sum(jnp.where(top1, 1.0, 0.0), axis=0, keepdims=True)
        in2 = in2 + jnp.sum(jnp.where(top2, 1.0, 0.0), axis=0, keepdims=True)
        c1s.append(c1)
        r2s.append(r2)
        e1s.append(jnp.where(top1, jnp.exp(x1 - a[0]), 0.0) * scale)
        e2s.append(jnp.where(top2, jnp.exp(x2 - b[0]), 0.0))
    bad = (in1 != float(K)) | (in2 != float(K)) | (total != float(K))
    for r in range(K - 1):
        bad = bad | (a[r][0:1] == a[r + 1][0:1]) | (b[r][0:1] == b[r + 1][0:1])
    cat = lambda xs: jnp.concatenate(xs, axis=0)
    return cat(r2s), cat(e2s), cat(c1s), cat(e1s), jnp.where(bad, 1.0, 0.0)


def _route_kernel(qp_ref, kb_ref, r2_ref, e2_ref, c1_ref, e1_ref):
    t = qp_ref.shape[0]
    for hh in range(PEER_HEADS):
        q = qp_ref[:, hh * PEER_DKEY:(hh + 1) * PEER_DKEY]
        st = _dot3(kb_ref[hh], q, _dot_nt)
        s1 = st[:PEER_KEYS]
        s2 = st[PEER_KEYS:]
        r2, e2, c1, e1, bad = _route_head_sorted(s1, s2)
        r2_ref[hh] = r2.astype(BF16)
        e2_ref[hh] = e2.astype(BF16)
        c1_ref[hh] = c1
        e1_ref[hh] = e1

        @pl.when(jnp.max(bad) > 0.0)
        def _():
            r2, e2, c1, e1 = _route_head_iterative(s1, s2, _cand_index(t))
            r2_ref[hh] = r2.astype(BF16)
            e2_ref[hh] = e2.astype(BF16)
            c1_ref[hh] = c1
            e1_ref[hh] = e1


def _route_call(qp, kb, tn):
    T = qp.shape[0]
    out = pl.BlockSpec((PEER_HEADS, PEER_KEYS, tn), lambda i: (0, 0, i))
    shp = jax.ShapeDtypeStruct((PEER_HEADS, PEER_KEYS, T), F32)
    shp16 = jax.ShapeDtypeStruct((PEER_HEADS, PEER_KEYS, T), BF16)
    return pl.pallas_call(
        _route_kernel,
        grid=(T // tn,),
        in_specs=[pl.BlockSpec((tn, PEER_HEADS * PEER_DKEY), lambda i: (i, 0)),
                  pl.BlockSpec(kb.shape, lambda i: (0, 0, 0))],
        out_specs=[out, out, out, out],
        out_shape=[shp16, shp16, shp, shp],
        compiler_params=_params(("parallel",)),
        name="route",
    )(qp, kb)


PEER_SUB = 256
ROWS16 = 16


def _gelu_half(x):
    k = math.sqrt(2.0 / math.pi)
    return x + x * jnp.tanh(x * (k + (k * 0.044715) * (x * x)))


def _peer_kernel(h2_ref, u_ref, vt_ref, r2_ref, e2_ref, c1_ref, e1_ref, x1_ref, g2_ref, fg_ref, o_ref, acc_ref,
                 w_ref, at_ref):
    e = pl.program_id(1)
    ne = pl.num_programs(1)
    te = u_ref.shape[0]

    @pl.when(e == 0)
    def _():
        acc_ref[...] = jnp.zeros_like(acc_ref)

    zero = jnp.zeros((), BF16)
    nsub = h2_ref.shape[1] // PEER_SUB
    toks = [slice(th * PEER_SUB, (th + 1) * PEER_SUB) for th in range(nsub)]

    def gates(th):
        tok = toks[th]
        for ii in range(te // PEER_KEYS):
            gs = [None] * (PEER_KEYS // ROWS16)
            for hh in range(PEER_HEADS):
                c1 = jnp.broadcast_to(c1_ref[hh, ii:ii + 1, tok], (ROWS16, PEER_SUB)).astype(BF16)
                e1 = jnp.broadcast_to(e1_ref[hh, ii:ii + 1, tok], (ROWS16, PEER_SUB)).astype(BF16)
                for k in range(PEER_KEYS // ROWS16):
                    rows = slice(k * ROWS16, (k + 1) * ROWS16)
                    term = jnp.where(r2_ref[hh, rows, tok] < c1, e2_ref[hh, rows, tok], zero) * e1
                    gs[k] = term if gs[k] is None else gs[k] + term
            for k in range(PEER_KEYS // ROWS16):
                lo = ii * PEER_KEYS + k * ROWS16
                w_ref[th, lo:lo + ROWS16, :] = gs[k]

    def project(th):
        at_ref[th] = _dot(u_ref[...], h2_ref[:, toks[th]])

    def activate(th):
        w_ref[th] = w_ref[th] * _gelu_half(at_ref[th]).astype(BF16)

    def combine(th):
        acc_ref[:, toks[th]] += _dot(vt_ref[...], w_ref[th])

    gates(0)
    project(0)
    for th in range(nsub):
        if th + 1 < nsub:
            gates(th + 1)
        activate(th)
        if th + 1 < nsub:
            project(th + 1)
        combine(th)

    @pl.when(e == ne - 1)
    def _():
        x2 = x1_ref[...] + g2_ref[0] * acc_ref[...].T
        y = x2 * lax.rsqrt(jnp.mean(x2 * x2, axis=-1, keepdims=True) + NORM_EPS) * fg_ref[...]
        o_ref[...] = y


def _peer_call(h2, u_tab, vt_tab, r2, e2, c1, e1, x1, g2rows, fg, tn, te, toks_per_batch):
    D, T = h2.shape
    E = u_tab.shape[0]
    rows = te // PEER_KEYS
    tab = pl.BlockSpec((PEER_HEADS, PEER_KEYS, tn), lambda i, e: (0, 0, i))
    sel = pl.BlockSpec((PEER_HEADS, rows, tn), lambda i, e: (0, e, i))
    return pl.pallas_call(
        _peer_kernel,
        grid=(T // tn, E // te),
        in_specs=[pl.BlockSpec((D, tn), lambda i, e: (0, i)),
                  pl.BlockSpec((te, D), lambda i, e: (e, 0)),
                  pl.BlockSpec((D, te), lambda i, e: (0, e)),
                  tab, tab, sel, sel,
                  pl.BlockSpec((tn, D), lambda i, e: (i, 0)),
                  pl.BlockSpec((1, 1, D), lambda i, e: ((i * tn) // toks_per_batch, 0, 0)),
                  pl.BlockSpec((1, D), lambda i, e: (0, 0))],
        out_specs=pl.BlockSpec((tn, D), lambda i, e: (i, 0)),
        out_shape=jax.ShapeDtypeStruct((T, D), F32),
        scratch_shapes=[pltpu.VMEM((D, tn), F32), pltpu.VMEM((tn // PEER_SUB, te, PEER_SUB), BF16),
                        pltpu.VMEM((tn // PEER_SUB, te, PEER_SUB), F32)],
        compiler_params=_params(("parallel", "arbitrary")),
        name="peer",
    )(h2, u_tab, vt_tab, r2, e2, c1, e1, x1, g2rows, fg)


def _rope_tables(L):
    rows = L // GRID_W
    row = jnp.repeat(jnp.arange(rows, dtype=jnp.int32), GRID_W).astype(F32)
    col = jnp.tile(jnp.arange(GRID_W, dtype=jnp.int32), rows).astype(F32)
    f = HEAD_DIM // 4
    inv = ROPE_BASE ** (-jnp.arange(f, dtype=F32) / f)
    ang_r = row[:, None] * inv[None, :]
    ang_c = col[:, None] * inv[None, :]
    ang = jnp.concatenate([ang_r, ang_r, ang_c, ang_c], axis=-1)
    reps = ROT_W // HEAD_DIM
    return jnp.tile(jnp.cos(ang), (1, reps)), jnp.tile(jnp.sin(ang), (1, reps))


def _rot_cols(w):
    n = w.shape[1]
    f = HEAD_DIM // 4
    d = jnp.arange(n) % (2 * f)
    src = jnp.where(d < f, jnp.arange(n) + f, jnp.arange(n) - f)
    sign = jnp.where(d < f, -1.0, 1.0).astype(w.dtype)
    return w[:, src] * sign[None, :]


def _dup_cols(w):
    return jnp.concatenate([w[:, :HEAD_DIM], w[:, :HEAD_DIM], w[:, HEAD_DIM:], w[:, HEAD_DIM:]], axis=1)


def _dft_table(L):
    N = 2 * L
    s = 64
    kh = jnp.arange(L // s, dtype=jnp.int32)
    kl = jnp.arange(s, dtype=jnp.int32)
    n = jnp.arange(L, dtype=jnp.int32)
    ph1 = ((kh[:, None] * s * n[None, :]) % N).astype(F32) * (2.0 * math.pi / N)
    ph2 = ((kl[:, None] * n[None, :]) % N).astype(F32) * (2.0 * math.pi / N)
    c1, s1 = jnp.cos(ph1)[:, None, :], jnp.sin(ph1)[:, None, :]
    c2, s2 = jnp.cos(ph2)[None, :, :], jnp.sin(ph2)[None, :, :]
    cosm = (c1 * c2 - s1 * s2).reshape(L, L)
    sinm = (s1 * c2 + c1 * s2).reshape(L, L)
    nyq = jnp.where(n % 2 == 0, 1.0, -1.0).astype(F32)
    k = jnp.arange(L, dtype=jnp.int32)
    q = jnp.where(k[:, None] == 0, nyq[None, :], -sinm)
    return jnp.stack([cosm, q]).astype(BF16)


def _filter_features(L):
    t = jnp.arange(L, dtype=F32) / L
    bands = jnp.arange(1, HY_EMB_BANDS + 1, dtype=F32)
    ang = 2.0 * math.pi * t[:, None] * bands[None, :]
    z = jnp.concatenate([t[:, None], jnp.cos(ang), jnp.sin(ang)], axis=-1)
    return jnp.pad(z, ((0, 0), (0, LANES - z.shape[1])))


def _tile(n, pref):
    return pref if n % pref == 0 else n


def kernel(x, c, ctx, c_ctx, w_mod, b_mod, norm1_g, w_in, attn_sink, hy_conv_w, hy_conv_b, hy_fw1, hy_fb1, hy_fw2, hy_fb2, hy_fw3, hy_fb3, hy_freq, hy_skip, w_o_attn, w_o_hy, w_out, norm2_g, peer_wq, peer_keys, peer_u, peer_v, final_g):
    B, L, D = x.shape
    assert B == 4 and D == D_MODEL and w_mod.shape[0] == 1
    T = B * L
    li = 0

    c8 = jnp.concatenate([c, c_ctx[None, :], jnp.zeros((3, D), F32)], axis=0)
    mod3 = _mod_call(c8, w_mod[li], b_mod[li]).reshape(8, 6, D)
    g1 = norm1_g[li].reshape(1, D)
    g2 = norm2_g[li].reshape(1, D)

    w = w_in[li]
    wq, wk, wv = w[:, :OFF_K], _dup_cols(w[:, OFF_K:OFF_V]), _dup_cols(w[:, OFF_V:OFF_HY])
    w_cat = jnp.concatenate([wq, wk, wv, w[:, OFF_HY:OFF_G], _rot_cols(wq), _rot_cols(wk)], axis=1).astype(BF16)
    cos_t, sin_t = _rope_tables(L)
    q, k, v, hy = _inproj_call(x, mod3, g1, w_cat, cos_t, sin_t, _tile(L, 512))
    kx, vx = _ctxproj_call(ctx, mod3, g1, jnp.concatenate([wk, wv], axis=1).astype(BF16))

    gsz = N_HEADS // N_KV_HEADS
    sink_b = jnp.broadcast_to(
        jnp.repeat(attn_sink[li].astype(F32).reshape(N_KV_HEADS, gsz), BLOCK, axis=1)[:, :, None],
        (N_KV_HEADS, gsz * BLOCK, LANES))
    y_attn = _attn_call(q, k, v, kx, vx, sink_b)

    H = HY_FILTER_HIDDEN
    w1p = jnp.pad(hy_fw1[li], ((0, LANES - hy_fw1.shape[1]), (0, 0)))
    deltas = jnp.abs(jnp.linspace(math.log(HY_DECAY_TARGET) / HY_SLOW_DECAY,
                                  math.log(HY_DECAY_TARGET) / HY_FAST_DECAY, HY_WIDTH, dtype=F32)).reshape(1, -1)
    hs, hd = _filt_call(_filter_features(L), w1p, hy_fb1[li].reshape(1, H), hy_freq[li].reshape(1, H),
                        hy_fw2[li], hy_fb2[li].reshape(1, H), hy_fw3[li], hy_fb3[li].reshape(1, -1), deltas)
    u, x0c = _sconv_call(hy, hy_conv_w[li], hy_conv_b[li].reshape(1, -1))
    yc = _lconv_call(u, hs, hd, _dft_table(L), hy_skip[li].reshape(1, -1), _tile(L, 256))

    x1, h2t, qp = _merge_call(x, mod3, g1, g2, y_attn, yc, x0c, w[:, OFF_G:].astype(BF16),
                             w_o_attn[li].astype(BF16), w_o_hy[li].astype(BF16), w_out[li].astype(BF16),
                             peer_wq[li].astype(BF16), _tile(L, 256))

    keys = peer_keys[li]
    zk = jnp.zeros_like(keys[:, 0])
    kb = jnp.concatenate([jnp.concatenate([keys[:, 0], zk], axis=2),
                          jnp.concatenate([zk, keys[:, 1]], axis=2)], axis=1)
    r2, e2, c1, e1 = _route_call(qp.reshape(T, D), kb, _tile(T, 256))

    out = _peer_call(h2t, peer_u[li].astype(BF16), peer_v[li].T.astype(BF16), r2, e2, c1, e1,
                     x1.reshape(T, D), mod3[:B, 5:6, :], final_g.reshape(1, D),
                     _tile(T, 1024), 1024, L)
    return out.reshape(B, L, D)
```

```python
import functools
import math

import jax
import jax.numpy as jnp
from jax import lax
from jax.experimental import pallas as pl
from jax.experimental.pallas import tpu as pltpu

F32 = jnp.float32
BF16 = jnp.bfloat16

D_MODEL = 1024
GRID_W = 64
NORM_EPS = 1e-6
N_HEADS = 8
N_KV_HEADS = 2
HEAD_DIM = 64
BLOCK = 128
ROPE_BASE = 10000.0
HY_WIDTH = 512
HY_EMB_BANDS = 16
HY_FILTER_HIDDEN = 64
HY_FAST_DECAY = 0.3
HY_SLOW_DECAY = 1.5
HY_DECAY_TARGET = 1e-2
PEER_HEADS = 8
PEER_KEYS = 128
PEER_TOPK = 16
PEER_DKEY = 128
Q_W = N_HEADS * HEAD_DIM
KV_W = N_KV_HEADS * HEAD_DIM
HY_IN = 3 * HY_WIDTH
OFF_K = Q_W
OFF_V = OFF_K + KV_W
OFF_HY = OFF_V + KV_W
OFF_G = OFF_HY + HY_IN

LANES = 128
VMEM_LIMIT = 56 * 1024 * 1024
NEG = -1e30
KV_DUP = 2 * KV_W


def _params(sem):
    return pltpu.CompilerParams(dimension_semantics=sem, vmem_limit_bytes=VMEM_LIMIT)


def _dot(a, b):
    return lax.dot_general(a, b, (((1,), (0,)), ((), ())), preferred_element_type=F32)


def _dot_nt(a, b):
    return lax.dot_general(a, b, (((1,), (1,)), ((), ())), preferred_element_type=F32)


def _dot_tn(a, b):
    return lax.dot_general(a, b, (((0,), (0,)), ((), ())), preferred_element_type=F32)


def _split(a):
    hi = a.astype(BF16)
    lo = (a - hi.astype(F32)).astype(BF16)
    return hi, lo


def _dot3(a, b, dot=_dot):
    ah, al = _split(a)
    bh, bl = _split(b)
    return dot(ah, bh) + dot(ah, bl) + dot(al, bh)


def _norm_mod(x, g, shift, scale):
    y = x * lax.rsqrt(jnp.mean(x * x, axis=-1, keepdims=True) + NORM_EPS) * g
    return y * (1.0 + scale) + shift


def _mod_kernel(c_ref, w_ref, b_ref, o_ref):
    c = c_ref[...]
    s = c * jax.nn.sigmoid(c)
    o_ref[...] = _dot3(s, w_ref[...]) + b_ref[...]


def _mod_call(c8, w_mod, b_mod):
    n = w_mod.shape[1] // D_MODEL
    return pl.pallas_call(
        _mod_kernel,
        grid=(n,),
        in_specs=[pl.BlockSpec((8, D_MODEL), lambda j: (0, 0)),
                  pl.BlockSpec((D_MODEL, D_MODEL), lambda j: (0, j)),
                  pl.BlockSpec((1, D_MODEL), lambda j: (0, j))],
        out_specs=pl.BlockSpec((8, D_MODEL), lambda j: (0, j)),
        out_shape=jax.ShapeDtypeStruct((8, w_mod.shape[1]), F32),
        compiler_params=_params(("arbitrary",)),
        name="mod",
    )(c8, w_mod, b_mod.reshape(1, -1))


ROT_W = Q_W + KV_DUP
CAT_W = Q_W + 2 * KV_DUP + HY_IN + ROT_W


def _inproj_kernel(x_ref, mod_ref, g_ref, w_ref, cos_ref, sin_ref, q_ref, k_ref, v_ref, hy_ref):
    m = mod_ref[0]
    h = _norm_mod(x_ref[0], g_ref[...], m[0:1], m[1:2]).astype(BF16)
    o_v = ROT_W
    o_hy = o_v + KV_DUP
    o_rot = o_hy + HY_IN
    p = _dot(h, w_ref[:, 0:ROT_W])
    pr = _dot(h, w_ref[:, o_rot:o_rot + ROT_W])
    qk = p * cos_ref[...] + pr * sin_ref[...]
    q_ref[0] = qk[:, :Q_W].astype(BF16)
    k_ref[0] = qk[:, Q_W:].astype(BF16)
    v_ref[0] = _dot(h, w_ref[:, o_v:o_hy]).astype(BF16)
    for j in range(HY_IN // HY_WIDTH):
        lo = o_hy + j * HY_WIDTH
        hy_ref[0, :, j * HY_WIDTH:(j + 1) * HY_WIDTH] = _dot(h, w_ref[:, lo:lo + HY_WIDTH]).astype(BF16)


def _inproj_call(x, mod3, g, w_cat, cos_t, sin_t, tm):
    B, L, D = x.shape
    return pl.pallas_call(
        _inproj_kernel,
        grid=(B, L // tm),
        in_specs=[pl.BlockSpec((1, tm, D), lambda b, i: (b, i, 0)),
                  pl.BlockSpec((1, 6, D), lambda b, i: (b, 0, 0)),
                  pl.BlockSpec((1, D), lambda b, i: (0, 0)),
                  pl.BlockSpec((D, CAT_W), lambda b, i: (0, 0)),
                  pl.BlockSpec((tm, ROT_W), lambda b, i: (i, 0)),
                  pl.BlockSpec((tm, ROT_W), lambda b, i: (i, 0))],
        out_specs=[pl.BlockSpec((1, tm, Q_W), lambda b, i: (b, i, 0)),
                   pl.BlockSpec((1, tm, KV_DUP), lambda b, i: (b, i, 0)),
                   pl.BlockSpec((1, tm, KV_DUP), lambda b, i: (b, i, 0)),
                   pl.BlockSpec((1, tm, HY_IN), lambda b, i: (b, i, 0))],
        out_shape=[jax.ShapeDtypeStruct((B, L, Q_W), BF16),
                   jax.ShapeDtypeStruct((B, L, KV_DUP), BF16),
                   jax.ShapeDtypeStruct((B, L, KV_DUP), BF16),
                   jax.ShapeDtypeStruct((B, L, HY_IN), BF16)],
        compiler_params=_params(("parallel", "arbitrary")),
        name="inproj",
    )(x, mod3, g, w_cat, cos_t, sin_t)


def _ctxproj_kernel(x_ref, mod_ref, g_ref, w_ref, k_ref, v_ref):
    m = mod_ref[0]
    h = _norm_mod(x_ref[0], g_ref[...], m[0:1], m[1:2]).astype(BF16)
    k_ref[0] = _dot(h, w_ref[:, :KV_DUP]).astype(BF16)
    v_ref[0] = _dot(h, w_ref[:, KV_DUP:]).astype(BF16)


def _ctxproj_call(ctx, mod3, g, w_kv):
    B, C, D = ctx.shape
    return pl.pallas_call(
        _ctxproj_kernel,
        grid=(B,),
        in_specs=[pl.BlockSpec((1, C, D), lambda b: (b, 0, 0)),
                  pl.BlockSpec((1, 6, D), lambda b: (4, 0, 0)),
                  pl.BlockSpec((1, D), lambda b: (0, 0)),
                  pl.BlockSpec((D, 2 * KV_DUP), lambda b: (0, 0))],
        out_specs=[pl.BlockSpec((1, C, KV_DUP), lambda b: (b, 0, 0)),
                   pl.BlockSpec((1, C, KV_DUP), lambda b: (b, 0, 0))],
        out_shape=[jax.ShapeDtypeStruct((B, C, KV_DUP), BF16),
                   jax.ShapeDtypeStruct((B, C, KV_DUP), BF16)],
        compiler_params=_params(("arbitrary",)),
        name="ctxproj",
    )(ctx, mod3, g, w_kv)


def _attn_kernel(q_ref, kp_ref, kc_ref, kn_ref, vp_ref, vc_ref, vn_ref, kx_ref, vx_ref, sink_ref, o_ref):
    n = pl.program_id(1)
    nb = pl.num_programs(1)
    q = q_ref[0]
    rows = (N_HEADS // N_KV_HEADS) * BLOCK
    lo = lax.broadcasted_iota(jnp.int32, (BLOCK, LANES), 1) < HEAD_DIM
    r = lax.broadcasted_iota(jnp.int32, (rows, BLOCK), 0) % BLOCK
    c = lax.broadcasted_iota(jnp.int32, (rows, BLOCK), 1)
    ok_prev = jnp.logical_and(c >= r, n > 0)
    ok_next = jnp.logical_and(c <= r, n < nb - 1)
    scale = HEAD_DIM ** -0.5
    zero = jnp.zeros((BLOCK, LANES), BF16)
    for g in range(N_KV_HEADS):
        sl = slice(g * LANES, (g + 1) * LANES)
        qa = q[:, 2 * g * LANES:(2 * g + 1) * LANES]
        qb = q[:, (2 * g + 1) * LANES:(2 * g + 2) * LANES]
        lhs = jnp.concatenate([jnp.where(lo, qa, zero), jnp.where(lo, zero, qa),
                               jnp.where(lo, qb, zero), jnp.where(lo, zero, qb)], axis=0)
        s_p = jnp.where(ok_prev, _dot_nt(lhs, kp_ref[0, :, sl]) * scale, NEG)
        s_c = _dot_nt(lhs, kc_ref[0, :, sl]) * scale
        s_n = jnp.where(ok_next, _dot_nt(lhs, kn_ref[0, :, sl]) * scale, NEG)
        s_x = _dot_nt(lhs, kx_ref[0, :, sl]) * scale
        sink = sink_ref[g][:, 0:1]
        half = s_x.shape[1] // 2
        m = jnp.max(jnp.maximum(jnp.maximum(s_p, s_c), jnp.maximum(s_n, jnp.maximum(s_x[:, :half], s_x[:, half:]))),
                    axis=-1, keepdims=True)
        m = jnp.maximum(m, sink)
        p_p = jnp.exp(s_p - m)
        p_c = jnp.exp(s_c - m)
        p_n = jnp.exp(s_n - m)
        p_x = jnp.exp(s_x - m)
        den = (jnp.sum((p_p + p_c) + (p_n + (p_x[:, :half] + p_x[:, half:])), axis=-1, keepdims=True)
               + jnp.exp(sink - m))
        o = (_dot(p_p.astype(BF16), vp_ref[0, :, sl]) + _dot(p_c.astype(BF16), vc_ref[0, :, sl])
             + _dot(p_n.astype(BF16), vn_ref[0, :, sl]) + _dot(p_x.astype(BF16), vx_ref[0, :, sl]))
        o = o / den
        o_ref[0, :, 2 * g * LANES:(2 * g + 1) * LANES] = jnp.where(
            lo, o[0:BLOCK], o[BLOCK:2 * BLOCK]).astype(BF16)
        o_ref[0, :, (2 * g + 1) * LANES:(2 * g + 2) * LANES] = jnp.where(
            lo, o[2 * BLOCK:3 * BLOCK], o[3 * BLOCK:4 * BLOCK]).astype(BF16)


def _attn_call(q, k, v, kx, vx, sink_b):
    B, L, _ = q.shape
    C = kx.shape[1]
    nb = L // BLOCK
    kv = lambda f: pl.BlockSpec((1, BLOCK, KV_DUP), f)
    prev = lambda b, n: (b, jnp.maximum(n - 1, 0), 0)
    cur = lambda b, n: (b, n, 0)
    nxt = lambda b, n: (b, jnp.minimum(n + 1, nb - 1), 0)
    rows = (N_HEADS // N_KV_HEADS) * BLOCK
    return pl.pallas_call(
        _attn_kernel,
        grid=(B, nb),
        in_specs=[pl.BlockSpec((1, BLOCK, Q_W), cur),
                  kv(prev), kv(cur), kv(nxt), kv(prev), kv(cur), kv(nxt),
                  pl.BlockSpec((1, C, KV_DUP), lambda b, n: (b, 0, 0)),
                  pl.BlockSpec((1, C, KV_DUP), lambda b, n: (b, 0, 0)),
                  pl.BlockSpec((N_KV_HEADS, rows, LANES), lambda b, n: (0, 0, 0))],
        out_specs=pl.BlockSpec((1, BLOCK, Q_W), cur),
        out_shape=jax.ShapeDtypeStruct((B, L, Q_W), BF16),
        compiler_params=_params(("parallel", "arbitrary")),
        name="attn",
    )(q, k, k, k, v, v, v, kx, vx, sink_b)


def _filt_kernel(z_ref, w1_ref, b1_ref, fr_ref, w2_ref, b2_ref, w3f_ref, w3b_ref, b3f_ref, b3b_ref,
                 dl_ref, hs_ref, hd_ref):
    L = z_ref.shape[0]
    z = z_ref[...]
    fr = fr_ref[...]
    h = jnp.sin(fr * (_dot3(z, w1_ref[...]) + b1_ref[...]))
    h = jnp.sin(fr * (_dot3(h, w2_ref[...]) + b2_ref[...]))
    decay = jnp.exp(-z[:, 0:1] * dl_ref[...])
    hf = (_dot3(h, w3f_ref[...]) + b3f_ref[...]) * decay
    hb = (_dot3(h, w3b_ref[...]) + b3b_ref[...]) * decay
    row = lax.broadcasted_iota(jnp.int32, hb.shape, 0)
    hb = jnp.where(row < L - 1, hb, 0.0)
    norm = jnp.sum(jnp.abs(hf), axis=0, keepdims=True) + jnp.sum(jnp.abs(hb), axis=0, keepdims=True)
    inv = 1.0 / norm
    hf = hf * inv
    hbs = jnp.where(row >= 1, pltpu.roll(hb, 1, 0), 0.0) * inv
    hs_ref[...] = (hf + hbs).astype(BF16)
    hd_ref[...] = (hf - hbs).astype(BF16)


def _filt_call(zf, w1p, b1, fr, w2, b2, w3, b3, absdelta):
    L = zf.shape[0]
    nct = HY_WIDTH // LANES
    H = HY_FILTER_HIDDEN
    full = lambda shape: pl.BlockSpec(shape, lambda j: (0, 0))
    return pl.pallas_call(
        _filt_kernel,
        grid=(nct,),
        in_specs=[full((L, LANES)), full((LANES, H)), full((1, H)), full((1, H)), full((H, H)), full((1, H)),
                  pl.BlockSpec((H, LANES), lambda j: (0, j)),
                  pl.BlockSpec((H, LANES), lambda j: (0, nct + j)),
                  pl.BlockSpec((1, LANES), lambda j: (0, j)),
                  pl.BlockSpec((1, LANES), lambda j: (0, nct + j)),
                  pl.BlockSpec((1, LANES), lambda j: (0, j))],
        out_specs=[pl.BlockSpec((L, LANES), lambda j: (0, j)),
                   pl.BlockSpec((L, LANES), lambda j: (0, j))],
        out_shape=[jax.ShapeDtypeStruct((L, HY_WIDTH), BF16),
                   jax.ShapeDtypeStruct((L, HY_WIDTH), BF16)],
        compiler_params=_params(("arbitrary",)),
        name="filt",
    )(zf, w1p, b1, fr, w2, b2, w3, w3, b3, b3, absdelta)


def _sconv_kernel(x0_ref, x1_ref, v_ref, w0_ref, w1_ref, w2_ref, b0_ref, b1_ref, b2_ref, u_ref, g_ref):
    L = x0_ref.shape[1]
    row = lax.broadcasted_iota(jnp.int32, (L, LANES), 0)

    def conv(z_ref, w_ref, b_ref):
        z = z_ref[0].astype(F32)
        w = w_ref[...]
        zp = jnp.where(row >= 1, pltpu.roll(z, 1, 0), 0.0)
        zn = jnp.where(row < L - 1, pltpu.roll(z, L - 1, 0), 0.0)
        return zp * w[0:1] + z * w[1:2] + zn * w[2:3] + b_ref[...]

    g_ref[0] = conv(x0_ref, w0_ref, b0_ref).astype(BF16)
    u_ref[0] = (conv(v_ref, w2_ref, b2_ref) * conv(x1_ref, w1_ref, b1_ref)).astype(BF16)


def _sconv_call(hy, conv_w, conv_b):
    B, L, _ = hy.shape
    nct = HY_WIDTH // LANES
    zs = lambda part: pl.BlockSpec((1, L, LANES), lambda b, j: (b, 0, part * nct + j))
    ws = lambda part: pl.BlockSpec((3, LANES), lambda b, j: (0, part * nct + j))
    bs = lambda part: pl.BlockSpec((1, LANES), lambda b, j: (0, part * nct + j))
    out = pl.BlockSpec((1, L, LANES), lambda b, j: (b, 0, j))
    return pl.pallas_call(
        _sconv_kernel,
        grid=(B, nct),
        in_specs=[zs(0), zs(1), zs(2), ws(0), ws(1), ws(2), bs(0), bs(1), bs(2)],
        out_specs=[out, out],
        out_shape=[jax.ShapeDtypeStruct((B, L, HY_WIDTH), BF16),
                   jax.ShapeDtypeStruct((B, L, HY_WIDTH), BF16)],
        compiler_params=_params(("parallel", "arbitrary")),
        name="sconv",
    )(hy, hy, hy, conv_w, conv_w, conv_w, conv_b, conv_b, conv_b)


def _lconv_kernel(u_ref, hs_ref, hd_ref, f_ref, skip_ref, y_ref, rhs_ref, acc_ref):
    nbat, L, _ = u_ref.shape
    j = pl.program_id(1)
    nj = pl.num_programs(1)
    tk = f_ref.shape[1]
    W = nbat * LANES

    @pl.when(j == 0)
    def _():
        for b in range(nbat):
            rhs_ref[:, b * LANES:(b + 1) * LANES] = u_ref[b]
        rhs_ref[:, W:W + LANES] = hs_ref[...]
        rhs_ref[:, W + LANES:W + 2 * LANES] = hd_ref[...]
        acc_ref[...] = jnp.zeros_like(acc_ref)

    fp = f_ref[0]
    fq = f_ref[1]
    ap = _dot(fp, rhs_ref[:, 0:W + LANES])
    aq = _dot(fq, rhs_ref[:, 0:W])
    kq = _dot(fq, rhs_ref[:, W + LANES:W + 2 * LANES])
    kq0 = _dot(fq[0:16], rhs_ref[:, W:W + LANES])[0:1]
    first = jnp.logical_and(lax.broadcasted_iota(jnp.int32, (tk, LANES), 0) == 0, j == 0)
    kp = ap[:, W:W + LANES]
    kq = jnp.where(first, kq0, kq)
    wgt = jnp.where(first, 0.5 / L, 1.0 / L)
    yps = []
    yqs = []
    for b in range(nbat):
        up = ap[:, b * LANES:(b + 1) * LANES]
        uq = aq[:, b * LANES:(b + 1) * LANES]
        yp = jnp.where(first, up * kp, up * kp - uq * kq)
        yq = jnp.where(first, uq * kq, up * kq + uq * kp)
        yps.append((yp * wgt).astype(BF16))
        yqs.append((yq * wgt).astype(BF16))
    yp = jnp.concatenate(yps, axis=1)
    yq = jnp.concatenate(yqs, axis=1)
    acc_ref[...] += _dot_tn(fp, yp) + _dot_tn(fq, yq)

    @pl.when(j == nj - 1)
    def _():
        for b in range(nbat):
            y = acc_ref[:, b * LANES:(b + 1) * LANES] + u_ref[b].astype(F32) * skip_ref[...]
            y_ref[b] = y.astype(BF16)


def _lconv_call(u, hs, hd, ftab, skip, tk):
    B, L, _ = u.shape
    nct = HY_WIDTH // LANES
    return pl.pallas_call(
        _lconv_kernel,
        grid=(nct, L // tk),
        in_specs=[pl.BlockSpec((B, L, LANES), lambda c, j: (0, 0, c)),
                  pl.BlockSpec((L, LANES), lambda c, j: (0, c)),
                  pl.BlockSpec((L, LANES), lambda c, j: (0, c)),
                  pl.BlockSpec((2, tk, L), lambda c, j: (0, j, 0)),
                  pl.BlockSpec((1, LANES), lambda c, j: (0, c))],
        out_specs=pl.BlockSpec((B, L, LANES), lambda c, j: (0, 0, c)),
        out_shape=jax.ShapeDtypeStruct((B, L, HY_WIDTH), BF16),
        scratch_shapes=[pltpu.VMEM((L, (B + 2) * LANES), BF16),
                        pltpu.VMEM((L, B * LANES), F32)],
        compiler_params=_params(("parallel", "arbitrary")),
        name="lconv",
    )(u, hs, hd, ftab, skip)


def _merge_kernel(x_ref, mod_ref, g1_ref, g2_ref, ya_ref, yc_ref, x0_ref, wg_ref, woa_ref, woh_ref, wout_ref,
                  wq_ref, x1_ref, h2t_ref, qp_ref):
    m = mod_ref[0]
    x = x_ref[0]
    h = _norm_mod(x, g1_ref[...], m[0:1], m[1:2]).astype(BF16)
    D = x.shape[-1]
    gate_a = jax.nn.sigmoid(_dot(h, wg_ref[:, :D]))
    gate_h = jax.nn.sigmoid(_dot(h, wg_ref[:, D:]))
    y_hy = yc_ref[0] * x0_ref[0]
    merged = gate_a * _dot(ya_ref[0], woa_ref[...]) + gate_h * _dot(y_hy, woh_ref[...])
    x1 = x + m[2:3] * _dot(merged.astype(BF16), wout_ref[...])
    x1_ref[0] = x1
    h2 = _norm_mod(x1, g2_ref[...], m[3:4], m[4:5])
    h2t_ref[...] = h2.T.astype(BF16)
    qp_ref[0] = _dot(h2.astype(BF16), wq_ref[...])


def _merge_call(x, mod3, g1, g2, ya, yc, x0c, wg, woa, woh, wout, wq, tm):
    B, L, D = x.shape
    tok = lambda w: pl.BlockSpec((1, tm, w), lambda b, i: (b, i, 0))
    full = lambda a: pl.BlockSpec(a.shape, lambda b, i: (0, 0))
    return pl.pallas_call(
        _merge_kernel,
        grid=(B, L // tm),
        in_specs=[tok(D), pl.BlockSpec((1, 6, D), lambda b, i: (b, 0, 0)), full(g1), full(g2),
                  tok(Q_W), tok(HY_WIDTH), tok(HY_WIDTH), full(wg), full(woa), full(woh), full(wout), full(wq)],
        out_specs=[tok(D), pl.BlockSpec((D, tm), lambda b, i: (0, b * (L // tm) + i)), tok(D)],
        out_shape=[jax.ShapeDtypeStruct((B, L, D), F32),
                   jax.ShapeDtypeStruct((D, B * L), BF16),
                   jax.ShapeDtypeStruct((B, L, D), F32)],
        compiler_params=_params(("parallel", "arbitrary")),
        name="merge",
    )(x, mod3, g1, g2, ya, yc, x0c, wg, woa, woh, wout, wq)


def _topk_rank(s):
    n, t = s.shape
    rowi = lax.broadcasted_iota(jnp.int32, (n, t), 0).astype(F32)
    topi = lax.broadcasted_iota(jnp.int32, (PEER_TOPK, t), 0)

    def body(it, carry):
        s, rank, vals = carry
        mx = jnp.max(s, axis=0, keepdims=True)
        first = jnp.min(jnp.where(s == mx, rowi, float(n)), axis=0, keepdims=True)
        sel = rowi == first
        return (jnp.where(sel, NEG, s), jnp.where(sel, jnp.asarray(it, F32), rank), jnp.where(topi == it, mx, vals))

    init = (s, jnp.full((n, t), float(PEER_TOPK), F32), jnp.zeros((PEER_TOPK, t), F32))
    _, rank, vals = lax.fori_loop(0, PEER_TOPK, body, init)
    return rank, vals


def _cand_blocks():
    blocks = []
    for c in range(PEER_TOPK // 2):
        valid = PEER_TOPK // (c + 1)
        blocks.append((c, -(-valid // 8) * 8, valid))
    return blocks


def _cand_index(t):
    K = PEER_TOPK
    fis = []
    for c, rows, valid in _cand_blocks():
        ri = lax.broadcasted_iota(jnp.int32, (rows, t), 0)
        fis.append(jnp.where(ri < valid, ri * K + c, K * K))
    ci = lax.broadcasted_iota(jnp.int32, (8, t), 0)
    fis.append(ci + K // 2)
    return jnp.concatenate(fis, axis=0).astype(F32)


def _route_head_iterative(s1, s2, fi):
    K = PEER_TOPK
    t = s1.shape[1]
    blocks = _cand_blocks()
    ncand = fi.shape[0]
    rank1, a = _topk_rank(s1)
    rank2, b = _topk_rank(s2)
    cands = []
    for c, rows, valid in blocks:
        cands.append(a[0:rows] + b[c:c + 1])
    cands.append(a[0:1] + b[K // 2:K])
    cand = jnp.where(fi < K * K, jnp.concatenate(cands, axis=0), NEG)

    def body(it, carry):
        cand, sel_all = carry
        mx = jnp.max(cand, axis=0, keepdims=True)
        first = jnp.min(jnp.where(cand == mx, fi, float(K * K)), axis=0, keepdims=True)
        sel = fi == first
        return jnp.where(sel, NEG, cand), jnp.where(sel, 1.0, sel_all)

    _, sel = lax.fori_loop(0, K, body, (cand, jnp.zeros((ncand, t), F32)))
    ea = jnp.exp(a - a[0:1])
    eb = jnp.exp(b - b[0:1])
    cnt = jnp.zeros((K, t), F32)
    zsum = jnp.zeros((1, t), F32)
    off = 0
    for c, rows, valid in blocks:
        blk = sel[off:off + rows]
        off += rows
        if rows < K:
            blk_full = jnp.concatenate([blk, jnp.zeros((K - rows, t), F32)], axis=0)
        else:
            blk_full = blk
        cnt = cnt + blk_full
        zsum = zsum + jnp.sum(blk * ea[0:rows], axis=0, keepdims=True) * eb[c:c + 1]
    tail = sel[off:off + 8]
    tcount = jnp.sum(tail, axis=0, keepdims=True)
    row0 = lax.broadcasted_iota(jnp.int32, (K, t), 0) == 0
    cnt = cnt + jnp.where(row0, tcount, 0.0)
    zsum = zsum + jnp.sum(tail * eb[K // 2:K], axis=0, keepdims=True)
    c1 = jnp.zeros((PEER_KEYS, t), F32)
    for r in range(K):
        c1 = jnp.where(rank1 == r, cnt[r:r + 1], c1)
    e1 = jnp.where(rank1 < K, jnp.exp(s1 - a[0:1]), 0.0) * (0.5 / zsum)
    e2 = jnp.where(rank2 < K, jnp.exp(s2 - b[0:1]), 0.0)
    return rank2, e2, c1, e1


def _sort_network(n):
    def merge(lo, hi, r):
        step = r * 2
        if step < hi - lo:
            yield from merge(lo, hi, step)
            yield from merge(lo + r, hi, step)
            for i in range(lo + r, hi - r, step):
                yield (i, i + r)
        else:
            yield (lo, lo + r)

    def sort(lo, hi):
        if hi - lo >= 1:
            mid = lo + (hi - lo) // 2
            yield from sort(lo, mid)
            yield from sort(mid + 1, hi)
            yield from merge(lo, hi, 1)

    return list(sort(0, n - 1))


def _exchange(xs, i, j):
    hi = jnp.maximum(xs[i], xs[j])
    xs[j] = jnp.minimum(xs[i], xs[j])
    xs[i] = hi


def _bitonic_finish(xs):
    n = len(xs)
    d = n // 2
    while d >= 1:
        for i in range(n):
            if i & d == 0:
                _exchange(xs, i, i + d)
        d //= 2


def _merge_sublanes(xs):
    n = len(xs)
    for shift in (4, 2, 1):
        other = [pltpu.roll(x, shift, 0) for x in xs]
        xs = [jnp.maximum(xs[i], other[n - 1 - i]) for i in range(n)]
        _bitonic_finish(xs)
    return xs


def _top_sorted(s):
    xs = [s[8 * v:8 * v + 8] for v in range(s.shape[0] // 8)]
    for i, j in _sort_network(len(xs)):
        _exchange(xs, i, j)
    return _merge_sublanes(xs)


def _route_head_sorted(s1, s2):
    K = PEER_TOPK
    t = s1.shape[1]
    a = _top_sorted(s1)
    b = _top_sorted(s2)
    sub = lax.broadcasted_iota(jnp.int32, (8, t), 0)
    a8 = a[7]
    for r in range(6, -1, -1):
        a8 = jnp.where(sub == r, a[r], a8)
    main = [jnp.where((sub + 1) * (c + 1) <= K, a8 + b[c], NEG) for c in range(K)]
    top = _merge_sublanes(list(main))
    single = [a[8 + i] + b[0] for i in range(K // 2)] + [jnp.full((8, t), NEG, F32)] * (K // 2)
    top = [jnp.maximum(top[i], single[K - 1 - i]) for i in range(K)]
    _bitonic_finish(top)
    tau = top[K - 1]
    ea8 = jnp.exp(a8 - a[0])
    cnt8 = jnp.zeros((8, t), F32)
    z8 = jnp.zeros((8, t), F32)
    for c in range(K):
        hit = main[c] >= tau
        cnt8 = cnt8 + jnp.where(hit, 1.0, 0.0)
        z8 = z8 + jnp.where(hit, ea8 * jnp.exp(b[c] - b[0]), 0.0)
    cnt = [jnp.broadcast_to(cnt8[r:r + 1], (8, t)) for r in range(8)]
    zsum = jnp.sum(z8, axis=0, keepdims=True)
    total = jnp.sum(cnt8, axis=0, keepdims=True)
    for i in range(K // 2):
        hit = jnp.where(single[i] >= tau, 1.0, 0.0)
        cnt.append(hit)
        zsum = zsum + hit[0:1] * jnp.exp(a[8 + i][0:1] - a[0][0:1])
        total = total + hit[0:1]
    in1 = jnp.zeros((1, t), F32)
    in2 = jnp.zeros((1, t), F32)
    c1s, e1s, r2s, e2s = [], [], [], []
    scale = 0.5 / zsum
    for v in range(PEER_KEYS // 8):
        x1 = s1[8 * v:8 * v + 8]
        x2 = s2[8 * v:8 * v + 8]
        c1 = jnp.zeros((8, t), F32)
        r2 = jnp.full((8, t), float(K), F32)
        for r in range(K):
            c1 = jnp.where(x1 == a[r], cnt[r], c1)
            r2 = jnp.where(x2 == b[r], float(r), r2)
        top1 = x1 >= a[K - 1]
        top2 = x2 >= b[K - 1]
        in1 = in1 + jnp.sum(jnp.where(top1, 1.0, 0.0), axis=0, keepdims=True)
        in2 = in2 + jnp.sum(jnp.where(top2, 1.0, 0.0), axis=0, keepdims=True)
        c1s.append(c1)
        r2s.append(r2)
        e1s.append(jnp.where(top1, jnp.exp(x1 - a[0]), 0.0) * scale)
        e2s.append(jnp.where(top2, jnp.exp(x2 - b[0]), 0.0))
    bad = (in1 != float(K)) | (in2 != float(K)) | (total != float(K))
    for r in range(K - 1):
        bad = bad | (a[r][0:1] == a[r + 1][0:1]) | (b[r][0:1] == b[r + 1][0:1])
    cat = lambda xs: jnp.concatenate(xs, axis=0)
    return cat(r2s), cat(e2s), cat(c1s), cat(e1s), jnp.where(bad, 1.0, 0.0)


def _route_kernel(qp_ref, kb_ref, r2_ref, e2_ref, c1_ref, e1_ref):
    t = qp_ref.shape[0]
    for hh in range(PEER_HEADS):
        q = qp_ref[:, hh * PEER_DKEY:(hh + 1) * PEER_DKEY]
        st = _dot3(kb_ref[hh], q, _dot_nt)
        s1 = st[:PEER_KEYS]
        s2 = st[PEER_KEYS:]
        r2, e2, c1, e1, bad = _route_head_sorted(s1, s2)
        r2_ref[hh] = r2.astype(BF16)
        e2_ref[hh] = e2.astype(BF16)
        c1_ref[hh] = c1
        e1_ref[hh] = e1

        @pl.when(jnp.max(bad) > 0.0)
        def _():
            r2, e2, c1, e1 = _route_head_iterative(s1, s2, _cand_index(t))
            r2_ref[hh] = r2.astype(BF16)
            e2_ref[hh] = e2.astype(BF16)
            c1_ref[hh] = c1
            e1_ref[hh] = e1


def _route_call(qp, kb, tn):
    T = qp.shape[0]
    out = pl.BlockSpec((PEER_HEADS, PEER_KEYS, tn), lambda i: (0, 0, i))
    shp = jax.ShapeDtypeStruct((PEER_HEADS, PEER_KEYS, T), F32)
    shp16 = jax.ShapeDtypeStruct((PEER_HEADS, PEER_KEYS, T), BF16)
    return pl.pallas_call(
        _route_kernel,
        grid=(T // tn,),
        in_specs=[pl.BlockSpec((tn, PEER_HEADS * PEER_DKEY), lambda i: (i, 0)),
                  pl.BlockSpec(kb.shape, lambda i: (0, 0, 0))],
        out_specs=[out, out, out, out],
        out_shape=[shp16, shp16, shp, shp],
        compiler_params=_params(("parallel",)),
        name="route",
    )(qp, kb)


PEER_SUB = 1024
ROWS16 = 16


def _gelu_half(x):
    k = math.sqrt(2.0 / math.pi)
    return x + x * jnp.tanh(x * (k + (k * 0.044715) * (x * x)))


def _peer_kernel(h2_ref, u_ref, v_ref, r2_ref, e2_ref, c1_ref, e1_ref, x1_ref, g2_ref, fg_ref, o_ref, acc_ref,
                 w_ref, at_ref):
    e = pl.program_id(1)
    ne = pl.num_programs(1)
    te = u_ref.shape[0]

    @pl.when(e == 0)
    def _():
        acc_ref[...] = jnp.zeros_like(acc_ref)

    zero = jnp.zeros((), BF16)
    nsub = h2_ref.shape[1] // PEER_SUB
    toks = [slice(th * PEER_SUB, (th + 1) * PEER_SUB) for th in range(nsub)]

    def gates(th):
        tok = toks[th]
        for ii in range(te // PEER_KEYS):
            gs = [None] * (PEER_KEYS // ROWS16)
            for hh in range(PEER_HEADS):
                c1 = jnp.broadcast_to(c1_ref[hh, ii:ii + 1, tok], (ROWS16, PEER_SUB)).astype(BF16)
                e1 = jnp.broadcast_to(e1_ref[hh, ii:ii + 1, tok], (ROWS16, PEER_SUB)).astype(BF16)
                for k in range(PEER_KEYS // ROWS16):
                    rows = slice(k * ROWS16, (k + 1) * ROWS16)
                    term = jnp.where(r2_ref[hh, rows, tok] < c1, e2_ref[hh, rows, tok], zero) * e1
                    gs[k] = term if gs[k] is None else gs[k] + term
            for k in range(PEER_KEYS // ROWS16):
                lo = ii * PEER_KEYS + k * ROWS16
                w_ref[th, lo:lo + ROWS16, :] = gs[k]

    def project(th):
        at_ref[th] = _dot(u_ref[...], h2_ref[:, toks[th]])

    def activate(th):
        w_ref[th] = w_ref[th] * _gelu_half(at_ref[th]).astype(BF16)

    def combine(th):
        acc_ref[:, toks[th]] += _dot_tn(v_ref[...], w_ref[th])

    gates(0)
    project(0)
    for th in range(nsub):
        if th + 1 < nsub:
            gates(th + 1)
        activate(th)
        if th + 1 < nsub:
            project(th + 1)
        combine(th)

    @pl.when(e == ne - 1)
    def _():
        x2 = x1_ref[...] + g2_ref[0] * acc_ref[...].T
        y = x2 * lax.rsqrt(jnp.mean(x2 * x2, axis=-1, keepdims=True) + NORM_EPS) * fg_ref[...]
        o_ref[...] = y


def _peer_call(h2, u_tab, v_tab, r2, e2, c1, e1, x1, g2rows, fg, tn, te, toks_per_batch):
    D, T = h2.shape
    E = u_tab.shape[0]
    rows = te // PEER_KEYS
    tab = pl.BlockSpec((PEER_HEADS, PEER_KEYS, tn), lambda i, e: (0, 0, i))
    sel = pl.BlockSpec((PEER_HEADS, rows, tn), lambda i, e: (0, e, i))
    return pl.pallas_call(
        _peer_kernel,
        grid=(T // tn, E // te),
        in_specs=[pl.BlockSpec((D, tn), lambda i, e: (0, i)),
                  pl.BlockSpec((te, D), lambda i, e: (e, 0)),
                  pl.BlockSpec((te, D), lambda i, e: (e, 0)),
                  tab, tab, sel, sel,
                  pl.BlockSpec((tn, D), lambda i, e: (i, 0)),
                  pl.BlockSpec((1, 1, D), lambda i, e: ((i * tn) // toks_per_batch, 0, 0)),
                  pl.BlockSpec((1, D), lambda i, e: (0, 0))],
        out_specs=pl.BlockSpec((tn, D), lambda i, e: (i, 0)),
        out_shape=jax.ShapeDtypeStruct((T, D), F32),
        scratch_shapes=[pltpu.VMEM((D, tn), F32), pltpu.VMEM((tn // PEER_SUB, te, PEER_SUB), BF16),
                        pltpu.VMEM((tn // PEER_SUB, te, PEER_SUB), F32)],
        compiler_params=_params(("parallel", "arbitrary")),
        name="peer",
    )(h2, u_tab, v_tab, r2, e2, c1, e1, x1, g2rows, fg)


def _rope_tables(L):
    rows = L // GRID_W
    row = jnp.repeat(jnp.arange(rows, dtype=jnp.int32), GRID_W).astype(F32)
    col = jnp.tile(jnp.arange(GRID_W, dtype=jnp.int32), rows).astype(F32)
    f = HEAD_DIM // 4
    inv = ROPE_BASE ** (-jnp.arange(f, dtype=F32) / f)
    ang_r = row[:, None] * inv[None, :]
    ang_c = col[:, None] * inv[None, :]
    ang = jnp.concatenate([ang_r, ang_r, ang_c, ang_c], axis=-1)
    reps = ROT_W // HEAD_DIM
    return jnp.tile(jnp.cos(ang), (1, reps)), jnp.tile(jnp.sin(ang), (1, reps))


def _rot_cols(w):
    n = w.shape[1]
    f = HEAD_DIM // 4
    d = jnp.arange(n) % (2 * f)
    src = jnp.where(d < f, jnp.arange(n) + f, jnp.arange(n) - f)
    sign = jnp.where(d < f, -1.0, 1.0).astype(w.dtype)
    return w[:, src] * sign[None, :]


def _dup_cols(w):
    return jnp.concatenate([w[:, :HEAD_DIM], w[:, :HEAD_DIM], w[:, HEAD_DIM:], w[:, HEAD_DIM:]], axis=1)


def _dft_table(L):
    N = 2 * L
    s = 64
    kh = jnp.arange(L // s, dtype=jnp.int32)
    kl = jnp.arange(s, dtype=jnp.int32)
    n = jnp.arange(L, dtype=jnp.int32)
    ph1 = ((kh[:, None] * s * n[None, :]) % N).astype(F32) * (2.0 * math.pi / N)
    ph2 = ((kl[:, None] * n[None, :]) % N).astype(F32) * (2.0 * math.pi / N)
    c1, s1 = jnp.cos(ph1)[:, None, :], jnp.sin(ph1)[:, None, :]
    c2, s2 = jnp.cos(ph2)[None, :, :], jnp.sin(ph2)[None, :, :]
    cosm = (c1 * c2 - s1 * s2).reshape(L, L)
    sinm = (s1 * c2 + c1 * s2).reshape(L, L)
    nyq = jnp.where(n % 2 == 0, 1.0, -1.0).astype(F32)
    k = jnp.arange(L, dtype=jnp.int32)
    q = jnp.where(k[:, None] == 0, nyq[None, :], -sinm)
    return jnp.stack([cosm, q]).astype(BF16)


def _filter_features(L):
    t = jnp.arange(L, dtype=F32) / L
    bands = jnp.arange(1, HY_EMB_BANDS + 1, dtype=F32)
    ang = 2.0 * math.pi * t[:, None] * bands[None, :]
    z = jnp.concatenate([t[:, None], jnp.cos(ang), jnp.sin(ang)], axis=-1)
    return jnp.pad(z, ((0, 0), (0, LANES - z.shape[1])))


def _tile(n, pref):
    return pref if n % pref == 0 else n


def kernel(x, c, ctx, c_ctx, w_mod, b_mod, norm1_g, w_in, attn_sink, hy_conv_w, hy_conv_b, hy_fw1, hy_fb1, hy_fw2, hy_fb2, hy_fw3, hy_fb3, hy_freq, hy_skip, w_o_attn, w_o_hy, w_out, norm2_g, peer_wq, peer_keys, peer_u, peer_v, final_g):
    B, L, D = x.shape
    assert B == 4 and D == D_MODEL and w_mod.shape[0] == 1
    T = B * L
    li = 0

    c8 = jnp.concatenate([c, c_ctx[None, :], jnp.zeros((3, D), F32)], axis=0)
    mod3 = _mod_call(c8, w_mod[li], b_mod[li]).reshape(8, 6, D)
    g1 = norm1_g[li].reshape(1, D)
    g2 = norm2_g[li].reshape(1, D)

    w = w_in[li]
    wq, wk, wv = w[:, :OFF_K], _dup_cols(w[:, OFF_K:OFF_V]), _dup_cols(w[:, OFF_V:OFF_HY])
    w_cat = jnp.concatenate([wq, wk, wv, w[:, OFF_HY:OFF_G], _rot_cols(wq), _rot_cols(wk)], axis=1).astype(BF16)
    cos_t, sin_t = _rope_tables(L)
    q, k, v, hy = _inproj_call(x, mod3, g1, w_cat, cos_t, sin_t, _tile(L, 512))
    kx, vx = _ctxproj_call(ctx, mod3, g1, jnp.concatenate([wk, wv], axis=1).astype(BF16))

    gsz = N_HEADS // N_KV_HEADS
    sink_b = jnp.broadcast_to(
        jnp.repeat(attn_sink[li].astype(F32).reshape(N_KV_HEADS, gsz), BLOCK, axis=1)[:, :, None],
        (N_KV_HEADS, gsz * BLOCK, LANES))
    y_attn = _attn_call(q, k, v, kx, vx, sink_b)

    H = HY_FILTER_HIDDEN
    w1p = jnp.pad(hy_fw1[li], ((0, LANES - hy_fw1.shape[1]), (0, 0)))
    deltas = jnp.abs(jnp.linspace(math.log(HY_DECAY_TARGET) / HY_SLOW_DECAY,
                                  math.log(HY_DECAY_TARGET) / HY_FAST_DECAY, HY_WIDTH, dtype=F32)).reshape(1, -1)
    hs, hd = _filt_call(_filter_features(L), w1p, hy_fb1[li].reshape(1, H), hy_freq[li].reshape(1, H),
                        hy_fw2[li], hy_fb2[li].reshape(1, H), hy_fw3[li], hy_fb3[li].reshape(1, -1), deltas)
    u, x0c = _sconv_call(hy, hy_conv_w[li], hy_conv_b[li].reshape(1, -1))
    yc = _lconv_call(u, hs, hd, _dft_table(L), hy_skip[li].reshape(1, -1), _tile(L, 256))

    x1, h2t, qp = _merge_call(x, mod3, g1, g2, y_attn, yc, x0c, w[:, OFF_G:].astype(BF16),
                             w_o_attn[li].astype(BF16), w_o_hy[li].astype(BF16), w_out[li].astype(BF16),
                             peer_wq[li].astype(BF16), _tile(L, 256))

    keys = peer_keys[li]
    zk = jnp.zeros_like(keys[:, 0])
    kb = jnp.concatenate([jnp.concatenate([keys[:, 0], zk], axis=2),
                          jnp.concatenate([zk, keys[:, 1]], axis=2)], axis=1)
    r2, e2, c1, e1 = _route_call(qp.reshape(T, D), kb, _tile(T, 256))

    out = _peer_call(h2t, peer_u[li].astype(BF16), peer_v[li].astype(BF16), r2, e2, c1, e1,
                     x1.reshape(T, D), mod3[:B, 5:6, :], final_g.reshape(1, D),
                     _tile(T, 1024), 1024, L)
    return out.reshape(B, L, D)
```

```python
import functools
import math

import jax
import jax.numpy as jnp
from jax import lax
from jax.experimental import pallas as pl
from jax.experimental.pallas import tpu as pltpu

F32 = jnp.float32
BF16 = jnp.bfloat16

D_MODEL = 1024
GRID_W = 64
NORM_EPS = 1e-6
N_HEADS = 8
N_KV_HEADS = 2
HEAD_DIM = 64
BLOCK = 128
ROPE_BASE = 10000.0
HY_WIDTH = 512
HY_EMB_BANDS = 16
HY_FILTER_HIDDEN = 64
HY_FAST_DECAY = 0.3
HY_SLOW_DECAY = 1.5
HY_DECAY_TARGET = 1e-2
PEER_HEADS = 8
PEER_KEYS = 128
PEER_TOPK = 16
PEER_DKEY = 128
Q_W = N_HEADS * HEAD_DIM
KV_W = N_KV_HEADS * HEAD_DIM
HY_IN = 3 * HY_WIDTH
OFF_K = Q_W
OFF_V = OFF_K + KV_W
OFF_HY = OFF_V + KV_W
OFF_G = OFF_HY + HY_IN

LANES = 128
VMEM_LIMIT = 56 * 1024 * 1024
NEG = -1e30
KV_DUP = 2 * KV_W


def _params(sem):
    return pltpu.CompilerParams(dimension_semantics=sem, vmem_limit_bytes=VMEM_LIMIT)


def _dot(a, b):
    return lax.dot_general(a, b, (((1,), (0,)), ((), ())), preferred_element_type=F32)


def _dot_nt(a, b):
    return lax.dot_general(a, b, (((1,), (1,)), ((), ())), preferred_element_type=F32)


def _dot_tn(a, b):
    return lax.dot_general(a, b, (((0,), (0,)), ((), ())), preferred_element_type=F32)


def _split(a):
    hi = a.astype(BF16)
    lo = (a - hi.astype(F32)).astype(BF16)
    return hi, lo


def _dot3(a, b, dot=_dot):
    ah, al = _split(a)
    bh, bl = _split(b)
    return dot(ah, bh) + dot(ah, bl) + dot(al, bh)


def _norm_mod(x, g, shift, scale):
    y = x * lax.rsqrt(jnp.mean(x * x, axis=-1, keepdims=True) + NORM_EPS) * g
    return y * (1.0 + scale) + shift


def _mod_kernel(c_ref, w_ref, b_ref, o_ref):
    c = c_ref[...]
    s = c * jax.nn.sigmoid(c)
    o_ref[...] = _dot3(s, w_ref[...]) + b_ref[...]


def _mod_call(c8, w_mod, b_mod):
    n = w_mod.shape[1] // D_MODEL
    return pl.pallas_call(
        _mod_kernel,
        grid=(n,),
        in_specs=[pl.BlockSpec((8, D_MODEL), lambda j: (0, 0)),
                  pl.BlockSpec((D_MODEL, D_MODEL), lambda j: (0, j)),
                  pl.BlockSpec((1, D_MODEL), lambda j: (0, j))],
        out_specs=pl.BlockSpec((8, D_MODEL), lambda j: (0, j)),
        out_shape=jax.ShapeDtypeStruct((8, w_mod.shape[1]), F32),
        compiler_params=_params(("arbitrary",)),
        name="mod",
    )(c8, w_mod, b_mod.reshape(1, -1))


ROT_W = Q_W + KV_DUP
CAT_W = Q_W + 2 * KV_DUP + HY_IN + ROT_W


def _inproj_kernel(x_ref, mod_ref, g_ref, w_ref, cos_ref, sin_ref, q_ref, k_ref, v_ref, hy_ref):
    m = mod_ref[0]
    h = _norm_mod(x_ref[0], g_ref[...], m[0:1], m[1:2]).astype(BF16)
    o_v = ROT_W
    o_hy = o_v + KV_DUP
    o_rot = o_hy + HY_IN
    p = _dot(h, w_ref[:, 0:ROT_W])
    pr = _dot(h, w_ref[:, o_rot:o_rot + ROT_W])
    qk = p * cos_ref[...] + pr * sin_ref[...]
    q_ref[0] = qk[:, :Q_W].astype(BF16)
    k_ref[0] = qk[:, Q_W:].astype(BF16)
    v_ref[0] = _dot(h, w_ref[:, o_v:o_hy]).astype(BF16)
    for j in range(HY_IN // HY_WIDTH):
        lo = o_hy + j * HY_WIDTH
        hy_ref[0, :, j * HY_WIDTH:(j + 1) * HY_WIDTH] = _dot(h, w_ref[:, lo:lo + HY_WIDTH]).astype(BF16)


def _inproj_call(x, mod3, g, w_cat, cos_t, sin_t, tm):
    B, L, D = x.shape
    return pl.pallas_call(
        _inproj_kernel,
        grid=(B, L // tm),
        in_specs=[pl.BlockSpec((1, tm, D), lambda b, i: (b, i, 0)),
                  pl.BlockSpec((1, 6, D), lambda b, i: (b, 0, 0)),
                  pl.BlockSpec((1, D), lambda b, i: (0, 0)),
                  pl.BlockSpec((D, CAT_W), lambda b, i: (0, 0)),
                  pl.BlockSpec((tm, ROT_W), lambda b, i: (i, 0)),
                  pl.BlockSpec((tm, ROT_W), lambda b, i: (i, 0))],
        out_specs=[pl.BlockSpec((1, tm, Q_W), lambda b, i: (b, i, 0)),
                   pl.BlockSpec((1, tm, KV_DUP), lambda b, i: (b, i, 0)),
                   pl.BlockSpec((1, tm, KV_DUP), lambda b, i: (b, i, 0)),
                   pl.BlockSpec((1, tm, HY_IN), lambda b, i: (b, i, 0))],
        out_shape=[jax.ShapeDtypeStruct((B, L, Q_W), BF16),
                   jax.ShapeDtypeStruct((B, L, KV_DUP), BF16),
                   jax.ShapeDtypeStruct((B, L, KV_DUP), BF16),
                   jax.ShapeDtypeStruct((B, L, HY_IN), BF16)],
        compiler_params=_params(("parallel", "arbitrary")),
        name="inproj",
    )(x, mod3, g, w_cat, cos_t, sin_t)


def _ctxproj_kernel(x_ref, mod_ref, g_ref, w_ref, k_ref, v_ref):
    m = mod_ref[0]
    h = _norm_mod(x_ref[0], g_ref[...], m[0:1], m[1:2]).astype(BF16)
    k_ref[0] = _dot(h, w_ref[:, :KV_DUP]).astype(BF16)
    v_ref[0] = _dot(h, w_ref[:, KV_DUP:]).astype(BF16)


def _ctxproj_call(ctx, mod3, g, w_kv):
    B, C, D = ctx.shape
    return pl.pallas_call(
        _ctxproj_kernel,
        grid=(B,),
        in_specs=[pl.BlockSpec((1, C, D), lambda b: (b, 0, 0)),
                  pl.BlockSpec((1, 6, D), lambda b: (4, 0, 0)),
                  pl.BlockSpec((1, D), lambda b: (0, 0)),
                  pl.BlockSpec((D, 2 * KV_DUP), lambda b: (0, 0))],
        out_specs=[pl.BlockSpec((1, C, KV_DUP), lambda b: (b, 0, 0)),
                   pl.BlockSpec((1, C, KV_DUP), lambda b: (b, 0, 0))],
        out_shape=[jax.ShapeDtypeStruct((B, C, KV_DUP), BF16),
                   jax.ShapeDtypeStruct((B, C, KV_DUP), BF16)],
        compiler_params=_params(("arbitrary",)),
        name="ctxproj",
    )(ctx, mod3, g, w_kv)


def _attn_kernel(q_ref, kp_ref, kc_ref, kn_ref, vp_ref, vc_ref, vn_ref, kx_ref, vx_ref, sink_ref, o_ref):
    n = pl.program_id(1)
    nb = pl.num_programs(1)
    q = q_ref[0]
    rows = (N_HEADS // N_KV_HEADS) * BLOCK
    lo = lax.broadcasted_iota(jnp.int32, (BLOCK, LANES), 1) < HEAD_DIM
    r = lax.broadcasted_iota(jnp.int32, (rows, BLOCK), 0) % BLOCK
    c = lax.broadcasted_iota(jnp.int32, (rows, BLOCK), 1)
    ok_prev = jnp.logical_and(c >= r, n > 0)
    ok_next = jnp.logical_and(c <= r, n < nb - 1)
    scale = HEAD_DIM ** -0.5
    zero = jnp.zeros((BLOCK, LANES), BF16)
    for g in range(N_KV_HEADS):
        sl = slice(g * LANES, (g + 1) * LANES)
        qa = q[:, 2 * g * LANES:(2 * g + 1) * LANES]
        qb = q[:, (2 * g + 1) * LANES:(2 * g + 2) * LANES]
        lhs = jnp.concatenate([jnp.where(lo, qa, zero), jnp.where(lo, zero, qa),
                               jnp.where(lo, qb, zero), jnp.where(lo, zero, qb)], axis=0)
        s_p = jnp.where(ok_prev, _dot_nt(lhs, kp_ref[0, :, sl]) * scale, NEG)
        s_c = _dot_nt(lhs, kc_ref[0, :, sl]) * scale
        s_n = jnp.where(ok_next, _dot_nt(lhs, kn_ref[0, :, sl]) * scale, NEG)
        s_x = _dot_nt(lhs, kx_ref[0, :, sl]) * scale
        sink = sink_ref[g][:, 0:1]
        half = s_x.shape[1] // 2
        m = jnp.max(jnp.maximum(jnp.maximum(s_p, s_c), jnp.maximum(s_n, jnp.maximum(s_x[:, :half], s_x[:, half:]))),
                    axis=-1, keepdims=True)
        m = jnp.maximum(m, sink)
        p_p = jnp.exp(s_p - m)
        p_c = jnp.exp(s_c - m)
        p_n = jnp.exp(s_n - m)
        p_x = jnp.exp(s_x - m)
        den = (jnp.sum((p_p + p_c) + (p_n + (p_x[:, :half] + p_x[:, half:])), axis=-1, keepdims=True)
               + jnp.exp(sink - m))
        o = (_dot(p_p.astype(BF16), vp_ref[0, :, sl]) + _dot(p_c.astype(BF16), vc_ref[0, :, sl])
             + _dot(p_n.astype(BF16), vn_ref[0, :, sl]) + _dot(p_x.astype(BF16), vx_ref[0, :, sl]))
        o = o / den
        o_ref[0, :, 2 * g * LANES:(2 * g + 1) * LANES] = jnp.where(
            lo, o[0:BLOCK], o[BLOCK:2 * BLOCK]).astype(BF16)
        o_ref[0, :, (2 * g + 1) * LANES:(2 * g + 2) * LANES] = jnp.where(
            lo, o[2 * BLOCK:3 * BLOCK], o[3 * BLOCK:4 * BLOCK]).astype(BF16)


def _attn_call(q, k, v, kx, vx, sink_b):
    B, L, _ = q.shape
    C = kx.shape[1]
    nb = L // BLOCK
    kv = lambda f: pl.BlockSpec((1, BLOCK, KV_DUP), f)
    prev = lambda b, n: (b, jnp.maximum(n - 1, 0), 0)
    cur = lambda b, n: (b, n, 0)
    nxt = lambda b, n: (b, jnp.minimum(n + 1, nb - 1), 0)
    rows = (N_HEADS // N_KV_HEADS) * BLOCK
    return pl.pallas_call(
        _attn_kernel,
        grid=(B, nb),
        in_specs=[pl.BlockSpec((1, BLOCK, Q_W), cur),
                  kv(prev), kv(cur), kv(nxt), kv(prev), kv(cur), kv(nxt),
                  pl.BlockSpec((1, C, KV_DUP), lambda b, n: (b, 0, 0)),
                  pl.BlockSpec((1, C, KV_DUP), lambda b, n: (b, 0, 0)),
                  pl.BlockSpec((N_KV_HEADS, rows, LANES), lambda b, n: (0, 0, 0))],
        out_specs=pl.BlockSpec((1, BLOCK, Q_W), cur),
        out_shape=jax.ShapeDtypeStruct((B, L, Q_W), BF16),
        compiler_params=_params(("parallel", "arbitrary")),
        name="attn",
    )(q, k, k, k, v, v, v, kx, vx, sink_b)


def _filt_kernel(z_ref, w1_ref, b1_ref, fr_ref, w2_ref, b2_ref, w3f_ref, w3b_ref, b3f_ref, b3b_ref,
                 dl_ref, hs_ref, hd_ref, h_ref):
    L = z_ref.shape[0]
    z = z_ref[...]

    @pl.when(pl.program_id(0) == 0)
    def _():
        fr = fr_ref[...]
        h1 = jnp.sin(fr * (_dot3(z, w1_ref[...]) + b1_ref[...]))
        h_ref[...] = jnp.sin(fr * (_dot3(h1, w2_ref[...]) + b2_ref[...]))

    h = h_ref[...]
    decay = jnp.exp(-z[:, 0:1] * dl_ref[...])
    hf = (_dot3(h, w3f_ref[...]) + b3f_ref[...]) * decay
    hb = (_dot3(h, w3b_ref[...]) + b3b_ref[...]) * decay
    row = lax.broadcasted_iota(jnp.int32, hb.shape, 0)
    hb = jnp.where(row < L - 1, hb, 0.0)
    norm = jnp.sum(jnp.abs(hf), axis=0, keepdims=True) + jnp.sum(jnp.abs(hb), axis=0, keepdims=True)
    inv = 1.0 / norm
    hf = hf * inv
    hbs = jnp.where(row >= 1, pltpu.roll(hb, 1, 0), 0.0) * inv
    hs_ref[...] = (hf + hbs).astype(BF16)
    hd_ref[...] = (hf - hbs).astype(BF16)


def _filt_call(zf, w1p, b1, fr, w2, b2, w3, b3, absdelta):
    L = zf.shape[0]
    nct = HY_WIDTH // LANES
    H = HY_FILTER_HIDDEN
    full = lambda shape: pl.BlockSpec(shape, lambda j: (0, 0))
    return pl.pallas_call(
        _filt_kernel,
        grid=(nct,),
        in_specs=[full((L, LANES)), full((LANES, H)), full((1, H)), full((1, H)), full((H, H)), full((1, H)),
                  pl.BlockSpec((H, LANES), lambda j: (0, j)),
                  pl.BlockSpec((H, LANES), lambda j: (0, nct + j)),
                  pl.BlockSpec((1, LANES), lambda j: (0, j)),
                  pl.BlockSpec((1, LANES), lambda j: (0, nct + j)),
                  pl.BlockSpec((1, LANES), lambda j: (0, j))],
        out_specs=[pl.BlockSpec((L, LANES), lambda j: (0, j)),
                   pl.BlockSpec((L, LANES), lambda j: (0, j))],
        out_shape=[jax.ShapeDtypeStruct((L, HY_WIDTH), BF16),
                   jax.ShapeDtypeStruct((L, HY_WIDTH), BF16)],
        scratch_shapes=[pltpu.VMEM((L, H), F32)],
        compiler_params=_params(("arbitrary",)),
        name="filt",
    )(zf, w1p, b1, fr, w2, b2, w3, w3, b3, b3, absdelta)


def _sconv_kernel(x0_ref, x1_ref, v_ref, w0_ref, w1_ref, w2_ref, b0_ref, b1_ref, b2_ref, u_ref, g_ref):
    L = x0_ref.shape[1]
    row = lax.broadcasted_iota(jnp.int32, (L, LANES), 0)

    def conv(z_ref, w_ref, b_ref):
        z = z_ref[0].astype(F32)
        w = w_ref[...]
        zp = jnp.where(row >= 1, pltpu.roll(z, 1, 0), 0.0)
        zn = jnp.where(row < L - 1, pltpu.roll(z, L - 1, 0), 0.0)
        return zp * w[0:1] + z * w[1:2] + zn * w[2:3] + b_ref[...]

    g_ref[0] = conv(x0_ref, w0_ref, b0_ref).astype(BF16)
    u_ref[0] = (conv(v_ref, w2_ref, b2_ref) * conv(x1_ref, w1_ref, b1_ref)).astype(BF16)


def _sconv_call(hy, conv_w, conv_b):
    B, L, _ = hy.shape
    nct = HY_WIDTH // LANES
    zs = lambda part: pl.BlockSpec((1, L, LANES), lambda b, j: (b, 0, part * nct + j))
    ws = lambda part: pl.BlockSpec((3, LANES), lambda b, j: (0, part * nct + j))
    bs = lambda part: pl.BlockSpec((1, LANES), lambda b, j: (0, part * nct + j))
    out = pl.BlockSpec((1, L, LANES), lambda b, j: (b, 0, j))
    return pl.pallas_call(
        _sconv_kernel,
        grid=(B, nct),
        in_specs=[zs(0), zs(1), zs(2), ws(0), ws(1), ws(2), bs(0), bs(1), bs(2)],
        out_specs=[out, out],
        out_shape=[jax.ShapeDtypeStruct((B, L, HY_WIDTH), BF16),
                   jax.ShapeDtypeStruct((B, L, HY_WIDTH), BF16)],
        compiler_params=_params(("parallel", "arbitrary")),
        name="sconv",
    )(hy, hy, hy, conv_w, conv_w, conv_w, conv_b, conv_b, conv_b)


def _lconv_kernel(u_ref, hs_ref, hd_ref, fp_ref, fq_ref, skip_ref, y_ref, rhs_ref, acc_ref):
    nbat, L, _ = u_ref.shape
    j = pl.program_id(1)
    nj = pl.num_programs(1)
    tk = fp_ref.shape[0]
    W = nbat * LANES

    @pl.when(j == 0)
    def _():
        for b in range(nbat):
            rhs_ref[:, b * LANES:(b + 1) * LANES] = u_ref[b]
        rhs_ref[:, W:W + LANES] = hs_ref[...]
        rhs_ref[:, W + LANES:W + 2 * LANES] = hd_ref[...]
        acc_ref[...] = jnp.zeros_like(acc_ref)

    fp = fp_ref[...]
    fq = fq_ref[...]
    ap = _dot(fp, rhs_ref[:, 0:W + LANES])
    aq = _dot(fq, rhs_ref[:, 0:W])
    kq = _dot(fq, rhs_ref[:, W + LANES:W + 2 * LANES])
    kq0 = _dot(fq[0:16], rhs_ref[:, W:W + LANES])[0:1]
    first = jnp.logical_and(lax.broadcasted_iota(jnp.int32, (tk, LANES), 0) == 0, j == 0)
    kp = ap[:, W:W + LANES]
    kq = jnp.where(first, kq0, kq)
    wgt = jnp.where(first, 0.5 / L, 1.0 / L)
    yps = []
    yqs = []
    for b in range(nbat):
        up = ap[:, b * LANES:(b + 1) * LANES]
        uq = aq[:, b * LANES:(b + 1) * LANES]
        yp = jnp.where(first, up * kp, up * kp - uq * kq)
        yq = jnp.where(first, uq * kq, up * kq + uq * kp)
        yps.append((yp * wgt).astype(BF16))
        yqs.append((yq * wgt).astype(BF16))
    yp = jnp.concatenate(yps, axis=1)
    yq = jnp.concatenate(yqs, axis=1)
    acc_ref[...] += _dot_tn(fp, yp) + _dot_tn(fq, yq)

    @pl.when(j == nj - 1)
    def _():
        for b in range(nbat):
            y = acc_ref[:, b * LANES:(b + 1) * LANES] + u_ref[b].astype(F32) * skip_ref[...]
            y_ref[b] = y.astype(BF16)


def _lconv_call(u, hs, hd, ftab, skip, tk):
    B, L, _ = u.shape
    nct = HY_WIDTH // LANES
    return pl.pallas_call(
        _lconv_kernel,
        grid=(nct, L // tk),
        in_specs=[pl.BlockSpec((B, L, LANES), lambda c, j: (0, 0, c)),
                  pl.BlockSpec((L, LANES), lambda c, j: (0, c)),
                  pl.BlockSpec((L, LANES), lambda c, j: (0, c)),
                  pl.BlockSpec((tk, L), lambda c, j: (j, 0)),
                  pl.BlockSpec((tk, L), lambda c, j: (j, 0)),
                  pl.BlockSpec((1, LANES), lambda c, j: (0, c))],
        out_specs=pl.BlockSpec((B, L, LANES), lambda c, j: (0, 0, c)),
        out_shape=jax.ShapeDtypeStruct((B, L, HY_WIDTH), BF16),
        scratch_shapes=[pltpu.VMEM((L, (B + 2) * LANES), BF16),
                        pltpu.VMEM((L, B * LANES), F32)],
        compiler_params=_params(("parallel", "arbitrary")),
        name="lconv",
    )(u, hs, hd, ftab[0], ftab[1], skip)


def _merge_kernel(x_ref, mod_ref, g1_ref, g2_ref, ya_ref, yc_ref, x0_ref, wg_ref, woa_ref, woh_ref, wout_ref,
                  wq_ref, x1_ref, h2t_ref, qp_ref):
    m = mod_ref[0]
    x = x_ref[0]
    h = _norm_mod(x, g1_ref[...], m[0:1], m[1:2]).astype(BF16)
    D = x.shape[-1]
    gate_a = jax.nn.sigmoid(_dot(h, wg_ref[:, :D]))
    gate_h = jax.nn.sigmoid(_dot(h, wg_ref[:, D:]))
    y_hy = yc_ref[0] * x0_ref[0]
    merged = gate_a * _dot(ya_ref[0], woa_ref[...]) + gate_h * _dot(y_hy, woh_ref[...])
    x1 = x + m[2:3] * _dot(merged.astype(BF16), wout_ref[...])
    x1_ref[0] = x1
    h2 = _norm_mod(x1, g2_ref[...], m[3:4], m[4:5])
    h2t_ref[...] = h2.T.astype(BF16)
    qp_ref[0] = _dot(h2.astype(BF16), wq_ref[...])


def _merge_call(x, mod3, g1, g2, ya, yc, x0c, wg, woa, woh, wout, wq, tm):
    B, L, D = x.shape
    tok = lambda w: pl.BlockSpec((1, tm, w), lambda b, i: (b, i, 0))
    full = lambda a: pl.BlockSpec(a.shape, lambda b, i: (0, 0))
    return pl.pallas_call(
        _merge_kernel,
        grid=(B, L // tm),
        in_specs=[tok(D), pl.BlockSpec((1, 6, D), lambda b, i: (b, 0, 0)), full(g1), full(g2),
                  tok(Q_W), tok(HY_WIDTH), tok(HY_WIDTH), full(wg), full(woa), full(woh), full(wout), full(wq)],
        out_specs=[tok(D), pl.BlockSpec((D, tm), lambda b, i: (0, b * (L // tm) + i)), tok(D)],
        out_shape=[jax.ShapeDtypeStruct((B, L, D), F32),
                   jax.ShapeDtypeStruct((D, B * L), BF16),
                   jax.ShapeDtypeStruct((B, L, D), F32)],
        compiler_params=_params(("parallel", "arbitrary")),
        name="merge",
    )(x, mod3, g1, g2, ya, yc, x0c, wg, woa, woh, wout, wq)


def _topk_rank(s):
    n, t = s.shape
    rowi = lax.broadcasted_iota(jnp.int32, (n, t), 0).astype(F32)
    topi = lax.broadcasted_iota(jnp.int32, (PEER_TOPK, t), 0)

    def body(it, carry):
        s, rank, vals = carry
        mx = jnp.max(s, axis=0, keepdims=True)
        first = jnp.min(jnp.where(s == mx, rowi, float(n)), axis=0, keepdims=True)
        sel = rowi == first
        return (jnp.where(sel, NEG, s), jnp.where(sel, jnp.asarray(it, F32), rank), jnp.where(topi == it, mx, vals))

    init = (s, jnp.full((n, t), float(PEER_TOPK), F32), jnp.zeros((PEER_TOPK, t), F32))
    _, rank, vals = lax.fori_loop(0, PEER_TOPK, body, init)
    return rank, vals


def _cand_blocks():
    blocks = []
    for c in range(PEER_TOPK // 2):
        valid = PEER_TOPK // (c + 1)
        blocks.append((c, -(-valid // 8) * 8, valid))
    return blocks


def _cand_index(t):
    K = PEER_TOPK
    fis = []
    for c, rows, valid in _cand_blocks():
        ri = lax.broadcasted_iota(jnp.int32, (rows, t), 0)
        fis.append(jnp.where(ri < valid, ri * K + c, K * K))
    ci = lax.broadcasted_iota(jnp.int32, (8, t), 0)
    fis.append(ci + K // 2)
    return jnp.concatenate(fis, axis=0).astype(F32)


def _route_head_iterative(s1, s2, fi):
    K = PEER_TOPK
    t = s1.shape[1]
    blocks = _cand_blocks()
    ncand = fi.shape[0]
    rank1, a = _topk_rank(s1)
    rank2, b = _topk_rank(s2)
    cands = []
    for c, rows, valid in blocks:
        cands.append(a[0:rows] + b[c:c + 1])
    cands.append(a[0:1] + b[K // 2:K])
    cand = jnp.where(fi < K * K, jnp.concatenate(cands, axis=0), NEG)

    def body(it, carry):
        cand, sel_all = carry
        mx = jnp.max(cand, axis=0, keepdims=True)
        first = jnp.min(jnp.where(cand == mx, fi, float(K * K)), axis=0, keepdims=True)
        sel = fi == first
        return jnp.where(sel, NEG, cand), jnp.where(sel, 1.0, sel_all)

    _, sel = lax.fori_loop(0, K, body, (cand, jnp.zeros((ncand, t), F32)))
    ea = jnp.exp(a - a[0:1])
    eb = jnp.exp(b - b[0:1])
    cnt = jnp.zeros((K, t), F32)
    zsum = jnp.zeros((1, t), F32)
    off = 0
    for c, rows, valid in blocks:
        blk = sel[off:off + rows]
        off += rows
        if rows < K:
            blk_full = jnp.concatenate([blk, jnp.zeros((K - rows, t), F32)], axis=0)
        else:
            blk_full = blk
        cnt = cnt + blk_full
        zsum = zsum + jnp.sum(blk * ea[0:rows], axis=0, keepdims=True) * eb[c:c + 1]
    tail = sel[off:off + 8]
    tcount = jnp.sum(tail, axis=0, keepdims=True)
    row0 = lax.broadcasted_iota(jnp.int32, (K, t), 0) == 0
    cnt = cnt + jnp.where(row0, tcount, 0.0)
    zsum = zsum + jnp.sum(tail * eb[K // 2:K], axis=0, keepdims=True)
    c1 = jnp.zeros((PEER_KEYS, t), F32)
    for r in range(K):
        c1 = jnp.where(rank1 == r, cnt[r:r + 1], c1)
    e1 = jnp.where(rank1 < K, jnp.exp(s1 - a[0:1]), 0.0) * (0.5 / zsum)
    e2 = jnp.where(rank2 < K, jnp.exp(s2 - b[0:1]), 0.0)
    return rank2, e2, c1, e1


def _sort_network(n):
    def merge(lo, hi, r):
        step = r * 2
        if step < hi - lo:
            yield from merge(lo, hi, step)
            yield from merge(lo + r, hi, step)
            for i in range(lo + r, hi - r, step):
                yield (i, i + r)
        else:
            yield (lo, lo + r)

    def sort(lo, hi):
        if hi - lo >= 1:
            mid = lo + (hi - lo) // 2
            yield from sort(lo, mid)
            yield from sort(mid + 1, hi)
            yield from merge(lo, hi, 1)

    return list(sort(0, n - 1))


def _exchange(xs, i, j):
    hi = jnp.maximum(xs[i], xs[j])
    xs[j] = jnp.minimum(xs[i], xs[j])
    xs[i] = hi


def _bitonic_finish(xs):
    n = len(xs)
    d = n // 2
    while d >= 1:
        for i in range(n):
            if i & d == 0:
                _exchange(xs, i, i + d)
        d //= 2


def _merge_sublanes(xs):
    n = len(xs)
    for shift in (4, 2, 1):
        other = [pltpu.roll(x, shift, 0) for x in xs]
        xs = [jnp.maximum(xs[i], other[n - 1 - i]) for i in range(n)]
        _bitonic_finish(xs)
    return xs


def _top_sorted(s):
    xs = [s[8 * v:8 * v + 8] for v in range(s.shape[0] // 8)]
    for i, j in _sort_network(len(xs)):
        _exchange(xs, i, j)
    return _merge_sublanes(xs)


def _route_head_sorted(s1, s2):
    K = PEER_TOPK
    t = s1.shape[1]
    a = _top_sorted(s1)
    b = _top_sorted(s2)
    sub = lax.broadcasted_iota(jnp.int32, (8, t), 0)
    a8 = a[7]
    for r in range(6, -1, -1):
        a8 = jnp.where(sub == r, a[r], a8)
    main = [jnp.where((sub + 1) * (c + 1) <= K, a8 + b[c], NEG) for c in range(K)]
    top = _merge_sublanes(list(main))
    single = [a[8 + i] + b[0] for i in range(K // 2)] + [jnp.full((8, t), NEG, F32)] * (K // 2)
    top = [jnp.maximum(top[i], single[K - 1 - i]) for i in range(K)]
    _bitonic_finish(top)
    tau = top[K - 1]
    ea8 = jnp.exp(a8 - a[0])
    cnt8 = jnp.zeros((8, t), F32)
    z8 = jnp.zeros((8, t), F32)
    for c in range(K):
        hit = main[c] >= tau
        cnt8 = cnt8 + jnp.where(hit, 1.0, 0.0)
        z8 = z8 + jnp.where(hit, ea8 * jnp.exp(b[c] - b[0]), 0.0)
    cnt = [jnp.broadcast_to(cnt8[r:r + 1], (8, t)) for r in range(8)]
    zsum = jnp.sum(z8, axis=0, keepdims=True)
    total = jnp.sum(cnt8, axis=0, keepdims=True)
    for i in range(K // 2):
        hit = jnp.where(single[i] >= tau, 1.0, 0.0)
        cnt.append(hit)
        zsum = zsum + hit[0:1] * jnp.exp(a[8 + i][0:1] - a[0][0:1])
        total = total + hit[0:1]
    in1 = jnp.zeros((1, t), F32)
    in2 = jnp.zeros((1, t), F32)
    c1s, e1s, r2s, e2s = [], [], [], []
    scale = 0.5 / zsum
    for v in range(PEER_KEYS // 8):
        x1 = s1[8 * v:8 * v + 8]
        x2 = s2[8 * v:8 * v + 8]
        c1 = jnp.zeros((8, t), F32)
        r2 = jnp.full((8, t), float(K), F32)
        for r in range(K):
            c1 = jnp.where(x1 == a[r], cnt[r], c1)
            r2 = jnp.where(x2 == b[r], float(r), r2)
        top1 = x1 >= a[K - 1]
        top2 = x2 >= b[K - 1]
        in1 = in1 + jnp.sum(jnp.where(top1, 1.0, 0.0), axis=0, keepdims=True)
        in2 = in2 + jnp.sum(jnp.where(top2, 1.0, 0.0), axis=0, keepdims=True)
        c1s.append(c1)
        r2s.append(r2)
        e1s.append(jnp.where(top1, jnp.exp(x1 - a[0]), 0.0) * scale)
        e2s.append(jnp.where(top2, jnp.exp(x2 - b[0]), 0.0))
    bad = (in1 != float(K)) | (in2 != float(K)) | (total != float(K))
    for r in range(K - 1):
        bad = bad | (a[r][0:1] == a[r + 1][0:1]) | (b[r][0:1] == b[r + 1][0:1])
    cat = lambda xs: jnp.concatenate(xs, axis=0)
    return cat(r2s), cat(e2s), cat(c1s), cat(e1s), jnp.where(bad, 1.0, 0.0)


def _route_kernel(qp_ref, kb_ref, r2_ref, e2_ref, c1_ref, e1_ref):
    t = qp_ref.shape[0]
    for hh in range(PEER_HEADS):
        q = qp_ref[:, hh * PEER_DKEY:(hh + 1) * PEER_DKEY]
        st = _dot3(kb_ref[hh], q, _dot_nt)
        s1 = st[:PEER_KEYS]
        s2 = st[PEER_KEYS:]
        r2, e2, c1, e1, bad = _route_head_sorted(s1, s2)
        r2_ref[hh] = r2.astype(BF16)
        e2_ref[hh] = e2.astype(BF16)
        c1_ref[hh] = c1
        e1_ref[hh] = e1

        @pl.when(jnp.max(bad) > 0.0)
        def _():
            r2, e2, c1, e1 = _route_head_iterative(s1, s2, _cand_index(t))
            r2_ref[hh] = r2.astype(BF16)
            e2_ref[hh] = e2.astype(BF16)
            c1_ref[hh] = c1
            e1_ref[hh] = e1


def _route_call(qp, kb, tn):
    T = qp.shape[0]
    out = pl.BlockSpec((PEER_HEADS, PEER_KEYS, tn), lambda i: (0, 0, i))
    shp = jax.ShapeDtypeStruct((PEER_HEADS, PEER_KEYS, T), F32)
    shp16 = jax.ShapeDtypeStruct((PEER_HEADS, PEER_KEYS, T), BF16)
    return pl.pallas_call(
        _route_kernel,
        grid=(T // tn,),
        in_specs=[pl.BlockSpec((tn, PEER_HEADS * PEER_DKEY), lambda i: (i, 0)),
                  pl.BlockSpec(kb.shape, lambda i: (0, 0, 0))],
        out_specs=[out, out, out, out],
        out_shape=[shp16, shp16, shp, shp],
        compiler_params=_params(("parallel",)),
        name="route",
    )(qp, kb)


PEER_SUB = 1024
ROWS16 = 16


def _gelu_half(x):
    k = math.sqrt(2.0 / math.pi)
    return x + x * jnp.tanh(x * (k + (k * 0.044715) * (x * x)))


def _peer_kernel(h2_ref, u_ref, v_ref, r2_ref, e2_ref, c1_ref, e1_ref, x1_ref, g2_ref, fg_ref, o_ref, acc_ref,
                 w_ref, at_ref):
    e = pl.program_id(1)
    ne = pl.num_programs(1)
    te = u_ref.shape[0]

    @pl.when(e == 0)
    def _():
        acc_ref[...] = jnp.zeros_like(acc_ref)

    zero = jnp.zeros((), BF16)
    nsub = h2_ref.shape[1] // PEER_SUB
    toks = [slice(th * PEER_SUB, (th + 1) * PEER_SUB) for th in range(nsub)]

    def gates(th):
        tok = toks[th]
        for ii in range(te // PEER_KEYS):
            gs = [None] * (PEER_KEYS // ROWS16)
            for hh in range(PEER_HEADS):
                c1 = jnp.broadcast_to(c1_ref[hh, ii:ii + 1, tok], (ROWS16, PEER_SUB)).astype(BF16)
                e1 = jnp.broadcast_to(e1_ref[hh, ii:ii + 1, tok], (ROWS16, PEER_SUB)).astype(BF16)
                for k in range(PEER_KEYS // ROWS16):
                    rows = slice(k * ROWS16, (k + 1) * ROWS16)
                    term = jnp.where(r2_ref[hh, rows, tok] < c1, e2_ref[hh, rows, tok], zero) * e1
                    gs[k] = term if gs[k] is None else gs[k] + term
            for k in range(PEER_KEYS // ROWS16):
                lo = ii * PEER_KEYS + k * ROWS16
                w_ref[th, lo:lo + ROWS16, :] = gs[k]

    def project(th):
        at_ref[th] = _dot(u_ref[...], h2_ref[:, toks[th]])

    def activate(th):
        w_ref[th] = w_ref[th] * _gelu_half(at_ref[th].astype(BF16))

    def combine(th):
        acc_ref[:, toks[th]] += _dot_tn(v_ref[...], w_ref[th])

    gates(0)
    project(0)
    for th in range(nsub):
        if th + 1 < nsub:
            gates(th + 1)
        activate(th)
        if th + 1 < nsub:
            project(th + 1)
        combine(th)

    @pl.when(e == ne - 1)
    def _():
        x2 = x1_ref[...] + g2_ref[0] * acc_ref[...].T
        y = x2 * lax.rsqrt(jnp.mean(x2 * x2, axis=-1, keepdims=True) + NORM_EPS) * fg_ref[...]
        o_ref[...] = y


def _peer_call(h2, u_tab, v_tab, r2, e2, c1, e1, x1, g2rows, fg, tn, te, toks_per_batch):
    D, T = h2.shape
    E = u_tab.shape[0]
    rows = te // PEER_KEYS
    tab = pl.BlockSpec((PEER_HEADS, PEER_KEYS, tn), lambda i, e: (0, 0, i))
    sel = pl.BlockSpec((PEER_HEADS, rows, tn), lambda i, e: (0, e, i))
    return pl.pallas_call(
        _peer_kernel,
        grid=(T // tn, E // te),
        in_specs=[pl.BlockSpec((D, tn), lambda i, e: (0, i)),
                  pl.BlockSpec((te, D), lambda i, e: (e, 0)),
                  pl.BlockSpec((te, D), lambda i, e: (e, 0)),
                  tab, tab, sel, sel,
                  pl.BlockSpec((tn, D), lambda i, e: (i, 0)),
                  pl.BlockSpec((1, 1, D), lambda i, e: ((i * tn) // toks_per_batch, 0, 0)),
                  pl.BlockSpec((1, D), lambda i, e: (0, 0))],
        out_specs=pl.BlockSpec((tn, D), lambda i, e: (i, 0)),
        out_shape=jax.ShapeDtypeStruct((T, D), F32),
        scratch_shapes=[pltpu.VMEM((D, tn), F32), pltpu.VMEM((tn // PEER_SUB, te, PEER_SUB), BF16),
                        pltpu.VMEM((tn // PEER_SUB, te, PEER_SUB), F32)],
        compiler_params=_params(("parallel", "arbitrary")),
        name="peer",
    )(h2, u_tab, v_tab, r2, e2, c1, e1, x1, g2rows, fg)


def _rope_tables(L):
    rows = L // GRID_W
    row = jnp.repeat(jnp.arange(rows, dtype=jnp.int32), GRID_W).astype(F32)
    col = jnp.tile(jnp.arange(GRID_W, dtype=jnp.int32), rows).astype(F32)
    f = HEAD_DIM // 4
    inv = ROPE_BASE ** (-jnp.arange(f, dtype=F32) / f)
    ang_r = row[:, None] * inv[None, :]
    ang_c = col[:, None] * inv[None, :]
    ang = jnp.concatenate([ang_r, ang_r, ang_c, ang_c], axis=-1)
    reps = ROT_W // HEAD_DIM
    return jnp.tile(jnp.cos(ang), (1, reps)), jnp.tile(jnp.sin(ang), (1, reps))


def _rot_cols(w):
    n = w.shape[1]
    f = HEAD_DIM // 4
    d = jnp.arange(n) % (2 * f)
    src = jnp.where(d < f, jnp.arange(n) + f, jnp.arange(n) - f)
    sign = jnp.where(d < f, -1.0, 1.0).astype(w.dtype)
    return w[:, src] * sign[None, :]


def _dup_cols(w):
    return jnp.concatenate([w[:, :HEAD_DIM], w[:, :HEAD_DIM], w[:, HEAD_DIM:], w[:, HEAD_DIM:]], axis=1)


def _dft_table(L):
    N = 2 * L
    s = 64
    kh = jnp.arange(L // s, dtype=jnp.int32)
    kl = jnp.arange(s, dtype=jnp.int32)
    n = jnp.arange(L, dtype=jnp.int32)
    ph1 = ((kh[:, None] * s * n[None, :]) % N).astype(F32) * (2.0 * math.pi / N)
    ph2 = ((kl[:, None] * n[None, :]) % N).astype(F32) * (2.0 * math.pi / N)
    c1, s1 = jnp.cos(ph1)[:, None, :], jnp.sin(ph1)[:, None, :]
    c2, s2 = jnp.cos(ph2)[None, :, :], jnp.sin(ph2)[None, :, :]
    cosm = (c1 * c2 - s1 * s2).reshape(L, L)
    sinm = (s1 * c2 + c1 * s2).reshape(L, L)
    nyq = jnp.where(n % 2 == 0, 1.0, -1.0).astype(F32)
    k = jnp.arange(L, dtype=jnp.int32)
    q = jnp.where(k[:, None] == 0, nyq[None, :], -sinm)
    return cosm.astype(BF16), q.astype(BF16)


def _filter_features(L):
    t = jnp.arange(L, dtype=F32) / L
    bands = jnp.arange(1, HY_EMB_BANDS + 1, dtype=F32)
    ang = 2.0 * math.pi * t[:, None] * bands[None, :]
    z = jnp.concatenate([t[:, None], jnp.cos(ang), jnp.sin(ang)], axis=-1)
    return jnp.pad(z, ((0, 0), (0, LANES - z.shape[1])))


def _tile(n, pref):
    return pref if n % pref == 0 else n


def kernel(x, c, ctx, c_ctx, w_mod, b_mod, norm1_g, w_in, attn_sink, hy_conv_w, hy_conv_b, hy_fw1, hy_fb1, hy_fw2, hy_fb2, hy_fw3, hy_fb3, hy_freq, hy_skip, w_o_attn, w_o_hy, w_out, norm2_g, peer_wq, peer_keys, peer_u, peer_v, final_g):
    B, L, D = x.shape
    assert B == 4 and D == D_MODEL and w_mod.shape[0] == 1
    T = B * L
    li = 0

    c8 = jnp.concatenate([c, c_ctx[None, :], jnp.zeros((3, D), F32)], axis=0)
    mod3 = _mod_call(c8, w_mod[li], b_mod[li]).reshape(8, 6, D)
    g1 = norm1_g[li].reshape(1, D)
    g2 = norm2_g[li].reshape(1, D)

    w = w_in[li]
    wq, wk, wv = w[:, :OFF_K], _dup_cols(w[:, OFF_K:OFF_V]), _dup_cols(w[:, OFF_V:OFF_HY])
    w_cat = jnp.concatenate([wq, wk, wv, w[:, OFF_HY:OFF_G], _rot_cols(wq), _rot_cols(wk)], axis=1).astype(BF16)
    cos_t, sin_t = _rope_tables(L)
    q, k, v, hy = _inproj_call(x, mod3, g1, w_cat, cos_t, sin_t, _tile(L, 512))
    kx, vx = _ctxproj_call(ctx, mod3, g1, jnp.concatenate([wk, wv], axis=1).astype(BF16))

    gsz = N_HEADS // N_KV_HEADS
    sink_b = jnp.broadcast_to(
        jnp.repeat(attn_sink[li].astype(F32).reshape(N_KV_HEADS, gsz), BLOCK, axis=1)[:, :, None],
        (N_KV_HEADS, gsz * BLOCK, LANES))
    y_attn = _attn_call(q, k, v, kx, vx, sink_b)

    H = HY_FILTER_HIDDEN
    w1p = jnp.pad(hy_fw1[li], ((0, LANES - hy_fw1.shape[1]), (0, 0)))
    deltas = jnp.abs(jnp.linspace(math.log(HY_DECAY_TARGET) / HY_SLOW_DECAY,
                                  math.log(HY_DECAY_TARGET) / HY_FAST_DECAY, HY_WIDTH, dtype=F32)).reshape(1, -1)
    hs, hd = _filt_call(_filter_features(L), w1p, hy_fb1[li].reshape(1, H), hy_freq[li].reshape(1, H),
                        hy_fw2[li], hy_fb2[li].reshape(1, H), hy_fw3[li], hy_fb3[li].reshape(1, -1), deltas)
    u, x0c = _sconv_call(hy, hy_conv_w[li], hy_conv_b[li].reshape(1, -1))
    yc = _lconv_call(u, hs, hd, _dft_table(L), hy_skip[li].reshape(1, -1), _tile(L, 256))

    x1, h2t, qp = _merge_call(x, mod3, g1, g2, y_attn, yc, x0c, w[:, OFF_G:].astype(BF16),
                             w_o_attn[li].astype(BF16), w_o_hy[li].astype(BF16), w_out[li].astype(BF16),
                             peer_wq[li].astype(BF16), _tile(L, 256))

    keys = peer_keys[li]
    zk = jnp.zeros_like(keys[:, 0])
    kb = jnp.concatenate([jnp.concatenate([keys[:, 0], zk], axis=2),
                          jnp.concatenate([zk, keys[:, 1]], axis=2)], axis=1)
    r2, e2, c1, e1 = _route_call(qp.reshape(T, D), kb, _tile(T, 256))

    out = _peer_call(h2t, peer_u[li].astype(BF16), peer_v[li].astype(BF16), r2, e2, c1, e1,
                     x1.reshape(T, D), mod3[:B, 5:6, :], final_g.reshape(1, D),
                     _tile(T, 1024), 1024, L)
    return out.reshape(B, L, D)
```

```python
import functools
import math

import jax
import jax.numpy as jnp
from jax import lax
from jax.experimental import pallas as pl
from jax.experimental.pallas import tpu as pltpu

F32 = jnp.float32
BF16 = jnp.bfloat16

D_MODEL = 1024
GRID_W = 64
NORM_EPS = 1e-6
N_HEADS = 8
N_KV_HEADS = 2
HEAD_DIM = 64
BLOCK = 128
ROPE_BASE = 10000.0
HY_WIDTH = 512
HY_EMB_BANDS = 16
HY_FILTER_HIDDEN = 64
HY_FAST_DECAY = 0.3
HY_SLOW_DECAY = 1.5
HY_DECAY_TARGET = 1e-2
PEER_HEADS = 8
PEER_KEYS = 128
PEER_TOPK = 16
PEER_DKEY = 128
Q_W = N_HEADS * HEAD_DIM
KV_W = N_KV_HEADS * HEAD_DIM
HY_IN = 3 * HY_WIDTH
OFF_K = Q_W
OFF_V = OFF_K + KV_W
OFF_HY = OFF_V + KV_W
OFF_G = OFF_HY + HY_IN

LANES = 128
VMEM_LIMIT = 56 * 1024 * 1024
NEG = -1e30
KV_DUP = 2 * KV_W


def _params(sem):
    return pltpu.CompilerParams(dimension_semantics=sem, vmem_limit_bytes=VMEM_LIMIT)


def _dot(a, b):
    return lax.dot_general(a, b, (((1,), (0,)), ((), ())), preferred_element_type=F32)


def _dot_nt(a, b):
    return lax.dot_general(a, b, (((1,), (1,)), ((), ())), preferred_element_type=F32)


def _dot_tn(a, b):
    return lax.dot_general(a, b, (((0,), (0,)), ((), ())), preferred_element_type=F32)


def _split(a):
    hi = a.astype(BF16)
    lo = (a - hi.astype(F32)).astype(BF16)
    return hi, lo


def _dot3(a, b, dot=_dot):
    ah, al = _split(a)
    bh, bl = _split(b)
    return dot(ah, bh) + dot(ah, bl) + dot(al, bh)


def _norm_mod(x, g, shift, scale):
    y = x * lax.rsqrt(jnp.mean(x * x, axis=-1, keepdims=True) + NORM_EPS) * g
    return y * (1.0 + scale) + shift


def _mod_kernel(c_ref, w_ref, b_ref, o_ref):
    c = c_ref[...]
    s = c * jax.nn.sigmoid(c)
    o_ref[...] = _dot3(s, w_ref[...]) + b_ref[...]


def _mod_call(c8, w_mod, b_mod):
    n = w_mod.shape[1] // D_MODEL
    return pl.pallas_call(
        _mod_kernel,
        grid=(n,),
        in_specs=[pl.BlockSpec((8, D_MODEL), lambda j: (0, 0)),
                  pl.BlockSpec((D_MODEL, D_MODEL), lambda j: (0, j)),
                  pl.BlockSpec((1, D_MODEL), lambda j: (0, j))],
        out_specs=pl.BlockSpec((8, D_MODEL), lambda j: (0, j)),
        out_shape=jax.ShapeDtypeStruct((8, w_mod.shape[1]), F32),
        compiler_params=_params(("arbitrary",)),
        name="mod",
    )(c8, w_mod, b_mod.reshape(1, -1))


ROT_W = Q_W + KV_DUP
CAT_W = Q_W + 2 * KV_DUP + HY_IN + ROT_W


def _inproj_kernel(x_ref, mod_ref, g_ref, w_ref, cos_ref, sin_ref, q_ref, k_ref, v_ref, hy_ref):
    m = mod_ref[0]
    h = _norm_mod(x_ref[0], g_ref[...], m[0:1], m[1:2]).astype(BF16)
    o_v = ROT_W
    o_hy = o_v + KV_DUP
    o_rot = o_hy + HY_IN
    p = _dot(h, w_ref[:, 0:ROT_W])
    pr = _dot(h, w_ref[:, o_rot:o_rot + ROT_W])
    qk = p * cos_ref[...] + pr * sin_ref[...]
    q_ref[0] = qk[:, :Q_W].astype(BF16)
    k_ref[0] = qk[:, Q_W:].astype(BF16)
    v_ref[0] = _dot(h, w_ref[:, o_v:o_hy]).astype(BF16)
    for j in range(HY_IN // HY_WIDTH):
        lo = o_hy + j * HY_WIDTH
        hy_ref[0, :, j * HY_WIDTH:(j + 1) * HY_WIDTH] = _dot(h, w_ref[:, lo:lo + HY_WIDTH]).astype(BF16)


def _inproj_call(x, mod3, g, w_cat, cos_t, sin_t, tm):
    B, L, D = x.shape
    return pl.pallas_call(
        _inproj_kernel,
        grid=(B, L // tm),
        in_specs=[pl.BlockSpec((1, tm, D), lambda b, i: (b, i, 0)),
                  pl.BlockSpec((1, 6, D), lambda b, i: (b, 0, 0)),
                  pl.BlockSpec((1, D), lambda b, i: (0, 0)),
                  pl.BlockSpec((D, CAT_W), lambda b, i: (0, 0)),
                  pl.BlockSpec((tm, ROT_W), lambda b, i: (i, 0)),
                  pl.BlockSpec((tm, ROT_W), lambda b, i: (i, 0))],
        out_specs=[pl.BlockSpec((1, tm, Q_W), lambda b, i: (b, i, 0)),
                   pl.BlockSpec((1, tm, KV_DUP), lambda b, i: (b, i, 0)),
                   pl.BlockSpec((1, tm, KV_DUP), lambda b, i: (b, i, 0)),
                   pl.BlockSpec((1, tm, HY_IN), lambda b, i: (b, i, 0))],
        out_shape=[jax.ShapeDtypeStruct((B, L, Q_W), BF16),
                   jax.ShapeDtypeStruct((B, L, KV_DUP), BF16),
                   jax.ShapeDtypeStruct((B, L, KV_DUP), BF16),
                   jax.ShapeDtypeStruct((B, L, HY_IN), BF16)],
        compiler_params=_params(("parallel", "arbitrary")),
        name="inproj",
    )(x, mod3, g, w_cat, cos_t, sin_t)


def _ctxproj_kernel(x_ref, mod_ref, g_ref, w_ref, k_ref, v_ref):
    m = mod_ref[0]
    h = _norm_mod(x_ref[0], g_ref[...], m[0:1], m[1:2]).astype(BF16)
    k_ref[0] = _dot(h, w_ref[:, :KV_DUP]).astype(BF16)
    v_ref[0] = _dot(h, w_ref[:, KV_DUP:]).astype(BF16)


def _ctxproj_call(ctx, mod3, g, w_kv):
    B, C, D = ctx.shape
    return pl.pallas_call(
        _ctxproj_kernel,
        grid=(B,),
        in_specs=[pl.BlockSpec((1, C, D), lambda b: (b, 0, 0)),
                  pl.BlockSpec((1, 6, D), lambda b: (4, 0, 0)),
                  pl.BlockSpec((1, D), lambda b: (0, 0)),
                  pl.BlockSpec((D, 2 * KV_DUP), lambda b: (0, 0))],
        out_specs=[pl.BlockSpec((1, C, KV_DUP), lambda b: (b, 0, 0)),
                   pl.BlockSpec((1, C, KV_DUP), lambda b: (b, 0, 0))],
        out_shape=[jax.ShapeDtypeStruct((B, C, KV_DUP), BF16),
                   jax.ShapeDtypeStruct((B, C, KV_DUP), BF16)],
        compiler_params=_params(("arbitrary",)),
        name="ctxproj",
    )(ctx, mod3, g, w_kv)


def _attn_kernel(q_ref, kp_ref, kc_ref, kn_ref, vp_ref, vc_ref, vn_ref, kx_ref, vx_ref, sink_ref, o_ref):
    n = pl.program_id(1)
    nb = pl.num_programs(1)
    q = q_ref[0]
    rows = (N_HEADS // N_KV_HEADS) * BLOCK
    lo = lax.broadcasted_iota(jnp.int32, (BLOCK, LANES), 1) < HEAD_DIM
    r = lax.broadcasted_iota(jnp.int32, (rows, BLOCK), 0) % BLOCK
    c = lax.broadcasted_iota(jnp.int32, (rows, BLOCK), 1)
    ok_prev = jnp.logical_and(c >= r, n > 0)
    ok_next = jnp.logical_and(c <= r, n < nb - 1)
    scale = HEAD_DIM ** -0.5
    zero = jnp.zeros((BLOCK, LANES), BF16)
    for g in range(N_KV_HEADS):
        sl = slice(g * LANES, (g + 1) * LANES)
        qa = q[:, 2 * g * LANES:(2 * g + 1) * LANES]
        qb = q[:, (2 * g + 1) * LANES:(2 * g + 2) * LANES]
        lhs = jnp.concatenate([jnp.where(lo, qa, zero), jnp.where(lo, zero, qa),
                               jnp.where(lo, qb, zero), jnp.where(lo, zero, qb)], axis=0)
        s_p = jnp.where(ok_prev, _dot_nt(lhs, kp_ref[0, :, sl]) * scale, NEG)
        s_c = _dot_nt(lhs, kc_ref[0, :, sl]) * scale
        s_n = jnp.where(ok_next, _dot_nt(lhs, kn_ref[0, :, sl]) * scale, NEG)
        s_x = _dot_nt(lhs, kx_ref[0, :, sl]) * scale
        sink = sink_ref[g][:, 0:1]
        half = s_x.shape[1] // 2
        m = jnp.max(jnp.maximum(jnp.maximum(s_p, s_c), jnp.maximum(s_n, jnp.maximum(s_x[:, :half], s_x[:, half:]))),
                    axis=-1, keepdims=True)
        m = jnp.maximum(m, sink)
        p_p = jnp.exp(s_p - m)
        p_c = jnp.exp(s_c - m)
        p_n = jnp.exp(s_n - m)
        p_x = jnp.exp(s_x - m)
        den = (jnp.sum((p_p + p_c) + (p_n + (p_x[:, :half] + p_x[:, half:])), axis=-1, keepdims=True)
               + jnp.exp(sink - m))
        o = (_dot(p_p.astype(BF16), vp_ref[0, :, sl]) + _dot(p_c.astype(BF16), vc_ref[0, :, sl])
             + _dot(p_n.astype(BF16), vn_ref[0, :, sl]) + _dot(p_x.astype(BF16), vx_ref[0, :, sl]))
        o = o / den
        o_ref[0, :, 2 * g * LANES:(2 * g + 1) * LANES] = jnp.where(
            lo, o[0:BLOCK], o[BLOCK:2 * BLOCK]).astype(BF16)
        o_ref[0, :, (2 * g + 1) * LANES:(2 * g + 2) * LANES] = jnp.where(
            lo, o[2 * BLOCK:3 * BLOCK], o[3 * BLOCK:4 * BLOCK]).astype(BF16)


def _attn_call(q, k, v, kx, vx, sink_b):
    B, L, _ = q.shape
    C = kx.shape[1]
    nb = L // BLOCK
    kv = lambda f: pl.BlockSpec((1, BLOCK, KV_DUP), f)
    prev = lambda b, n: (b, jnp.maximum(n - 1, 0), 0)
    cur = lambda b, n: (b, n, 0)
    nxt = lambda b, n: (b, jnp.minimum(n + 1, nb - 1), 0)
    rows = (N_HEADS // N_KV_HEADS) * BLOCK
    return pl.pallas_call(
        _attn_kernel,
        grid=(B, nb),
        in_specs=[pl.BlockSpec((1, BLOCK, Q_W), cur),
                  kv(prev), kv(cur), kv(nxt), kv(prev), kv(cur), kv(nxt),
                  pl.BlockSpec((1, C, KV_DUP), lambda b, n: (b, 0, 0)),
                  pl.BlockSpec((1, C, KV_DUP), lambda b, n: (b, 0, 0)),
                  pl.BlockSpec((N_KV_HEADS, rows, LANES), lambda b, n: (0, 0, 0))],
        out_specs=pl.BlockSpec((1, BLOCK, Q_W), cur),
        out_shape=jax.ShapeDtypeStruct((B, L, Q_W), BF16),
        compiler_params=_params(("parallel", "arbitrary")),
        name="attn",
    )(q, k, k, k, v, v, v, kx, vx, sink_b)


def _filt_kernel(z_ref, w1_ref, b1_ref, fr_ref, w2_ref, b2_ref, w3f_ref, w3b_ref, b3f_ref, b3b_ref,
                 dl_ref, hs_ref, hd_ref, h_ref, par_ref):
    L = z_ref.shape[0]
    z = z_ref[...]

    @pl.when(pl.program_id(0) == 0)
    def _():
        fr = fr_ref[...]
        h1 = jnp.sin(fr * (_dot3(z, w1_ref[...]) + b1_ref[...]))
        h_ref[...] = jnp.sin(fr * (_dot3(h1, w2_ref[...]) + b2_ref[...]))

    h = h_ref[...]
    decay = jnp.exp(-z[:, 0:1] * dl_ref[...])
    hf = (_dot3(h, w3f_ref[...]) + b3f_ref[...]) * decay
    hb = (_dot3(h, w3b_ref[...]) + b3b_ref[...]) * decay
    row = lax.broadcasted_iota(jnp.int32, hb.shape, 0)
    hb = jnp.where(row < L - 1, hb, 0.0)
    norm = jnp.sum(jnp.abs(hf), axis=0, keepdims=True) + jnp.sum(jnp.abs(hb), axis=0, keepdims=True)
    inv = 1.0 / norm
    hf = hf * inv
    hbs = jnp.where(row >= 1, pltpu.roll(hb, 1, 0), 0.0) * inv
    H = L // 2
    for out_ref, val in ((hs_ref, hf + hbs), (hd_ref, hf - hbs)):
        par_ref[...] = val
        out_ref[0] = par_ref[pl.ds(0, H, stride=2), :].astype(BF16)
        out_ref[1] = par_ref[pl.ds(1, H, stride=2), :].astype(BF16)


def _filt_call(zf, w1p, b1, fr, w2, b2, w3, b3, absdelta):
    L = zf.shape[0]
    nct = HY_WIDTH // LANES
    H = HY_FILTER_HIDDEN
    full = lambda shape: pl.BlockSpec(shape, lambda j: (0, 0))
    return pl.pallas_call(
        _filt_kernel,
        grid=(nct,),
        in_specs=[full((L, LANES)), full((LANES, H)), full((1, H)), full((1, H)), full((H, H)), full((1, H)),
                  pl.BlockSpec((H, LANES), lambda j: (0, j)),
                  pl.BlockSpec((H, LANES), lambda j: (0, nct + j)),
                  pl.BlockSpec((1, LANES), lambda j: (0, j)),
                  pl.BlockSpec((1, LANES), lambda j: (0, nct + j)),
                  pl.BlockSpec((1, LANES), lambda j: (0, j))],
        out_specs=[pl.BlockSpec((2, L // 2, LANES), lambda j: (0, 0, j)),
                   pl.BlockSpec((2, L // 2, LANES), lambda j: (0, 0, j))],
        out_shape=[jax.ShapeDtypeStruct((2, L // 2, HY_WIDTH), BF16),
                   jax.ShapeDtypeStruct((2, L // 2, HY_WIDTH), BF16)],
        scratch_shapes=[pltpu.VMEM((L, H), F32), pltpu.VMEM((L, LANES), F32)],
        compiler_params=_params(("arbitrary",)),
        name="filt",
    )(zf, w1p, b1, fr, w2, b2, w3, w3, b3, b3, absdelta)


def _sconv_kernel(x0_ref, x1_ref, v_ref, w0_ref, w1_ref, w2_ref, b0_ref, b1_ref, b2_ref, u_ref, g_ref, par_ref):
    L = x0_ref.shape[1]
    row = lax.broadcasted_iota(jnp.int32, (L, LANES), 0)

    def conv(z_ref, w_ref, b_ref):
        z = z_ref[0].astype(F32)
        w = w_ref[...]
        zp = jnp.where(row >= 1, pltpu.roll(z, 1, 0), 0.0)
        zn = jnp.where(row < L - 1, pltpu.roll(z, L - 1, 0), 0.0)
        return zp * w[0:1] + z * w[1:2] + zn * w[2:3] + b_ref[...]

    g_ref[0] = conv(x0_ref, w0_ref, b0_ref).astype(BF16)
    par_ref[...] = conv(v_ref, w2_ref, b2_ref) * conv(x1_ref, w1_ref, b1_ref)
    u_ref[0, 0] = par_ref[pl.ds(0, L // 2, stride=2), :].astype(BF16)
    u_ref[0, 1] = par_ref[pl.ds(1, L // 2, stride=2), :].astype(BF16)


def _sconv_call(hy, conv_w, conv_b):
    B, L, _ = hy.shape
    nct = HY_WIDTH // LANES
    zs = lambda part: pl.BlockSpec((1, L, LANES), lambda b, j: (b, 0, part * nct + j))
    ws = lambda part: pl.BlockSpec((3, LANES), lambda b, j: (0, part * nct + j))
    bs = lambda part: pl.BlockSpec((1, LANES), lambda b, j: (0, part * nct + j))
    out = pl.BlockSpec((1, L, LANES), lambda b, j: (b, 0, j))
    return pl.pallas_call(
        _sconv_kernel,
        grid=(B, nct),
        in_specs=[zs(0), zs(1), zs(2), ws(0), ws(1), ws(2), bs(0), bs(1), bs(2)],
        out_specs=[pl.BlockSpec((1, 2, L // 2, LANES), lambda b, j: (b, 0, 0, j)), out],
        out_shape=[jax.ShapeDtypeStruct((B, 2, L // 2, HY_WIDTH), BF16),
                   jax.ShapeDtypeStruct((B, L, HY_WIDTH), BF16)],
        scratch_shapes=[pltpu.VMEM((L, LANES), F32)],
        compiler_params=_params(("parallel", "arbitrary")),
        name="sconv",
    )(hy, hy, hy, conv_w, conv_w, conv_w, conv_b, conv_b, conv_b)


def _lconv_kernel(u_ref, hs_ref, hd_ref, fp_ref, fq_ref, cw_ref, sw_ref, skip_ref, y_ref, rhs_ref, acce_ref, acco_ref,
                  par_ref):
    nbat = u_ref.shape[0]
    H = u_ref.shape[2]
    j = pl.program_id(1)
    nj = pl.num_programs(1)
    tk = fp_ref.shape[0]
    W = nbat * LANES
    blk = lambda i: slice(i * LANES, (i + 1) * LANES)
    HS_E, HS_O, HD_E, HD_O = 2 * nbat, 2 * nbat + 1, 2 * nbat + 2, 2 * nbat + 3

    @pl.when(j == 0)
    def _():
        for b in range(nbat):
            rhs_ref[:, blk(b)] = u_ref[b, 0]
            rhs_ref[:, blk(nbat + b)] = u_ref[b, 1]
        rhs_ref[:, blk(HS_E)] = hs_ref[0]
        rhs_ref[:, blk(HS_O)] = hs_ref[1]
        rhs_ref[:, blk(HD_E)] = hd_ref[0]
        rhs_ref[:, blk(HD_O)] = hd_ref[1]
        acce_ref[...] = jnp.zeros_like(acce_ref)
        acco_ref[...] = jnp.zeros_like(acco_ref)

    fp = fp_ref[...]
    fq = fq_ref[...]
    ap = _dot(fp, rhs_ref[...])
    aq = _dot(fq, rhs_ref[...])
    first = jnp.logical_and(lax.broadcasted_iota(jnp.int32, (tk, LANES), 0) == 0, j == 0)
    cw = cw_ref[...]
    sw = sw_ref[...]
    csum = lambda i: ap[:, blk(i)]
    ssum = lambda i: jnp.where(first, 0.0, -aq[:, blk(i)])
    alt = lambda i: aq[:, blk(i)]

    gs = cw * csum(HS_O) - sw * ssum(HS_O)
    hdd = sw * csum(HD_O) + cw * ssum(HD_O)
    kr, ki = csum(HS_E) + gs, -(ssum(HD_E) + hdd)
    kr2, ki2 = csum(HS_E) - gs, ssum(HD_E) - hdd
    wgt = jnp.where(first, 0.25 / H, 0.5 / H)
    outs = [[], [], [], []]
    for b in range(nbat):
        a, bb, c, d = csum(b), ssum(b), csum(nbat + b), ssum(nbat + b)
        g = cw * c - sw * d
        h = sw * c + cw * d
        xr, xi = a + g, -(bb + h)
        xr2, xi2 = a - g, bb - h
        ar, ai = xr * kr - xi * ki, xr * ki + xi * kr
        br, bi = xr2 * kr2 - xi2 * ki2, xr2 * ki2 + xi2 * kr2
        yer, yei = ar + br, ai - bi
        dr, di = ar - br, ai + bi
        yor, yoi = dr * cw - di * sw, dr * sw + di * cw
        yei = jnp.where(first, 2.0 * (alt(b) * alt(HS_E) - alt(nbat + b) * alt(HD_O)), yei)
        yoi = jnp.where(first, 2.0 * (alt(b) * alt(HD_O) + alt(nbat + b) * alt(HS_E)), yoi)
        for lst, val in zip(outs, (yer, yei, yor, yoi)):
            lst.append((val * wgt).astype(BF16))
    yer, yei, yor, yoi = [jnp.concatenate(x, axis=1) for x in outs]
    acce_ref[...] += _dot_tn(fp, yer) + _dot_tn(fq, yei)
    acco_ref[...] += _dot_tn(fp, yor) + _dot_tn(fq, yoi)

    @pl.when(j == nj - 1)
    def _():
        for b in range(nbat):
            par_ref[pl.ds(0, H, stride=2), :] = acce_ref[:, blk(b)] + u_ref[b, 0].astype(F32) * skip_ref[...]
            par_ref[pl.ds(1, H, stride=2), :] = acco_ref[:, blk(b)] + u_ref[b, 1].astype(F32) * skip_ref[...]
            y_ref[b] = par_ref[...].astype(BF16)


def _lconv_call(u, hs, hd, ftab, cw, sw, skip, tk):
    B, _, H, _ = u.shape
    nct = HY_WIDTH // LANES
    return pl.pallas_call(
        _lconv_kernel,
        grid=(nct, H // tk),
        in_specs=[pl.BlockSpec((B, 2, H, LANES), lambda c, j: (0, 0, 0, c)),
                  pl.BlockSpec((2, H, LANES), lambda c, j: (0, 0, c)),
                  pl.BlockSpec((2, H, LANES), lambda c, j: (0, 0, c)),
                  pl.BlockSpec((tk, H), lambda c, j: (j, 0)),
                  pl.BlockSpec((tk, H), lambda c, j: (j, 0)),
                  pl.BlockSpec((tk, LANES), lambda c, j: (j, 0)),
                  pl.BlockSpec((tk, LANES), lambda c, j: (j, 0)),
                  pl.BlockSpec((1, LANES), lambda c, j: (0, c))],
        out_specs=pl.BlockSpec((B, 2 * H, LANES), lambda c, j: (0, 0, c)),
        out_shape=jax.ShapeDtypeStruct((B, 2 * H, HY_WIDTH), BF16),
        scratch_shapes=[pltpu.VMEM((H, (2 * B + 4) * LANES), BF16),
                        pltpu.VMEM((H, B * LANES), F32),
                        pltpu.VMEM((H, B * LANES), F32),
                        pltpu.VMEM((2 * H, LANES), F32)],
        compiler_params=_params(("parallel", "arbitrary")),
        name="lconv",
    )(u, hs, hd, ftab[0], ftab[1], cw, sw, skip)


def _merge_kernel(x_ref, mod_ref, g1_ref, g2_ref, ya_ref, yc_ref, x0_ref, wg_ref, woa_ref, woh_ref, wout_ref,
                  wq_ref, x1_ref, h2t_ref, qp_ref):
    m = mod_ref[0]
    x = x_ref[0]
    h = _norm_mod(x, g1_ref[...], m[0:1], m[1:2]).astype(BF16)
    D = x.shape[-1]
    gate_a = jax.nn.sigmoid(_dot(h, wg_ref[:, :D]))
    gate_h = jax.nn.sigmoid(_dot(h, wg_ref[:, D:]))
    y_hy = yc_ref[0] * x0_ref[0]
    merged = gate_a * _dot(ya_ref[0], woa_ref[...]) + gate_h * _dot(y_hy, woh_ref[...])
    x1 = x + m[2:3] * _dot(merged.astype(BF16), wout_ref[...])
    x1_ref[0] = x1
    h2 = _norm_mod(x1, g2_ref[...], m[3:4], m[4:5])
    h2t_ref[...] = h2.T.astype(BF16)
    qp_ref[0] = _dot(h2.astype(BF16), wq_ref[...])


def _merge_call(x, mod3, g1, g2, ya, yc, x0c, wg, woa, woh, wout, wq, tm):
    B, L, D = x.shape
    tok = lambda w: pl.BlockSpec((1, tm, w), lambda b, i: (b, i, 0))
    full = lambda a: pl.BlockSpec(a.shape, lambda b, i: (0, 0))
    return pl.pallas_call(
        _merge_kernel,
        grid=(B, L // tm),
        in_specs=[tok(D), pl.BlockSpec((1, 6, D), lambda b, i: (b, 0, 0)), full(g1), full(g2),
                  tok(Q_W), tok(HY_WIDTH), tok(HY_WIDTH), full(wg), full(woa), full(woh), full(wout), full(wq)],
        out_specs=[tok(D), pl.BlockSpec((D, tm), lambda b, i: (0, b * (L // tm) + i)), tok(D)],
        out_shape=[jax.ShapeDtypeStruct((B, L, D), F32),
                   jax.ShapeDtypeStruct((D, B * L), BF16),
                   jax.ShapeDtypeStruct((B, L, D), F32)],
        compiler_params=_params(("parallel", "arbitrary")),
        name="merge",
    )(x, mod3, g1, g2, ya, yc, x0c, wg, woa, woh, wout, wq)


def _topk_rank(s):
    n, t = s.shape
    rowi = lax.broadcasted_iota(jnp.int32, (n, t), 0).astype(F32)
    topi = lax.broadcasted_iota(jnp.int32, (PEER_TOPK, t), 0)

    def body(it, carry):
        s, rank, vals = carry
        mx = jnp.max(s, axis=0, keepdims=True)
        first = jnp.min(jnp.where(s == mx, rowi, float(n)), axis=0, keepdims=True)
        sel = rowi == first
        return (jnp.where(sel, NEG, s), jnp.where(sel, jnp.asarray(it, F32), rank), jnp.where(topi == it, mx, vals))

    init = (s, jnp.full((n, t), float(PEER_TOPK), F32), jnp.zeros((PEER_TOPK, t), F32))
    _, rank, vals = lax.fori_loop(0, PEER_TOPK, body, init)
    return rank, vals


def _cand_blocks():
    blocks = []
    for c in range(PEER_TOPK // 2):
        valid = PEER_TOPK // (c + 1)
        blocks.append((c, -(-valid // 8) * 8, valid))
    return blocks


def _cand_index(t):
    K = PEER_TOPK
    fis = []
    for c, rows, valid in _cand_blocks():
        ri = lax.broadcasted_iota(jnp.int32, (rows, t), 0)
        fis.append(jnp.where(ri < valid, ri * K + c, K * K))
    ci = lax.broadcasted_iota(jnp.int32, (8, t), 0)
    fis.append(ci + K // 2)
    return jnp.concatenate(fis, axis=0).astype(F32)


def _route_head_iterative(s1, s2, fi):
    K = PEER_TOPK
    t = s1.shape[1]
    blocks = _cand_blocks()
    ncand = fi.shape[0]
    rank1, a = _topk_rank(s1)
    rank2, b = _topk_rank(s2)
    cands = []
    for c, rows, valid in blocks:
        cands.append(a[0:rows] + b[c:c + 1])
    cands.append(a[0:1] + b[K // 2:K])
    cand = jnp.where(fi < K * K, jnp.concatenate(cands, axis=0), NEG)

    def body(it, carry):
        cand, sel_all = carry
        mx = jnp.max(cand, axis=0, keepdims=True)
        first = jnp.min(jnp.where(cand == mx, fi, float(K * K)), axis=0, keepdims=True)
        sel = fi == first
        return jnp.where(sel, NEG, cand), jnp.where(sel, 1.0, sel_all)

    _, sel = lax.fori_loop(0, K, body, (cand, jnp.zeros((ncand, t), F32)))
    ea = jnp.exp(a - a[0:1])
    eb = jnp.exp(b - b[0:1])
    cnt = jnp.zeros((K, t), F32)
    zsum = jnp.zeros((1, t), F32)
    off = 0
    for c, rows, valid in blocks:
        blk = sel[off:off + rows]
        off += rows
        if rows < K:
            blk_full = jnp.concatenate([blk, jnp.zeros((K - rows, t), F32)], axis=0)
        else:
            blk_full = blk
        cnt = cnt + blk_full
        zsum = zsum + jnp.sum(blk * ea[0:rows], axis=0, keepdims=True) * eb[c:c + 1]
    tail = sel[off:off + 8]
    tcount = jnp.sum(tail, axis=0, keepdims=True)
    row0 = lax.broadcasted_iota(jnp.int32, (K, t), 0) == 0
    cnt = cnt + jnp.where(row0, tcount, 0.0)
    zsum = zsum + jnp.sum(tail * eb[K // 2:K], axis=0, keepdims=True)
    c1 = jnp.zeros((PEER_KEYS, t), F32)
    for r in range(K):
        c1 = jnp.where(rank1 == r, cnt[r:r + 1], c1)
    e1 = jnp.where(rank1 < K, jnp.exp(s1 - a[0:1]), 0.0) * (0.5 / zsum)
    e2 = jnp.where(rank2 < K, jnp.exp(s2 - b[0:1]), 0.0)
    return rank2, e2, c1, e1


def _sort_network(n):
    def merge(lo, hi, r):
        step = r * 2
        if step < hi - lo:
            yield from merge(lo, hi, step)
            yield from merge(lo + r, hi, step)
            for i in range(lo + r, hi - r, step):
                yield (i, i + r)
        else:
            yield (lo, lo + r)

    def sort(lo, hi):
        if hi - lo >= 1:
            mid = lo + (hi - lo) // 2
            yield from sort(lo, mid)
            yield from sort(mid + 1, hi)
            yield from merge(lo, hi, 1)

    return list(sort(0, n - 1))


def _exchange(xs, i, j):
    hi = jnp.maximum(xs[i], xs[j])
    xs[j] = jnp.minimum(xs[i], xs[j])
    xs[i] = hi


def _bitonic_finish(xs):
    n = len(xs)
    d = n // 2
    while d >= 1:
        for i in range(n):
            if i & d == 0:
                _exchange(xs, i, i + d)
        d //= 2


def _merge_sublanes(xs):
    n = len(xs)
    for shift in (4, 2, 1):
        other = [pltpu.roll(x, shift, 0) for x in xs]
        xs = [jnp.maximum(xs[i], other[n - 1 - i]) for i in range(n)]
        _bitonic_finish(xs)
    return xs


def _top_sorted(s):
    xs = [s[8 * v:8 * v + 8] for v in range(s.shape[0] // 8)]
    for i, j in _sort_network(len(xs)):
        _exchange(xs, i, j)
    return _merge_sublanes(xs)


def _route_head_sorted(s1, s2):
    K = PEER_TOPK
    t = s1.shape[1]
    a = _top_sorted(s1)
    b = _top_sorted(s2)
    sub = lax.broadcasted_iota(jnp.int32, (8, t), 0)
    a8 = a[7]
    for r in range(6, -1, -1):
        a8 = jnp.where(sub == r, a[r], a8)
    main = [jnp.where((sub + 1) * (c + 1) <= K, a8 + b[c], NEG) for c in range(K)]
    top = _merge_sublanes(list(main))
    single = [a[8 + i] + b[0] for i in range(K // 2)] + [jnp.full((8, t), NEG, F32)] * (K // 2)
    top = [jnp.maximum(top[i], single[K - 1 - i]) for i in range(K)]
    _bitonic_finish(top)
    tau = top[K - 1]
    ea8 = jnp.exp(a8 - a[0])
    cnt8 = jnp.zeros((8, t), F32)
    z8 = jnp.zeros((8, t), F32)
    for c in range(K):
        hit = main[c] >= tau
        cnt8 = cnt8 + jnp.where(hit, 1.0, 0.0)
        z8 = z8 + jnp.where(hit, ea8 * jnp.exp(b[c] - b[0]), 0.0)
    cnt = [jnp.broadcast_to(cnt8[r:r + 1], (8, t)) for r in range(8)]
    zsum = jnp.sum(z8, axis=0, keepdims=True)
    total = jnp.sum(cnt8, axis=0, keepdims=True)
    for i in range(K // 2):
        hit = jnp.where(single[i] >= tau, 1.0, 0.0)
        cnt.append(hit)
        zsum = zsum + hit[0:1] * jnp.exp(a[8 + i][0:1] - a[0][0:1])
        total = total + hit[0:1]
    in1 = jnp.zeros((1, t), F32)
    in2 = jnp.zeros((1, t), F32)
    c1s, e1s, r2s, e2s = [], [], [], []
    scale = 0.5 / zsum
    for v in range(PEER_KEYS // 8):
        x1 = s1[8 * v:8 * v + 8]
        x2 = s2[8 * v:8 * v + 8]
        c1 = jnp.zeros((8, t), F32)
        r2 = jnp.full((8, t), float(K), F32)
        for r in range(K):
            c1 = jnp.where(x1 == a[r], cnt[r], c1)
            r2 = jnp.where(x2 == b[r], float(r), r2)
        top1 = x1 >= a[K - 1]
        top2 = x2 >= b[K - 1]
        in1 = in1 + jnp.sum(jnp.where(top1, 1.0, 0.0), axis=0, keepdims=True)
        in2 = in2 + jnp.sum(jnp.where(top2, 1.0, 0.0), axis=0, keepdims=True)
        c1s.append(c1)
        r2s.append(r2)
        e1s.append(jnp.where(top1, jnp.exp(x1 - a[0]), 0.0) * scale)
        e2s.append(jnp.where(top2, jnp.exp(x2 - b[0]), 0.0))
    bad = (in1 != float(K)) | (in2 != float(K)) | (total != float(K))
    for r in range(K - 1):
        bad = bad | (a[r][0:1] == a[r + 1][0:1]) | (b[r][0:1] == b[r + 1][0:1])
    cat = lambda xs: jnp.concatenate(xs, axis=0)
    return cat(r2s), cat(e2s), cat(c1s), cat(e1s), jnp.where(bad, 1.0, 0.0)


def _route_kernel(qp_ref, kb_ref, r2_ref, e2_ref, c1_ref, e1_ref):
    t = qp_ref.shape[0]
    for hh in range(PEER_HEADS):
        q = qp_ref[:, hh * PEER_DKEY:(hh + 1) * PEER_DKEY]
        st = _dot3(kb_ref[hh], q, _dot_nt)
        s1 = st[:PEER_KEYS]
        s2 = st[PEER_KEYS:]
        r2, e2, c1, e1, bad = _route_head_sorted(s1, s2)
        r2_ref[hh] = r2.astype(BF16)
        e2_ref[hh] = e2.astype(BF16)
        c1_ref[hh] = c1
        e1_ref[hh] = e1

        @pl.when(jnp.max(bad) > 0.0)
        def _():
            r2, e2, c1, e1 = _route_head_iterative(s1, s2, _cand_index(t))
            r2_ref[hh] = r2.astype(BF16)
            e2_ref[hh] = e2.astype(BF16)
            c1_ref[hh] = c1
            e1_ref[hh] = e1


def _route_call(qp, kb, tn):
    T = qp.shape[0]
    out = pl.BlockSpec((PEER_HEADS, PEER_KEYS, tn), lambda i: (0, 0, i))
    shp = jax.ShapeDtypeStruct((PEER_HEADS, PEER_KEYS, T), F32)
    shp16 = jax.ShapeDtypeStruct((PEER_HEADS, PEER_KEYS, T), BF16)
    return pl.pallas_call(
        _route_kernel,
        grid=(T // tn,),
        in_specs=[pl.BlockSpec((tn, PEER_HEADS * PEER_DKEY), lambda i: (i, 0)),
                  pl.BlockSpec(kb.shape, lambda i: (0, 0, 0))],
        out_specs=[out, out, out, out],
        out_shape=[shp16, shp16, shp, shp],
        compiler_params=_params(("parallel",)),
        name="route",
    )(qp, kb)


PEER_SUB = 1024
ROWS16 = 16


def _gelu_half(x):
    k = math.sqrt(2.0 / math.pi)
    return x + x * jnp.tanh(x * (k + (k * 0.044715) * (x * x)))


def _peer_kernel(h2_ref, u_ref, v_ref, r2_ref, e2_ref, c1_ref, e1_ref, x1_ref, g2_ref, fg_ref, o_ref, acc_ref,
                 w_ref, at_ref):
    e = pl.program_id(1)
    ne = pl.num_programs(1)
    te = u_ref.shape[0]

    @pl.when(e == 0)
    def _():
        acc_ref[...] = jnp.zeros_like(acc_ref)

    zero = jnp.zeros((), BF16)
    nsub = h2_ref.shape[1] // PEER_SUB
    toks = [slice(th * PEER_SUB, (th + 1) * PEER_SUB) for th in range(nsub)]

    def gates(th):
        tok = toks[th]
        for ii in range(te // PEER_KEYS):
            gs = [None] * (PEER_KEYS // ROWS16)
            for hh in range(PEER_HEADS):
                c1 = jnp.broadcast_to(c1_ref[hh, ii:ii + 1, tok], (ROWS16, PEER_SUB)).astype(BF16)
                e1 = jnp.broadcast_to(e1_ref[hh, ii:ii + 1, tok], (ROWS16, PEER_SUB)).astype(BF16)
                for k in range(PEER_KEYS // ROWS16):
                    rows = slice(k * ROWS16, (k + 1) * ROWS16)
                    term = jnp.where(r2_ref[hh, rows, tok] < c1, e2_ref[hh, rows, tok], zero) * e1
                    gs[k] = term if gs[k] is None else gs[k] + term
            for k in range(PEER_KEYS // ROWS16):
                lo = ii * PEER_KEYS + k * ROWS16
                w_ref[th, lo:lo + ROWS16, :] = gs[k]

    def project(th):
        at_ref[th] = _dot(u_ref[...], h2_ref[:, toks[th]])

    def activate(th):
        w_ref[th] = w_ref[th] * _gelu_half(at_ref[th].astype(BF16))

    def combine(th):
        acc_ref[:, toks[th]] += _dot_tn(v_ref[...], w_ref[th])

    gates(0)
    project(0)
    for th in range(nsub):
        if th + 1 < nsub:
            gates(th + 1)
        activate(th)
        if th + 1 < nsub:
            project(th + 1)
        combine(th)

    @pl.when(e == ne - 1)
    def _():
        x2 = x1_ref[...] + g2_ref[0] * acc_ref[...].T
        y = x2 * lax.rsqrt(jnp.mean(x2 * x2, axis=-1, keepdims=True) + NORM_EPS) * fg_ref[...]
        o_ref[...] = y


def _peer_call(h2, u_tab, v_tab, r2, e2, c1, e1, x1, g2rows, fg, tn, te, toks_per_batch):
    D, T = h2.shape
    E = u_tab.shape[0]
    rows = te // PEER_KEYS
    tab = pl.BlockSpec((PEER_HEADS, PEER_KEYS, tn), lambda i, e: (0, 0, i))
    sel = pl.BlockSpec((PEER_HEADS, rows, tn), lambda i, e: (0, e, i))
    return pl.pallas_call(
        _peer_kernel,
        grid=(T // tn, E // te),
        in_specs=[pl.BlockSpec((D, tn), lambda i, e: (0, i)),
                  pl.BlockSpec((te, D), lambda i, e: (e, 0)),
                  pl.BlockSpec((te, D), lambda i, e: (e, 0)),
                  tab, tab, sel, sel,
                  pl.BlockSpec((tn, D), lambda i, e: (i, 0)),
                  pl.BlockSpec((1, 1, D), lambda i, e: ((i * tn) // toks_per_batch, 0, 0)),
                  pl.BlockSpec((1, D), lambda i, e: (0, 0))],
        out_specs=pl.BlockSpec((tn, D), lambda i, e: (i, 0)),
        out_shape=jax.ShapeDtypeStruct((T, D), F32),
        scratch_shapes=[pltpu.VMEM((D, tn), F32), pltpu.VMEM((tn // PEER_SUB, te, PEER_SUB), BF16),
                        pltpu.VMEM((tn // PEER_SUB, te, PEER_SUB), F32)],
        compiler_params=_params(("parallel", "arbitrary")),
        name="peer",
    )(h2, u_tab, v_tab, r2, e2, c1, e1, x1, g2rows, fg)


def _rope_tables(L):
    rows = L // GRID_W
    row = jnp.repeat(jnp.arange(rows, dtype=jnp.int32), GRID_W).astype(F32)
    col = jnp.tile(jnp.arange(GRID_W, dtype=jnp.int32), rows).astype(F32)
    f = HEAD_DIM // 4
    inv = ROPE_BASE ** (-jnp.arange(f, dtype=F32) / f)
    ang_r = row[:, None] * inv[None, :]
    ang_c = col[:, None] * inv[None, :]
    ang = jnp.concatenate([ang_r, ang_r, ang_c, ang_c], axis=-1)
    reps = ROT_W // HEAD_DIM
    return jnp.tile(jnp.cos(ang), (1, reps)), jnp.tile(jnp.sin(ang), (1, reps))


def _rot_cols(w):
    rows, n = w.shape
    f = HEAD_DIM // 4
    w4 = w.reshape(rows, n // (2 * f), 2, f)
    return jnp.stack([-w4[:, :, 1], w4[:, :, 0]], axis=2).reshape(rows, n)


def _dup_cols(w):
    return jnp.concatenate([w[:, :HEAD_DIM], w[:, :HEAD_DIM], w[:, HEAD_DIM:], w[:, HEAD_DIM:]], axis=1)


def _dft_table(L):
    N = 2 * L
    s = 64
    kh = jnp.arange(L // s, dtype=jnp.int32)
    kl = jnp.arange(s, dtype=jnp.int32)
    n = jnp.arange(L, dtype=jnp.int32)
    ph1 = ((kh[:, None] * s * n[None, :]) % N).astype(F32) * (2.0 * math.pi / N)
    ph2 = ((kl[:, None] * n[None, :]) % N).astype(F32) * (2.0 * math.pi / N)
    c1, s1 = jnp.cos(ph1)[:, None, :], jnp.sin(ph1)[:, None, :]
    c2, s2 = jnp.cos(ph2)[None, :, :], jnp.sin(ph2)[None, :, :]
    cosm = (c1 * c2 - s1 * s2).reshape(L, L)
    sinm = (s1 * c2 + c1 * s2).reshape(L, L)
    nyq = jnp.where(n % 2 == 0, 1.0, -1.0).astype(F32)
    k = jnp.arange(L, dtype=jnp.int32)
    q = jnp.where(k[:, None] == 0, nyq[None, :], -sinm)
    return cosm.astype(BF16), q.astype(BF16)


def _filter_features(L):
    t = jnp.arange(L, dtype=F32) / L
    bands = jnp.arange(1, HY_EMB_BANDS + 1, dtype=F32)
    ang = 2.0 * math.pi * t[:, None] * bands[None, :]
    z = jnp.concatenate([t[:, None], jnp.cos(ang), jnp.sin(ang)], axis=-1)
    return jnp.pad(z, ((0, 0), (0, LANES - z.shape[1])))


def _tile(n, pref):
    return pref if n % pref == 0 else n


def kernel(x, c, ctx, c_ctx, w_mod, b_mod, norm1_g, w_in, attn_sink, hy_conv_w, hy_conv_b, hy_fw1, hy_fb1, hy_fw2, hy_fb2, hy_fw3, hy_fb3, hy_freq, hy_skip, w_o_attn, w_o_hy, w_out, norm2_g, peer_wq, peer_keys, peer_u, peer_v, final_g):
    B, L, D = x.shape
    assert B == 4 and D == D_MODEL and w_mod.shape[0] == 1
    T = B * L
    li = 0

    c8 = jnp.concatenate([c, c_ctx[None, :], jnp.zeros((3, D), F32)], axis=0)
    mod3 = _mod_call(c8, w_mod[li], b_mod[li]).reshape(8, 6, D)
    g1 = norm1_g[li].reshape(1, D)
    g2 = norm2_g[li].reshape(1, D)

    w = w_in[li]
    wq, wk, wv = w[:, :OFF_K], _dup_cols(w[:, OFF_K:OFF_V]), _dup_cols(w[:, OFF_V:OFF_HY])
    w_cat = jnp.concatenate([wq, wk, wv, w[:, OFF_HY:OFF_G], _rot_cols(wq), _rot_cols(wk)], axis=1).astype(BF16)
    cos_t, sin_t = _rope_tables(L)
    q, k, v, hy = _inproj_call(x, mod3, g1, w_cat, cos_t, sin_t, _tile(L, 512))
    kx, vx = _ctxproj_call(ctx, mod3, g1, jnp.concatenate([wk, wv], axis=1).astype(BF16))

    gsz = N_HEADS // N_KV_HEADS
    sink_b = jnp.broadcast_to(
        jnp.repeat(attn_sink[li].astype(F32).reshape(N_KV_HEADS, gsz), BLOCK, axis=1)[:, :, None],
        (N_KV_HEADS, gsz * BLOCK, LANES))
    y_attn = _attn_call(q, k, v, kx, vx, sink_b)

    H = HY_FILTER_HIDDEN
    w1p = jnp.pad(hy_fw1[li], ((0, LANES - hy_fw1.shape[1]), (0, 0)))
    deltas = jnp.abs(jnp.linspace(math.log(HY_DECAY_TARGET) / HY_SLOW_DECAY,
                                  math.log(HY_DECAY_TARGET) / HY_FAST_DECAY, HY_WIDTH, dtype=F32)).reshape(1, -1)
    hs, hd = _filt_call(_filter_features(L), w1p, hy_fb1[li].reshape(1, H), hy_freq[li].reshape(1, H),
                        hy_fw2[li], hy_fb2[li].reshape(1, H), hy_fw3[li], hy_fb3[li].reshape(1, -1), deltas)
    u, x0c = _sconv_call(hy, hy_conv_w[li], hy_conv_b[li].reshape(1, -1))
    kk = jnp.arange(L // 2, dtype=F32)[:, None] * (math.pi / L)
    cw = jnp.broadcast_to(jnp.cos(kk), (L // 2, LANES))
    sw = jnp.broadcast_to(jnp.sin(kk), (L // 2, LANES))
    yc = _lconv_call(u, hs, hd, _dft_table(L // 2), cw, sw, hy_skip[li].reshape(1, -1), _tile(L // 2, 256))

    x1, h2t, qp = _merge_call(x, mod3, g1, g2, y_attn, yc, x0c, w[:, OFF_G:].astype(BF16),
                             w_o_attn[li].astype(BF16), w_o_hy[li].astype(BF16), w_out[li].astype(BF16),
                             peer_wq[li].astype(BF16), _tile(L, 256))

    keys = peer_keys[li]
    zk = jnp.zeros_like(keys[:, 0])
    kb = jnp.concatenate([jnp.concatenate([keys[:, 0], zk], axis=2),
                          jnp.concatenate([zk, keys[:, 1]], axis=2)], axis=1)
    r2, e2, c1, e1 = _route_call(qp.reshape(T, D), kb, _tile(T, 256))

    out = _peer_call(h2t, peer_u[li].astype(BF16), peer_v[li].astype(BF16), r2, e2, c1, e1,
                     x1.reshape(T, D), mod3[:B, 5:6, :], final_g.reshape(1, D),
                     _tile(T, 1024), 1024, L)
    return out.reshape(B, L, D)
```

```python
import functools
import math

import jax
import jax.numpy as jnp
from jax import lax
from jax.experimental import pallas as pl
from jax.experimental.pallas import tpu as pltpu

F32 = jnp.float32
BF16 = jnp.bfloat16

D_MODEL = 1024
GRID_W = 64
NORM_EPS = 1e-6
N_HEADS = 8
N_KV_HEADS = 2
HEAD_DIM = 64
BLOCK = 128
ROPE_BASE = 10000.0
HY_WIDTH = 512
HY_EMB_BANDS = 16
HY_FILTER_HIDDEN = 64
HY_FAST_DECAY = 0.3
HY_SLOW_DECAY = 1.5
HY_DECAY_TARGET = 1e-2
PEER_HEADS = 8
PEER_KEYS = 128
PEER_TOPK = 16
PEER_DKEY = 128
Q_W = N_HEADS * HEAD_DIM
KV_W = N_KV_HEADS * HEAD_DIM
HY_IN = 3 * HY_WIDTH
OFF_K = Q_W
OFF_V = OFF_K + KV_W
OFF_HY = OFF_V + KV_W
OFF_G = OFF_HY + HY_IN

LANES = 128
VMEM_LIMIT = 56 * 1024 * 1024
NEG = -1e30
KV_DUP = 2 * KV_W


def _params(sem):
    return pltpu.CompilerParams(dimension_semantics=sem, vmem_limit_bytes=VMEM_LIMIT)


def _dot(a, b):
    return lax.dot_general(a, b, (((1,), (0,)), ((), ())), preferred_element_type=F32)


def _dot_nt(a, b):
    return lax.dot_general(a, b, (((1,), (1,)), ((), ())), preferred_element_type=F32)


def _dot_tn(a, b):
    return lax.dot_general(a, b, (((0,), (0,)), ((), ())), preferred_element_type=F32)


def _split(a):
    hi = a.astype(BF16)
    lo = (a - hi.astype(F32)).astype(BF16)
    return hi, lo


def _dot3(a, b, dot=_dot):
    ah, al = _split(a)
    bh, bl = _split(b)
    return dot(ah, bh) + dot(ah, bl) + dot(al, bh)


def _norm_mod(x, g, shift, scale):
    y = x * lax.rsqrt(jnp.mean(x * x, axis=-1, keepdims=True) + NORM_EPS) * g
    return y * (1.0 + scale) + shift


def _mod_kernel(c_ref, w_ref, b_ref, o_ref):
    c = c_ref[...]
    s = c * jax.nn.sigmoid(c)
    o_ref[...] = _dot3(s, w_ref[...]) + b_ref[...]


def _mod_call(c8, w_mod, b_mod):
    n = w_mod.shape[1] // D_MODEL
    return pl.pallas_call(
        _mod_kernel,
        grid=(n,),
        in_specs=[pl.BlockSpec((8, D_MODEL), lambda j: (0, 0)),
                  pl.BlockSpec((D_MODEL, D_MODEL), lambda j: (0, j)),
                  pl.BlockSpec((1, D_MODEL), lambda j: (0, j))],
        out_specs=pl.BlockSpec((8, D_MODEL), lambda j: (0, j)),
        out_shape=jax.ShapeDtypeStruct((8, w_mod.shape[1]), F32),
        compiler_params=_params(("arbitrary",)),
        name="mod",
    )(c8, w_mod, b_mod.reshape(1, -1))


ROT_W = Q_W + KV_DUP
CAT_W = Q_W + 2 * KV_DUP + HY_IN + ROT_W


def _inproj_kernel(x_ref, mod_ref, g_ref, w_ref, cos_ref, sin_ref, q_ref, k_ref, v_ref, hy_ref):
    m = mod_ref[0]
    h = _norm_mod(x_ref[0], g_ref[...], m[0:1], m[1:2]).astype(BF16)
    o_v = ROT_W
    o_hy = o_v + KV_DUP
    o_rot = o_hy + HY_IN
    p = _dot(h, w_ref[:, 0:ROT_W])
    pr = _dot(h, w_ref[:, o_rot:o_rot + ROT_W])
    qk = p * cos_ref[...] + pr * sin_ref[...]
    q_ref[0] = qk[:, :Q_W].astype(BF16)
    k_ref[0] = qk[:, Q_W:].astype(BF16)
    v_ref[0] = _dot(h, w_ref[:, o_v:o_hy]).astype(BF16)
    for j in range(HY_IN // HY_WIDTH):
        lo = o_hy + j * HY_WIDTH
        hy_ref[0, :, j * HY_WIDTH:(j + 1) * HY_WIDTH] = _dot(h, w_ref[:, lo:lo + HY_WIDTH]).astype(BF16)


def _inproj_call(x, mod3, g, w_cat, cos_t, sin_t, tm):
    B, L, D = x.shape
    return pl.pallas_call(
        _inproj_kernel,
        grid=(B, L // tm),
        in_specs=[pl.BlockSpec((1, tm, D), lambda b, i: (b, i, 0)),
                  pl.BlockSpec((1, 6, D), lambda b, i: (b, 0, 0)),
                  pl.BlockSpec((1, D), lambda b, i: (0, 0)),
                  pl.BlockSpec((D, CAT_W), lambda b, i: (0, 0)),
                  pl.BlockSpec((tm, ROT_W), lambda b, i: (i, 0)),
                  pl.BlockSpec((tm, ROT_W), lambda b, i: (i, 0))],
        out_specs=[pl.BlockSpec((1, tm, Q_W), lambda b, i: (b, i, 0)),
                   pl.BlockSpec((1, tm, KV_DUP), lambda b, i: (b, i, 0)),
                   pl.BlockSpec((1, tm, KV_DUP), lambda b, i: (b, i, 0)),
                   pl.BlockSpec((1, tm, HY_IN), lambda b, i: (b, i, 0))],
        out_shape=[jax.ShapeDtypeStruct((B, L, Q_W), BF16),
                   jax.ShapeDtypeStruct((B, L, KV_DUP), BF16),
                   jax.ShapeDtypeStruct((B, L, KV_DUP), BF16),
                   jax.ShapeDtypeStruct((B, L, HY_IN), BF16)],
        compiler_params=_params(("parallel", "arbitrary")),
        name="inproj",
    )(x, mod3, g, w_cat, cos_t, sin_t)


def _ctxproj_kernel(x_ref, mod_ref, g_ref, w_ref, k_ref, v_ref):
    m = mod_ref[0]
    h = _norm_mod(x_ref[0], g_ref[...], m[0:1], m[1:2]).astype(BF16)
    k_ref[0] = _dot(h, w_ref[:, :KV_DUP]).astype(BF16)
    v_ref[0] = _dot(h, w_ref[:, KV_DUP:]).astype(BF16)


def _ctxproj_call(ctx, mod3, g, w_kv):
    B, C, D = ctx.shape
    return pl.pallas_call(
        _ctxproj_kernel,
        grid=(B,),
        in_specs=[pl.BlockSpec((1, C, D), lambda b: (b, 0, 0)),
                  pl.BlockSpec((1, 6, D), lambda b: (4, 0, 0)),
                  pl.BlockSpec((1, D), lambda b: (0, 0)),
                  pl.BlockSpec((D, 2 * KV_DUP), lambda b: (0, 0))],
        out_specs=[pl.BlockSpec((1, C, KV_DUP), lambda b: (b, 0, 0)),
                   pl.BlockSpec((1, C, KV_DUP), lambda b: (b, 0, 0))],
        out_shape=[jax.ShapeDtypeStruct((B, C, KV_DUP), BF16),
                   jax.ShapeDtypeStruct((B, C, KV_DUP), BF16)],
        compiler_params=_params(("arbitrary",)),
        name="ctxproj",
    )(ctx, mod3, g, w_kv)


def _attn_kernel(q_ref, kp_ref, kc_ref, kn_ref, vp_ref, vc_ref, vn_ref, kx_ref, vx_ref, sink_ref, o_ref):
    n = pl.program_id(1)
    nb = pl.num_programs(1)
    q = q_ref[0]
    rows = (N_HEADS // N_KV_HEADS) * BLOCK
    lo = lax.broadcasted_iota(jnp.int32, (BLOCK, LANES), 1) < HEAD_DIM
    r = lax.broadcasted_iota(jnp.int32, (rows, BLOCK), 0) % BLOCK
    c = lax.broadcasted_iota(jnp.int32, (rows, BLOCK), 1)
    ok_prev = jnp.logical_and(c >= r, n > 0)
    ok_next = jnp.logical_and(c <= r, n < nb - 1)
    scale = HEAD_DIM ** -0.5
    zero = jnp.zeros((BLOCK, LANES), BF16)
    for g in range(N_KV_HEADS):
        sl = slice(g * LANES, (g + 1) * LANES)
        qa = q[:, 2 * g * LANES:(2 * g + 1) * LANES]
        qb = q[:, (2 * g + 1) * LANES:(2 * g + 2) * LANES]
        lhs = jnp.concatenate([jnp.where(lo, qa, zero), jnp.where(lo, zero, qa),
                               jnp.where(lo, qb, zero), jnp.where(lo, zero, qb)], axis=0)
        s_p = jnp.where(ok_prev, _dot_nt(lhs, kp_ref[0, :, sl]) * scale, NEG)
        s_c = _dot_nt(lhs, kc_ref[0, :, sl]) * scale
        s_n = jnp.where(ok_next, _dot_nt(lhs, kn_ref[0, :, sl]) * scale, NEG)
        s_x = _dot_nt(lhs, kx_ref[0, :, sl]) * scale
        sink = sink_ref[g][:, 0:1]
        half = s_x.shape[1] // 2
        m = jnp.max(jnp.maximum(jnp.maximum(s_p, s_c), jnp.maximum(s_n, jnp.maximum(s_x[:, :half], s_x[:, half:]))),
                    axis=-1, keepdims=True)
        m = jnp.maximum(m, sink)
        p_p = jnp.exp(s_p - m)
        p_c = jnp.exp(s_c - m)
        p_n = jnp.exp(s_n - m)
        p_x = jnp.exp(s_x - m)
        den = (jnp.sum((p_p + p_c) + (p_n + (p_x[:, :half] + p_x[:, half:])), axis=-1, keepdims=True)
               + jnp.exp(sink - m))
        o = (_dot(p_p.astype(BF16), vp_ref[0, :, sl]) + _dot(p_c.astype(BF16), vc_ref[0, :, sl])
             + _dot(p_n.astype(BF16), vn_ref[0, :, sl]) + _dot(p_x.astype(BF16), vx_ref[0, :, sl]))
        o = o / den
        o_ref[0, :, 2 * g * LANES:(2 * g + 1) * LANES] = jnp.where(
            lo, o[0:BLOCK], o[BLOCK:2 * BLOCK]).astype(BF16)
        o_ref[0, :, (2 * g + 1) * LANES:(2 * g + 2) * LANES] = jnp.where(
            lo, o[2 * BLOCK:3 * BLOCK], o[3 * BLOCK:4 * BLOCK]).astype(BF16)


def _attn_call(q, k, v, kx, vx, sink_b):
    B, L, _ = q.shape
    C = kx.shape[1]
    nb = L // BLOCK
    kv = lambda f: pl.BlockSpec((1, BLOCK, KV_DUP), f)
    prev = lambda b, n: (b, jnp.maximum(n - 1, 0), 0)
    cur = lambda b, n: (b, n, 0)
    nxt = lambda b, n: (b, jnp.minimum(n + 1, nb - 1), 0)
    rows = (N_HEADS // N_KV_HEADS) * BLOCK
    return pl.pallas_call(
        _attn_kernel,
        grid=(B, nb),
        in_specs=[pl.BlockSpec((1, BLOCK, Q_W), cur),
                  kv(prev), kv(cur), kv(nxt), kv(prev), kv(cur), kv(nxt),
                  pl.BlockSpec((1, C, KV_DUP), lambda b, n: (b, 0, 0)),
                  pl.BlockSpec((1, C, KV_DUP), lambda b, n: (b, 0, 0)),
                  pl.BlockSpec((N_KV_HEADS, rows, LANES), lambda b, n: (0, 0, 0))],
        out_specs=pl.BlockSpec((1, BLOCK, Q_W), cur),
        out_shape=jax.ShapeDtypeStruct((B, L, Q_W), BF16),
        compiler_params=_params(("parallel", "arbitrary")),
        name="attn",
    )(q, k, k, k, v, v, v, kx, vx, sink_b)


def _filt_kernel(z_ref, w1_ref, b1_ref, fr_ref, w2_ref, b2_ref, w3f_ref, w3b_ref, b3f_ref, b3b_ref,
                 dl_ref, hs_ref, hd_ref, h_ref, par_ref):
    L = z_ref.shape[0]
    z = z_ref[...]

    @pl.when(pl.program_id(0) == 0)
    def _():
        fr = fr_ref[...]
        h1 = jnp.sin(fr * (_dot3(z, w1_ref[...]) + b1_ref[...]))
        h_ref[...] = jnp.sin(fr * (_dot3(h1, w2_ref[...]) + b2_ref[...]))

    h = h_ref[...]
    decay = jnp.exp(-z[:, 0:1] * dl_ref[...])
    hf = (_dot3(h, w3f_ref[...]) + b3f_ref[...]) * decay
    hb = (_dot3(h, w3b_ref[...]) + b3b_ref[...]) * decay
    row = lax.broadcasted_iota(jnp.int32, hb.shape, 0)
    hb = jnp.where(row < L - 1, hb, 0.0)
    norm = jnp.sum(jnp.abs(hf), axis=0, keepdims=True) + jnp.sum(jnp.abs(hb), axis=0, keepdims=True)
    inv = 1.0 / norm
    hf = hf * inv
    hbs = jnp.where(row >= 1, pltpu.roll(hb, 1, 0), 0.0) * inv
    H = L // 2
    for out_ref, val in ((hs_ref, hf + hbs), (hd_ref, hf - hbs)):
        par_ref[...] = val
        out_ref[0] = par_ref[pl.ds(0, H, stride=2), :].astype(BF16)
        out_ref[1] = par_ref[pl.ds(1, H, stride=2), :].astype(BF16)


def _filt_call(zf, w1p, b1, fr, w2, b2, w3, b3, absdelta):
    L = zf.shape[0]
    nct = HY_WIDTH // LANES
    H = HY_FILTER_HIDDEN
    full = lambda shape: pl.BlockSpec(shape, lambda j: (0, 0))
    return pl.pallas_call(
        _filt_kernel,
        grid=(nct,),
        in_specs=[full((L, LANES)), full((LANES, H)), full((1, H)), full((1, H)), full((H, H)), full((1, H)),
                  pl.BlockSpec((H, LANES), lambda j: (0, j)),
                  pl.BlockSpec((H, LANES), lambda j: (0, nct + j)),
                  pl.BlockSpec((1, LANES), lambda j: (0, j)),
                  pl.BlockSpec((1, LANES), lambda j: (0, nct + j)),
                  pl.BlockSpec((1, LANES), lambda j: (0, j))],
        out_specs=[pl.BlockSpec((2, L // 2, LANES), lambda j: (0, 0, j)),
                   pl.BlockSpec((2, L // 2, LANES), lambda j: (0, 0, j))],
        out_shape=[jax.ShapeDtypeStruct((2, L // 2, HY_WIDTH), BF16),
                   jax.ShapeDtypeStruct((2, L // 2, HY_WIDTH), BF16)],
        scratch_shapes=[pltpu.VMEM((L, H), F32), pltpu.VMEM((L, LANES), F32)],
        compiler_params=_params(("arbitrary",)),
        name="filt",
    )(zf, w1p, b1, fr, w2, b2, w3, w3, b3, b3, absdelta)


def _sconv_kernel(x0_ref, x1_ref, v_ref, w0_ref, w1_ref, w2_ref, b0_ref, b1_ref, b2_ref, u_ref, g_ref, par_ref):
    L = x0_ref.shape[1]
    row = lax.broadcasted_iota(jnp.int32, (L, LANES), 0)

    def conv(z_ref, w_ref, b_ref):
        z = z_ref[0].astype(F32)
        w = w_ref[...]
        zp = jnp.where(row >= 1, pltpu.roll(z, 1, 0), 0.0)
        zn = jnp.where(row < L - 1, pltpu.roll(z, L - 1, 0), 0.0)
        return zp * w[0:1] + z * w[1:2] + zn * w[2:3] + b_ref[...]

    g_ref[0] = conv(x0_ref, w0_ref, b0_ref).astype(BF16)
    par_ref[...] = conv(v_ref, w2_ref, b2_ref) * conv(x1_ref, w1_ref, b1_ref)
    u_ref[0, 0] = par_ref[pl.ds(0, L // 2, stride=2), :].astype(BF16)
    u_ref[0, 1] = par_ref[pl.ds(1, L // 2, stride=2), :].astype(BF16)


def _sconv_call(hy, conv_w, conv_b):
    B, L, _ = hy.shape
    nct = HY_WIDTH // LANES
    zs = lambda part: pl.BlockSpec((1, L, LANES), lambda b, j: (b, 0, part * nct + j))
    ws = lambda part: pl.BlockSpec((3, LANES), lambda b, j: (0, part * nct + j))
    bs = lambda part: pl.BlockSpec((1, LANES), lambda b, j: (0, part * nct + j))
    out = pl.BlockSpec((1, L, LANES), lambda b, j: (b, 0, j))
    return pl.pallas_call(
        _sconv_kernel,
        grid=(B, nct),
        in_specs=[zs(0), zs(1), zs(2), ws(0), ws(1), ws(2), bs(0), bs(1), bs(2)],
        out_specs=[pl.BlockSpec((1, 2, L // 2, LANES), lambda b, j: (b, 0, 0, j)), out],
        out_shape=[jax.ShapeDtypeStruct((B, 2, L // 2, HY_WIDTH), BF16),
                   jax.ShapeDtypeStruct((B, L, HY_WIDTH), BF16)],
        scratch_shapes=[pltpu.VMEM((L, LANES), F32)],
        compiler_params=_params(("parallel", "arbitrary")),
        name="sconv",
    )(hy, hy, hy, conv_w, conv_w, conv_w, conv_b, conv_b, conv_b)


def _lconv_kernel(u_ref, hs_ref, hd_ref, fp_ref, fq_ref, cw_ref, sw_ref, skip_ref, y_ref, rhs_ref, acce_ref, acco_ref,
                  par_ref):
    nbat = u_ref.shape[0]
    H = u_ref.shape[2]
    j = pl.program_id(1)
    nj = pl.num_programs(1)
    tk = fp_ref.shape[0]
    W = nbat * LANES
    blk = lambda i: slice(i * LANES, (i + 1) * LANES)
    HS_E, HS_O, HD_E, HD_O = 2 * nbat, 2 * nbat + 1, 2 * nbat + 2, 2 * nbat + 3

    @pl.when(j == 0)
    def _():
        for b in range(nbat):
            rhs_ref[:, blk(b)] = u_ref[b, 0]
            rhs_ref[:, blk(nbat + b)] = u_ref[b, 1]
        rhs_ref[:, blk(HS_E)] = hs_ref[0]
        rhs_ref[:, blk(HS_O)] = hs_ref[1]
        rhs_ref[:, blk(HD_E)] = hd_ref[0]
        rhs_ref[:, blk(HD_O)] = hd_ref[1]
        acce_ref[...] = jnp.zeros_like(acce_ref)
        acco_ref[...] = jnp.zeros_like(acco_ref)

    fp = fp_ref[...]
    fq = fq_ref[...]
    ap = _dot(fp, rhs_ref[...])
    aq = _dot(fq, rhs_ref[...])
    first = jnp.logical_and(lax.broadcasted_iota(jnp.int32, (tk, LANES), 0) == 0, j == 0)
    cw = cw_ref[...]
    sw = sw_ref[...]
    csum = lambda i: ap[:, blk(i)]
    ssum = lambda i: jnp.where(first, 0.0, -aq[:, blk(i)])
    alt = lambda i: aq[:, blk(i)]

    gs = cw * csum(HS_O) - sw * ssum(HS_O)
    hdd = sw * csum(HD_O) + cw * ssum(HD_O)
    kr, ki = csum(HS_E) + gs, -(ssum(HD_E) + hdd)
    kr2, ki2 = csum(HS_E) - gs, ssum(HD_E) - hdd
    wgt = jnp.where(first, 0.25 / H, 0.5 / H)
    outs = [[], [], [], []]
    for b in range(nbat):
        a, bb, c, d = csum(b), ssum(b), csum(nbat + b), ssum(nbat + b)
        g = cw * c - sw * d
        h = sw * c + cw * d
        xr, xi = a + g, -(bb + h)
        xr2, xi2 = a - g, bb - h
        ar, ai = xr * kr - xi * ki, xr * ki + xi * kr
        br, bi = xr2 * kr2 - xi2 * ki2, xr2 * ki2 + xi2 * kr2
        yer, yei = ar + br, ai - bi
        dr, di = ar - br, ai + bi
        yor, yoi = dr * cw - di * sw, dr * sw + di * cw
        yei = jnp.where(first, 2.0 * (alt(b) * alt(HS_E) - alt(nbat + b) * alt(HD_O)), yei)
        yoi = jnp.where(first, 2.0 * (alt(b) * alt(HD_O) + alt(nbat + b) * alt(HS_E)), yoi)
        for lst, val in zip(outs, (yer, yei, yor, yoi)):
            lst.append((val * wgt).astype(BF16))
    yer, yei, yor, yoi = [jnp.concatenate(x, axis=1) for x in outs]
    acce_ref[...] += _dot_tn(fp, yer) + _dot_tn(fq, yei)
    acco_ref[...] += _dot_tn(fp, yor) + _dot_tn(fq, yoi)

    @pl.when(j == nj - 1)
    def _():
        for b in range(nbat):
            par_ref[pl.ds(0, H, stride=2), :] = acce_ref[:, blk(b)] + u_ref[b, 0].astype(F32) * skip_ref[...]
            par_ref[pl.ds(1, H, stride=2), :] = acco_ref[:, blk(b)] + u_ref[b, 1].astype(F32) * skip_ref[...]
            y_ref[b] = par_ref[...].astype(BF16)


def _lconv_call(u, hs, hd, ftab, cw, sw, skip, tk):
    B, _, H, _ = u.shape
    nct = HY_WIDTH // LANES
    return pl.pallas_call(
        _lconv_kernel,
        grid=(nct, H // tk),
        in_specs=[pl.BlockSpec((B, 2, H, LANES), lambda c, j: (0, 0, 0, c)),
                  pl.BlockSpec((2, H, LANES), lambda c, j: (0, 0, c)),
                  pl.BlockSpec((2, H, LANES), lambda c, j: (0, 0, c)),
                  pl.BlockSpec((tk, H), lambda c, j: (j, 0)),
                  pl.BlockSpec((tk, H), lambda c, j: (j, 0)),
                  pl.BlockSpec((tk, LANES), lambda c, j: (j, 0)),
                  pl.BlockSpec((tk, LANES), lambda c, j: (j, 0)),
                  pl.BlockSpec((1, LANES), lambda c, j: (0, c))],
        out_specs=pl.BlockSpec((B, 2 * H, LANES), lambda c, j: (0, 0, c)),
        out_shape=jax.ShapeDtypeStruct((B, 2 * H, HY_WIDTH), BF16),
        scratch_shapes=[pltpu.VMEM((H, (2 * B + 4) * LANES), BF16),
                        pltpu.VMEM((H, B * LANES), F32),
                        pltpu.VMEM((H, B * LANES), F32),
                        pltpu.VMEM((2 * H, LANES), F32)],
        compiler_params=_params(("parallel", "arbitrary")),
        name="lconv",
    )(u, hs, hd, ftab[0], ftab[1], cw, sw, skip)


def _merge_kernel(x_ref, mod_ref, g1_ref, g2_ref, ya_ref, yc_ref, x0_ref, wg_ref, woa_ref, woh_ref, wout_ref,
                  wq_ref, x1_ref, h2t_ref, qp_ref):
    m = mod_ref[0]
    x = x_ref[0]
    h = _norm_mod(x, g1_ref[...], m[0:1], m[1:2]).astype(BF16)
    D = x.shape[-1]
    gate_a = jax.nn.sigmoid(_dot(h, wg_ref[:, :D]))
    gate_h = jax.nn.sigmoid(_dot(h, wg_ref[:, D:]))
    y_hy = yc_ref[0] * x0_ref[0]
    merged = gate_a * _dot(ya_ref[0], woa_ref[...]) + gate_h * _dot(y_hy, woh_ref[...])
    x1 = x + m[2:3] * _dot(merged.astype(BF16), wout_ref[...])
    x1_ref[0] = x1
    h2 = _norm_mod(x1, g2_ref[...], m[3:4], m[4:5])
    h2t_ref[...] = h2.T.astype(BF16)
    qp_ref[0] = _dot(h2.astype(BF16), wq_ref[...])


def _merge_call(x, mod3, g1, g2, ya, yc, x0c, wg, woa, woh, wout, wq, tm):
    B, L, D = x.shape
    tok = lambda w: pl.BlockSpec((1, tm, w), lambda b, i: (b, i, 0))
    full = lambda a: pl.BlockSpec(a.shape, lambda b, i: (0, 0))
    return pl.pallas_call(
        _merge_kernel,
        grid=(B, L // tm),
        in_specs=[tok(D), pl.BlockSpec((1, 6, D), lambda b, i: (b, 0, 0)), full(g1), full(g2),
                  tok(Q_W), tok(HY_WIDTH), tok(HY_WIDTH), full(wg), full(woa), full(woh), full(wout), full(wq)],
        out_specs=[tok(D), pl.BlockSpec((D, tm), lambda b, i: (0, b * (L // tm) + i)), tok(D)],
        out_shape=[jax.ShapeDtypeStruct((B, L, D), F32),
                   jax.ShapeDtypeStruct((D, B * L), BF16),
                   jax.ShapeDtypeStruct((B, L, D), F32)],
        compiler_params=_params(("parallel", "arbitrary")),
        name="merge",
    )(x, mod3, g1, g2, ya, yc, x0c, wg, woa, woh, wout, wq)


def _topk_rank(s):
    n, t = s.shape
    rowi = lax.broadcasted_iota(jnp.int32, (n, t), 0).astype(F32)
    topi = lax.broadcasted_iota(jnp.int32, (PEER_TOPK, t), 0)

    def body(it, carry):
        s, rank, vals = carry
        mx = jnp.max(s, axis=0, keepdims=True)
        first = jnp.min(jnp.where(s == mx, rowi, float(n)), axis=0, keepdims=True)
        sel = rowi == first
        return (jnp.where(sel, NEG, s), jnp.where(sel, jnp.asarray(it, F32), rank), jnp.where(topi == it, mx, vals))

    init = (s, jnp.full((n, t), float(PEER_TOPK), F32), jnp.zeros((PEER_TOPK, t), F32))
    _, rank, vals = lax.fori_loop(0, PEER_TOPK, body, init)
    return rank, vals


def _cand_blocks():
    blocks = []
    for c in range(PEER_TOPK // 2):
        valid = PEER_TOPK // (c + 1)
        blocks.append((c, -(-valid // 8) * 8, valid))
    return blocks


def _cand_index(t):
    K = PEER_TOPK
    fis = []
    for c, rows, valid in _cand_blocks():
        ri = lax.broadcasted_iota(jnp.int32, (rows, t), 0)
        fis.append(jnp.where(ri < valid, ri * K + c, K * K))
    ci = lax.broadcasted_iota(jnp.int32, (8, t), 0)
    fis.append(ci + K // 2)
    return jnp.concatenate(fis, axis=0).astype(F32)


def _route_head_iterative(s1, s2, fi):
    K = PEER_TOPK
    t = s1.shape[1]
    blocks = _cand_blocks()
    ncand = fi.shape[0]
    rank1, a = _topk_rank(s1)
    rank2, b = _topk_rank(s2)
    cands = []
    for c, rows, valid in blocks:
        cands.append(a[0:rows] + b[c:c + 1])
    cands.append(a[0:1] + b[K // 2:K])
    cand = jnp.where(fi < K * K, jnp.concatenate(cands, axis=0), NEG)

    def body(it, carry):
        cand, sel_all = carry
        mx = jnp.max(cand, axis=0, keepdims=True)
        first = jnp.min(jnp.where(cand == mx, fi, float(K * K)), axis=0, keepdims=True)
        sel = fi == first
        return jnp.where(sel, NEG, cand), jnp.where(sel, 1.0, sel_all)

    _, sel = lax.fori_loop(0, K, body, (cand, jnp.zeros((ncand, t), F32)))
    ea = jnp.exp(a - a[0:1])
    eb = jnp.exp(b - b[0:1])
    cnt = jnp.zeros((K, t), F32)
    zsum = jnp.zeros((1, t), F32)
    off = 0
    for c, rows, valid in blocks:
        blk = sel[off:off + rows]
        off += rows
        if rows < K:
            blk_full = jnp.concatenate([blk, jnp.zeros((K - rows, t), F32)], axis=0)
        else:
            blk_full = blk
        cnt = cnt + blk_full
        zsum = zsum + jnp.sum(blk * ea[0:rows], axis=0, keepdims=True) * eb[c:c + 1]
    tail = sel[off:off + 8]
    tcount = jnp.sum(tail, axis=0, keepdims=True)
    row0 = lax.broadcasted_iota(jnp.int32, (K, t), 0) == 0
    cnt = cnt + jnp.where(row0, tcount, 0.0)
    zsum = zsum + jnp.sum(tail * eb[K // 2:K], axis=0, keepdims=True)
    c1 = jnp.zeros((PEER_KEYS, t), F32)
    for r in range(K):
        c1 = jnp.where(rank1 == r, cnt[r:r + 1], c1)
    e1 = jnp.where(rank1 < K, jnp.exp(s1 - a[0:1]), 0.0) * (0.5 / zsum)
    e2 = jnp.where(rank2 < K, jnp.exp(s2 - b[0:1]), 0.0)
    return rank2, e2, c1, e1


def _sort_network(n):
    def merge(lo, hi, r):
        step = r * 2
        if step < hi - lo:
            yield from merge(lo, hi, step)
            yield from merge(lo + r, hi, step)
            for i in range(lo + r, hi - r, step):
                yield (i, i + r)
        else:
            yield (lo, lo + r)

    def sort(lo, hi):
        if hi - lo >= 1:
            mid = lo + (hi - lo) // 2
            yield from sort(lo, mid)
            yield from sort(mid + 1, hi)
            yield from merge(lo, hi, 1)

    return list(sort(0, n - 1))


def _exchange(xs, i, j):
    hi = jnp.maximum(xs[i], xs[j])
    xs[j] = jnp.minimum(xs[i], xs[j])
    xs[i] = hi


def _bitonic_finish(xs):
    n = len(xs)
    d = n // 2
    while d >= 1:
        for i in range(n):
            if i & d == 0:
                _exchange(xs, i, i + d)
        d //= 2


def _merge_sublanes(xs):
    n = len(xs)
    for shift in (4, 2, 1):
        other = [pltpu.roll(x, shift, 0) for x in xs]
        xs = [jnp.maximum(xs[i], other[n - 1 - i]) for i in range(n)]
        _bitonic_finish(xs)
    return xs


def _top_sorted(s):
    xs = [s[8 * v:8 * v + 8] for v in range(s.shape[0] // 8)]
    for i, j in _sort_network(len(xs)):
        _exchange(xs, i, j)
    return _merge_sublanes(xs)


def _route_head_sorted(s1, s2):
    K = PEER_TOPK
    t = s1.shape[1]
    a = _top_sorted(s1)
    b = _top_sorted(s2)
    sub = lax.broadcasted_iota(jnp.int32, (8, t), 0)
    a8 = a[7]
    for r in range(6, -1, -1):
        a8 = jnp.where(sub == r, a[r], a8)
    main = [jnp.where((sub + 1) * (c + 1) <= K, a8 + b[c], NEG) for c in range(K)]
    top = _merge_sublanes(list(main))
    single = [a[8 + i] + b[0] for i in range(K // 2)] + [jnp.full((8, t), NEG, F32)] * (K // 2)
    top = [jnp.maximum(top[i], single[K - 1 - i]) for i in range(K)]
    _bitonic_finish(top)
    tau = top[K - 1]
    ea8 = jnp.exp(a8 - a[0])
    cnt8 = jnp.zeros((8, t), F32)
    z8 = jnp.zeros((8, t), F32)
    for c in range(K):
        hit = main[c] >= tau
        cnt8 = cnt8 + jnp.where(hit, 1.0, 0.0)
        z8 = z8 + jnp.where(hit, ea8 * jnp.exp(b[c] - b[0]), 0.0)
    cnt = [jnp.broadcast_to(cnt8[r:r + 1], (8, t)) for r in range(8)]
    zsum = jnp.sum(z8, axis=0, keepdims=True)
    total = jnp.sum(cnt8, axis=0, keepdims=True)
    for i in range(K // 2):
        hit = jnp.where(single[i] >= tau, 1.0, 0.0)
        cnt.append(hit)
        zsum = zsum + hit[0:1] * jnp.exp(a[8 + i][0:1] - a[0][0:1])
        total = total + hit[0:1]
    in1 = jnp.zeros((8, t), F32)
    in2 = jnp.zeros((8, t), F32)
    c1s, e1s, r2s, e2s = [], [], [], []
    scale = 0.5 / zsum
    for v in range(PEER_KEYS // 8):
        x1 = s1[8 * v:8 * v + 8]
        x2 = s2[8 * v:8 * v + 8]
        c1 = jnp.zeros((8, t), F32)
        r2 = jnp.full((8, t), float(K), F32)
        for r in range(K):
            c1 = jnp.where(x1 == a[r], cnt[r], c1)
            r2 = jnp.where(x2 == b[r], float(r), r2)
        top1 = x1 >= a[K - 1]
        top2 = x2 >= b[K - 1]
        in1 = in1 + jnp.where(top1, 1.0, 0.0)
        in2 = in2 + jnp.where(top2, 1.0, 0.0)
        c1s.append(c1)
        r2s.append(r2)
        e1s.append(jnp.where(top1, jnp.exp(x1 - a[0]), 0.0) * scale)
        e2s.append(jnp.where(top2, jnp.exp(x2 - b[0]), 0.0))
    in1 = jnp.sum(in1, axis=0, keepdims=True)
    in2 = jnp.sum(in2, axis=0, keepdims=True)
    bad = (in1 != float(K)) | (in2 != float(K)) | (total != float(K))
    for r in range(K - 1):
        bad = bad | (a[r][0:1] == a[r + 1][0:1]) | (b[r][0:1] == b[r + 1][0:1])
    cat = lambda xs: jnp.concatenate(xs, axis=0)
    return cat(r2s), cat(e2s), cat(c1s), cat(e1s), jnp.where(bad, 1.0, 0.0)


def _route_kernel(qp_ref, kb_ref, r2_ref, e2_ref, c1_ref, e1_ref):
    t = qp_ref.shape[0]
    for hh in range(PEER_HEADS):
        q = qp_ref[:, hh * PEER_DKEY:(hh + 1) * PEER_DKEY]
        st = _dot3(kb_ref[hh], q, _dot_nt)
        s1 = st[:PEER_KEYS]
        s2 = st[PEER_KEYS:]
        r2, e2, c1, e1, bad = _route_head_sorted(s1, s2)
        r2_ref[hh] = r2.astype(BF16)
        e2_ref[hh] = e2.astype(BF16)
        c1_ref[hh] = c1
        e1_ref[hh] = e1

        @pl.when(jnp.max(bad) > 0.0)
        def _():
            r2, e2, c1, e1 = _route_head_iterative(s1, s2, _cand_index(t))
            r2_ref[hh] = r2.astype(BF16)
            e2_ref[hh] = e2.astype(BF16)
            c1_ref[hh] = c1
            e1_ref[hh] = e1


def _route_call(qp, kb, tn):
    T = qp.shape[0]
    out = pl.BlockSpec((PEER_HEADS, PEER_KEYS, tn), lambda i: (0, 0, i))
    shp = jax.ShapeDtypeStruct((PEER_HEADS, PEER_KEYS, T), F32)
    shp16 = jax.ShapeDtypeStruct((PEER_HEADS, PEER_KEYS, T), BF16)
    return pl.pallas_call(
        _route_kernel,
        grid=(T // tn,),
        in_specs=[pl.BlockSpec((tn, PEER_HEADS * PEER_DKEY), lambda i: (i, 0)),
                  pl.BlockSpec(kb.shape, lambda i: (0, 0, 0))],
        out_specs=[out, out, out, out],
        out_shape=[shp16, shp16, shp, shp],
        compiler_params=_params(("parallel",)),
        name="route",
    )(qp, kb)


PEER_SUB = 1024
ROWS16 = 16


def _gelu_half(x):
    k = math.sqrt(2.0 / math.pi)
    return x + x * jnp.tanh(x * (k + (k * 0.044715) * (x * x)))


def _peer_kernel(h2_ref, u_ref, v_ref, r2_ref, e2_ref, c1_ref, e1_ref, x1_ref, g2_ref, fg_ref, o_ref, acc_ref,
                 w_ref, at_ref):
    e = pl.program_id(1)
    ne = pl.num_programs(1)
    te = u_ref.shape[0]

    @pl.when(e == 0)
    def _():
        acc_ref[...] = jnp.zeros_like(acc_ref)

    zero = jnp.zeros((), BF16)
    nsub = h2_ref.shape[1] // PEER_SUB
    toks = [slice(th * PEER_SUB, (th + 1) * PEER_SUB) for th in range(nsub)]

    def gates(th):
        tok = toks[th]
        for ii in range(te // PEER_KEYS):
            gs = [None] * (PEER_KEYS // ROWS16)
            for hh in range(PEER_HEADS):
                c1 = jnp.broadcast_to(c1_ref[hh, ii:ii + 1, tok], (ROWS16, PEER_SUB)).astype(BF16)
                e1 = jnp.broadcast_to(e1_ref[hh, ii:ii + 1, tok], (ROWS16, PEER_SUB)).astype(BF16)
                for k in range(PEER_KEYS // ROWS16):
                    rows = slice(k * ROWS16, (k + 1) * ROWS16)
                    term = jnp.where(r2_ref[hh, rows, tok] < c1, e2_ref[hh, rows, tok], zero) * e1
                    gs[k] = term if gs[k] is None else gs[k] + term
            for k in range(PEER_KEYS // ROWS16):
                lo = ii * PEER_KEYS + k * ROWS16
                w_ref[th, lo:lo + ROWS16, :] = gs[k]

    def project(th):
        at_ref[th] = _dot(u_ref[...], h2_ref[:, toks[th]])

    def activate(th):
        w_ref[th] = w_ref[th] * _gelu_half(at_ref[th].astype(BF16))

    def combine(th):
        acc_ref[:, toks[th]] += _dot_tn(v_ref[...], w_ref[th])

    gates(0)
    project(0)
    for th in range(nsub):
        if th + 1 < nsub:
            gates(th + 1)
        activate(th)
        if th + 1 < nsub:
            project(th + 1)
        combine(th)

    @pl.when(e == ne - 1)
    def _():
        x2 = x1_ref[...] + g2_ref[0] * acc_ref[...].T
        y = x2 * lax.rsqrt(jnp.mean(x2 * x2, axis=-1, keepdims=True) + NORM_EPS) * fg_ref[...]
        o_ref[...] = y


def _peer_call(h2, u_tab, v_tab, r2, e2, c1, e1, x1, g2rows, fg, tn, te, toks_per_batch):
    D, T = h2.shape
    E = u_tab.shape[0]
    rows = te // PEER_KEYS
    tab = pl.BlockSpec((PEER_HEADS, PEER_KEYS, tn), lambda i, e: (0, 0, i))
    sel = pl.BlockSpec((PEER_HEADS, rows, tn), lambda i, e: (0, e, i))
    return pl.pallas_call(
        _peer_kernel,
        grid=(T // tn, E // te),
        in_specs=[pl.BlockSpec((D, tn), lambda i, e: (0, i)),
                  pl.BlockSpec((te, D), lambda i, e: (e, 0)),
                  pl.BlockSpec((te, D), lambda i, e: (e, 0)),
                  tab, tab, sel, sel,
                  pl.BlockSpec((tn, D), lambda i, e: (i, 0)),
                  pl.BlockSpec((1, 1, D), lambda i, e: ((i * tn) // toks_per_batch, 0, 0)),
                  pl.BlockSpec((1, D), lambda i, e: (0, 0))],
        out_specs=pl.BlockSpec((tn, D), lambda i, e: (i, 0)),
        out_shape=jax.ShapeDtypeStruct((T, D), F32),
        scratch_shapes=[pltpu.VMEM((D, tn), F32), pltpu.VMEM((tn // PEER_SUB, te, PEER_SUB), BF16),
                        pltpu.VMEM((tn // PEER_SUB, te, PEER_SUB), F32)],
        compiler_params=_params(("parallel", "arbitrary")),
        name="peer",
    )(h2, u_tab, v_tab, r2, e2, c1, e1, x1, g2rows, fg)


def _rope_tables(L):
    rows = L // GRID_W
    row = jnp.repeat(jnp.arange(rows, dtype=jnp.int32), GRID_W).astype(F32)
    col = jnp.tile(jnp.arange(GRID_W, dtype=jnp.int32), rows).astype(F32)
    f = HEAD_DIM // 4
    inv = ROPE_BASE ** (-jnp.arange(f, dtype=F32) / f)
    ang_r = row[:, None] * inv[None, :]
    ang_c = col[:, None] * inv[None, :]
    ang = jnp.concatenate([ang_r, ang_r, ang_c, ang_c], axis=-1)
    reps = ROT_W // HEAD_DIM
    return jnp.tile(jnp.cos(ang), (1, reps)), jnp.tile(jnp.sin(ang), (1, reps))


def _rot_cols(w):
    rows, n = w.shape
    f = HEAD_DIM // 4
    w4 = w.reshape(rows, n // (2 * f), 2, f)
    return jnp.stack([-w4[:, :, 1], w4[:, :, 0]], axis=2).reshape(rows, n)


def _dup_cols(w):
    return jnp.concatenate([w[:, :HEAD_DIM], w[:, :HEAD_DIM], w[:, HEAD_DIM:], w[:, HEAD_DIM:]], axis=1)


def _dft_table(L):
    N = 2 * L
    s = 64
    kh = jnp.arange(L // s, dtype=jnp.int32)
    kl = jnp.arange(s, dtype=jnp.int32)
    n = jnp.arange(L, dtype=jnp.int32)
    ph1 = ((kh[:, None] * s * n[None, :]) % N).astype(F32) * (2.0 * math.pi / N)
    ph2 = ((kl[:, None] * n[None, :]) % N).astype(F32) * (2.0 * math.pi / N)
    c1, s1 = jnp.cos(ph1)[:, None, :], jnp.sin(ph1)[:, None, :]
    c2, s2 = jnp.cos(ph2)[None, :, :], jnp.sin(ph2)[None, :, :]
    cosm = (c1 * c2 - s1 * s2).reshape(L, L)
    sinm = (s1 * c2 + c1 * s2).reshape(L, L)
    nyq = jnp.where(n % 2 == 0, 1.0, -1.0).astype(F32)
    k = jnp.arange(L, dtype=jnp.int32)
    q = jnp.where(k[:, None] == 0, nyq[None, :], -sinm)
    return cosm.astype(BF16), q.astype(BF16)


def _filter_features(L):
    t = jnp.arange(L, dtype=F32) / L
    bands = jnp.arange(1, HY_EMB_BANDS + 1, dtype=F32)
    ang = 2.0 * math.pi * t[:, None] * bands[None, :]
    z = jnp.concatenate([t[:, None], jnp.cos(ang), jnp.sin(ang)], axis=-1)
    return jnp.pad(z, ((0, 0), (0, LANES - z.shape[1])))


def _tile(n, pref):
    return pref if n % pref == 0 else n


def kernel(x, c, ctx, c_ctx, w_mod, b_mod, norm1_g, w_in, attn_sink, hy_conv_w, hy_conv_b, hy_fw1, hy_fb1, hy_fw2, hy_fb2, hy_fw3, hy_fb3, hy_freq, hy_skip, w_o_attn, w_o_hy, w_out, norm2_g, peer_wq, peer_keys, peer_u, peer_v, final_g):
    B, L, D = x.shape
    assert B == 4 and D == D_MODEL and w_mod.shape[0] == 1
    T = B * L
    li = 0

    c8 = jnp.concatenate([c, c_ctx[None, :], jnp.zeros((3, D), F32)], axis=0)
    mod3 = _mod_call(c8, w_mod[li], b_mod[li]).reshape(8, 6, D)
    g1 = norm1_g[li].reshape(1, D)
    g2 = norm2_g[li].reshape(1, D)

    w = w_in[li]
    wq, wk, wv = w[:, :OFF_K], _dup_cols(w[:, OFF_K:OFF_V]), _dup_cols(w[:, OFF_V:OFF_HY])
    w_cat = jnp.concatenate([wq, wk, wv, w[:, OFF_HY:OFF_G], _rot_cols(wq), _rot_cols(wk)], axis=1).astype(BF16)
    cos_t, sin_t = _rope_tables(L)
    q, k, v, hy = _inproj_call(x, mod3, g1, w_cat, cos_t, sin_t, _tile(L, 512))
    kx, vx = _ctxproj_call(ctx, mod3, g1, jnp.concatenate([wk, wv], axis=1).astype(BF16))

    gsz = N_HEADS // N_KV_HEADS
    sink_b = jnp.broadcast_to(
        jnp.repeat(attn_sink[li].astype(F32).reshape(N_KV_HEADS, gsz), BLOCK, axis=1)[:, :, None],
        (N_KV_HEADS, gsz * BLOCK, LANES))
    y_attn = _attn_call(q, k, v, kx, vx, sink_b)

    H = HY_FILTER_HIDDEN
    w1p = jnp.pad(hy_fw1[li], ((0, LANES - hy_fw1.shape[1]), (0, 0)))
    deltas = jnp.abs(jnp.linspace(math.log(HY_DECAY_TARGET) / HY_SLOW_DECAY,
                                  math.log(HY_DECAY_TARGET) / HY_FAST_DECAY, HY_WIDTH, dtype=F32)).reshape(1, -1)
    hs, hd = _filt_call(_filter_features(L), w1p, hy_fb1[li].reshape(1, H), hy_freq[li].reshape(1, H),
                        hy_fw2[li], hy_fb2[li].reshape(1, H), hy_fw3[li], hy_fb3[li].reshape(1, -1), deltas)
    u, x0c = _sconv_call(hy, hy_conv_w[li], hy_conv_b[li].reshape(1, -1))
    kk = jnp.arange(L // 2, dtype=F32)[:, None] * (math.pi / L)
    cw = jnp.broadcast_to(jnp.cos(kk), (L // 2, LANES))
    sw = jnp.broadcast_to(jnp.sin(kk), (L // 2, LANES))
    yc = _lconv_call(u, hs, hd, _dft_table(L // 2), cw, sw, hy_skip[li].reshape(1, -1), _tile(L // 2, 256))

    x1, h2t, qp = _merge_call(x, mod3, g1, g2, y_attn, yc, x0c, w[:, OFF_G:].astype(BF16),
                             w_o_attn[li].astype(BF16), w_o_hy[li].astype(BF16), w_out[li].astype(BF16),
                             peer_wq[li].astype(BF16), _tile(L, 512))

    keys = peer_keys[li]
    zk = jnp.zeros_like(keys[:, 0])
    kb = jnp.concatenate([jnp.concatenate([keys[:, 0], zk], axis=2),
                          jnp.concatenate([zk, keys[:, 1]], axis=2)], axis=1)
    r2, e2, c1, e1 = _route_call(qp.reshape(T, D), kb, _tile(T, 512))

    out = _peer_call(h2t, peer_u[li].astype(BF16), peer_v[li].astype(BF16), r2, e2, c1, e1,
                     x1.reshape(T, D), mod3[:B, 5:6, :], final_g.reshape(1, D),
                     _tile(T, 1024), 1024, L)
    return out.reshape(B, L, D)
```

```python
import functools
import math

import jax
import jax.numpy as jnp
from jax import lax
from jax.experimental import pallas as pl
from jax.experimental.pallas import tpu as pltpu

F32 = jnp.float32
BF16 = jnp.bfloat16

D_MODEL = 1024
GRID_W = 64
NORM_EPS = 1e-6
N_HEADS = 8
N_KV_HEADS = 2
HEAD_DIM = 64
BLOCK = 128
ROPE_BASE = 10000.0
HY_WIDTH = 512
HY_EMB_BANDS = 16
HY_FILTER_HIDDEN = 64
HY_FAST_DECAY = 0.3
HY_SLOW_DECAY = 1.5
HY_DECAY_TARGET = 1e-2
PEER_HEADS = 8
PEER_KEYS = 128
PEER_TOPK = 16
PEER_DKEY = 128
Q_W = N_HEADS * HEAD_DIM
KV_W = N_KV_HEADS * HEAD_DIM
HY_IN = 3 * HY_WIDTH
OFF_K = Q_W
OFF_V = OFF_K + KV_W
OFF_HY = OFF_V + KV_W
OFF_G = OFF_HY + HY_IN

LANES = 128
VMEM_LIMIT = 56 * 1024 * 1024
NEG = -1e30
KV_DUP = 2 * KV_W


def _params(sem):
    return pltpu.CompilerParams(dimension_semantics=sem, vmem_limit_bytes=VMEM_LIMIT)


def _dot(a, b):
    return lax.dot_general(a, b, (((1,), (0,)), ((), ())), preferred_element_type=F32)


def _dot_nt(a, b):
    return lax.dot_general(a, b, (((1,), (1,)), ((), ())), preferred_element_type=F32)


def _dot_tn(a, b):
    return lax.dot_general(a, b, (((0,), (0,)), ((), ())), preferred_element_type=F32)


def _split(a):
    hi = a.astype(BF16)
    lo = (a - hi.astype(F32)).astype(BF16)
    return hi, lo


def _dot3(a, b, dot=_dot):
    ah, al = _split(a)
    bh, bl = _split(b)
    return dot(ah, bh) + dot(ah, bl) + dot(al, bh)


def _norm_mod(x, g, shift, scale):
    y = x * lax.rsqrt(jnp.mean(x * x, axis=-1, keepdims=True) + NORM_EPS) * g
    return y * (1.0 + scale) + shift


def _mod_kernel(c_ref, w_ref, b_ref, o_ref):
    c = c_ref[...]
    s = c * jax.nn.sigmoid(c)
    o_ref[...] = _dot3(s, w_ref[...]) + b_ref[...]


def _mod_call(c8, w_mod, b_mod):
    n = w_mod.shape[1] // D_MODEL
    return pl.pallas_call(
        _mod_kernel,
        grid=(n,),
        in_specs=[pl.BlockSpec((8, D_MODEL), lambda j: (0, 0)),
                  pl.BlockSpec((D_MODEL, D_MODEL), lambda j: (0, j)),
                  pl.BlockSpec((1, D_MODEL), lambda j: (0, j))],
        out_specs=pl.BlockSpec((8, D_MODEL), lambda j: (0, j)),
        out_shape=jax.ShapeDtypeStruct((8, w_mod.shape[1]), F32),
        compiler_params=_params(("arbitrary",)),
        name="mod",
    )(c8, w_mod, b_mod.reshape(1, -1))


ROT_W = Q_W + KV_DUP
CAT_W = Q_W + 2 * KV_DUP + HY_IN + ROT_W


def _inproj_kernel(x_ref, mod_ref, g_ref, w_ref, cos_ref, sin_ref, q_ref, k_ref, v_ref, hy_ref):
    m = mod_ref[0]
    h = _norm_mod(x_ref[0], g_ref[...], m[0:1], m[1:2]).astype(BF16)
    o_v = ROT_W
    o_hy = o_v + KV_DUP
    o_rot = o_hy + HY_IN
    p = _dot(h, w_ref[:, 0:ROT_W])
    pr = _dot(h, w_ref[:, o_rot:o_rot + ROT_W])
    qk = p * cos_ref[...] + pr * sin_ref[...]
    q_ref[0] = qk[:, :Q_W].astype(BF16)
    k_ref[0] = qk[:, Q_W:].astype(BF16)
    v_ref[0] = _dot(h, w_ref[:, o_v:o_hy]).astype(BF16)
    for j in range(HY_IN // HY_WIDTH):
        lo = o_hy + j * HY_WIDTH
        hy_ref[0, :, j * HY_WIDTH:(j + 1) * HY_WIDTH] = _dot(h, w_ref[:, lo:lo + HY_WIDTH]).astype(BF16)


def _inproj_call(x, mod3, g, w_cat, cos_t, sin_t, tm):
    B, L, D = x.shape
    return pl.pallas_call(
        _inproj_kernel,
        grid=(B, L // tm),
        in_specs=[pl.BlockSpec((1, tm, D), lambda b, i: (b, i, 0)),
                  pl.BlockSpec((1, 6, D), lambda b, i: (b, 0, 0)),
                  pl.BlockSpec((1, D), lambda b, i: (0, 0)),
                  pl.BlockSpec((D, CAT_W), lambda b, i: (0, 0)),
                  pl.BlockSpec((tm, ROT_W), lambda b, i: (i, 0)),
                  pl.BlockSpec((tm, ROT_W), lambda b, i: (i, 0))],
        out_specs=[pl.BlockSpec((1, tm, Q_W), lambda b, i: (b, i, 0)),
                   pl.BlockSpec((1, tm, KV_DUP), lambda b, i: (b, i, 0)),
                   pl.BlockSpec((1, tm, KV_DUP), lambda b, i: (b, i, 0)),
                   pl.BlockSpec((1, tm, HY_IN), lambda b, i: (b, i, 0))],
        out_shape=[jax.ShapeDtypeStruct((B, L, Q_W), BF16),
                   jax.ShapeDtypeStruct((B, L, KV_DUP), BF16),
                   jax.ShapeDtypeStruct((B, L, KV_DUP), BF16),
                   jax.ShapeDtypeStruct((B, L, HY_IN), BF16)],
        compiler_params=_params(("parallel", "arbitrary")),
        name="inproj",
    )(x, mod3, g, w_cat, cos_t, sin_t)


def _ctxproj_kernel(x_ref, mod_ref, g_ref, w_ref, k_ref, v_ref):
    m = mod_ref[0]
    h = _norm_mod(x_ref[0], g_ref[...], m[0:1], m[1:2]).astype(BF16)
    k_ref[0] = _dot(h, w_ref[:, :KV_DUP]).astype(BF16)
    v_ref[0] = _dot(h, w_ref[:, KV_DUP:]).astype(BF16)


def _ctxproj_call(ctx, mod3, g, w_kv):
    B, C, D = ctx.shape
    return pl.pallas_call(
        _ctxproj_kernel,
        grid=(B,),
        in_specs=[pl.BlockSpec((1, C, D), lambda b: (b, 0, 0)),
                  pl.BlockSpec((1, 6, D), lambda b: (4, 0, 0)),
                  pl.BlockSpec((1, D), lambda b: (0, 0)),
                  pl.BlockSpec((D, 2 * KV_DUP), lambda b: (0, 0))],
        out_specs=[pl.BlockSpec((1, C, KV_DUP), lambda b: (b, 0, 0)),
                   pl.BlockSpec((1, C, KV_DUP), lambda b: (b, 0, 0))],
        out_shape=[jax.ShapeDtypeStruct((B, C, KV_DUP), BF16),
                   jax.ShapeDtypeStruct((B, C, KV_DUP), BF16)],
        compiler_params=_params(("arbitrary",)),
        name="ctxproj",
    )(ctx, mod3, g, w_kv)


def _attn_kernel(q_ref, kp_ref, kc_ref, kn_ref, vp_ref, vc_ref, vn_ref, kx_ref, vx_ref, sink_ref, o_ref):
    n = pl.program_id(1)
    nb = pl.num_programs(1)
    q = q_ref[0]
    rows = (N_HEADS // N_KV_HEADS) * BLOCK
    lo = lax.broadcasted_iota(jnp.int32, (BLOCK, LANES), 1) < HEAD_DIM
    r = lax.broadcasted_iota(jnp.int32, (rows, BLOCK), 0) % BLOCK
    c = lax.broadcasted_iota(jnp.int32, (rows, BLOCK), 1)
    ok_prev = jnp.logical_and(c >= r, n > 0)
    ok_next = jnp.logical_and(c <= r, n < nb - 1)
    scale = HEAD_DIM ** -0.5
    zero = jnp.zeros((BLOCK, LANES), BF16)
    for g in range(N_KV_HEADS):
        sl = slice(g * LANES, (g + 1) * LANES)
        qa = q[:, 2 * g * LANES:(2 * g + 1) * LANES]
        qb = q[:, (2 * g + 1) * LANES:(2 * g + 2) * LANES]
        lhs = jnp.concatenate([jnp.where(lo, qa, zero), jnp.where(lo, zero, qa),
                               jnp.where(lo, qb, zero), jnp.where(lo, zero, qb)], axis=0)
        s_p = jnp.where(ok_prev, _dot_nt(lhs, kp_ref[0, :, sl]) * scale, NEG)
        s_c = _dot_nt(lhs, kc_ref[0, :, sl]) * scale
        s_n = jnp.where(ok_next, _dot_nt(lhs, kn_ref[0, :, sl]) * scale, NEG)
        s_x = _dot_nt(lhs, kx_ref[0, :, sl]) * scale
        sink = sink_ref[g][:, 0:1]
        half = s_x.shape[1] // 2
        m = jnp.max(jnp.maximum(jnp.maximum(s_p, s_c), jnp.maximum(s_n, jnp.maximum(s_x[:, :half], s_x[:, half:]))),
                    axis=-1, keepdims=True)
        m = jnp.maximum(m, sink)
        p_p = jnp.exp(s_p - m)
        p_c = jnp.exp(s_c - m)
        p_n = jnp.exp(s_n - m)
        p_x = jnp.exp(s_x - m)
        den = (jnp.sum((p_p + p_c) + (p_n + (p_x[:, :half] + p_x[:, half:])), axis=-1, keepdims=True)
               + jnp.exp(sink - m))
        o = (_dot(p_p.astype(BF16), vp_ref[0, :, sl]) + _dot(p_c.astype(BF16), vc_ref[0, :, sl])
             + _dot(p_n.astype(BF16), vn_ref[0, :, sl]) + _dot(p_x.astype(BF16), vx_ref[0, :, sl]))
        o = o / den
        o_ref[0, :, 2 * g * LANES:(2 * g + 1) * LANES] = jnp.where(
            lo, o[0:BLOCK], o[BLOCK:2 * BLOCK]).astype(BF16)
        o_ref[0, :, (2 * g + 1) * LANES:(2 * g + 2) * LANES] = jnp.where(
            lo, o[2 * BLOCK:3 * BLOCK], o[3 * BLOCK:4 * BLOCK]).astype(BF16)


def _attn_call(q, k, v, kx, vx, sink_b):
    B, L, _ = q.shape
    C = kx.shape[1]
    nb = L // BLOCK
    kv = lambda f: pl.BlockSpec((1, BLOCK, KV_DUP), f)
    prev = lambda b, n: (b, jnp.maximum(n - 1, 0), 0)
    cur = lambda b, n: (b, n, 0)
    nxt = lambda b, n: (b, jnp.minimum(n + 1, nb - 1), 0)
    rows = (N_HEADS // N_KV_HEADS) * BLOCK
    return pl.pallas_call(
        _attn_kernel,
        grid=(B, nb),
        in_specs=[pl.BlockSpec((1, BLOCK, Q_W), cur),
                  kv(prev), kv(cur), kv(nxt), kv(prev), kv(cur), kv(nxt),
                  pl.BlockSpec((1, C, KV_DUP), lambda b, n: (b, 0, 0)),
                  pl.BlockSpec((1, C, KV_DUP), lambda b, n: (b, 0, 0)),
                  pl.BlockSpec((N_KV_HEADS, rows, LANES), lambda b, n: (0, 0, 0))],
        out_specs=pl.BlockSpec((1, BLOCK, Q_W), cur),
        out_shape=jax.ShapeDtypeStruct((B, L, Q_W), BF16),
        compiler_params=_params(("parallel", "arbitrary")),
        name="attn",
    )(q, k, k, k, v, v, v, kx, vx, sink_b)


def _filt_kernel(z_ref, w1_ref, b1_ref, fr_ref, w2_ref, b2_ref, w3f_ref, w3b_ref, b3f_ref, b3b_ref,
                 dl_ref, hs_ref, hd_ref, h_ref, par_ref):
    L = z_ref.shape[0]
    z = z_ref[...]

    @pl.when(pl.program_id(0) == 0)
    def _():
        fr = fr_ref[...]
        h1 = jnp.sin(fr * (_dot3(z, w1_ref[...]) + b1_ref[...]))
        h_ref[...] = jnp.sin(fr * (_dot3(h1, w2_ref[...]) + b2_ref[...]))

    h = h_ref[...]
    decay = jnp.exp(-z[:, 0:1] * dl_ref[...])
    hf = (_dot3(h, w3f_ref[...]) + b3f_ref[...]) * decay
    hb = (_dot3(h, w3b_ref[...]) + b3b_ref[...]) * decay
    row = lax.broadcasted_iota(jnp.int32, hb.shape, 0)
    hb = jnp.where(row < L - 1, hb, 0.0)
    norm = jnp.sum(jnp.abs(hf), axis=0, keepdims=True) + jnp.sum(jnp.abs(hb), axis=0, keepdims=True)
    inv = 1.0 / norm
    hf = hf * inv
    hbs = jnp.where(row >= 1, pltpu.roll(hb, 1, 0), 0.0) * inv
    H = L // 2
    for out_ref, val in ((hs_ref, hf + hbs), (hd_ref, hf - hbs)):
        par_ref[...] = val
        out_ref[0] = par_ref[pl.ds(0, H, stride=2), :].astype(BF16)
        out_ref[1] = par_ref[pl.ds(1, H, stride=2), :].astype(BF16)


def _filt_call(zf, w1p, b1, fr, w2, b2, w3, b3, absdelta):
    L = zf.shape[0]
    nct = HY_WIDTH // LANES
    H = HY_FILTER_HIDDEN
    full = lambda shape: pl.BlockSpec(shape, lambda j: (0, 0))
    return pl.pallas_call(
        _filt_kernel,
        grid=(nct,),
        in_specs=[full((L, LANES)), full((LANES, H)), full((1, H)), full((1, H)), full((H, H)), full((1, H)),
                  pl.BlockSpec((H, LANES), lambda j: (0, j)),
                  pl.BlockSpec((H, LANES), lambda j: (0, nct + j)),
                  pl.BlockSpec((1, LANES), lambda j: (0, j)),
                  pl.BlockSpec((1, LANES), lambda j: (0, nct + j)),
                  pl.BlockSpec((1, LANES), lambda j: (0, j))],
        out_specs=[pl.BlockSpec((2, L // 2, LANES), lambda j: (0, 0, j)),
                   pl.BlockSpec((2, L // 2, LANES), lambda j: (0, 0, j))],
        out_shape=[jax.ShapeDtypeStruct((2, L // 2, HY_WIDTH), BF16),
                   jax.ShapeDtypeStruct((2, L // 2, HY_WIDTH), BF16)],
        scratch_shapes=[pltpu.VMEM((L, H), F32), pltpu.VMEM((L, LANES), F32)],
        compiler_params=_params(("arbitrary",)),
        name="filt",
    )(zf, w1p, b1, fr, w2, b2, w3, w3, b3, b3, absdelta)


def _sconv_kernel(x0_ref, x1_ref, v_ref, w0_ref, w1_ref, w2_ref, b0_ref, b1_ref, b2_ref, u_ref, g_ref, par_ref):
    L = x0_ref.shape[1]
    row = lax.broadcasted_iota(jnp.int32, (L, LANES), 0)

    def conv(z_ref, w_ref, b_ref):
        z = z_ref[0].astype(F32)
        w = w_ref[...]
        zp = jnp.where(row >= 1, pltpu.roll(z, 1, 0), 0.0)
        zn = jnp.where(row < L - 1, pltpu.roll(z, L - 1, 0), 0.0)
        return zp * w[0:1] + z * w[1:2] + zn * w[2:3] + b_ref[...]

    g_ref[0] = conv(x0_ref, w0_ref, b0_ref).astype(BF16)
    par_ref[...] = conv(v_ref, w2_ref, b2_ref) * conv(x1_ref, w1_ref, b1_ref)
    u_ref[0, 0] = par_ref[pl.ds(0, L // 2, stride=2), :].astype(BF16)
    u_ref[0, 1] = par_ref[pl.ds(1, L // 2, stride=2), :].astype(BF16)


def _sconv_call(hy, conv_w, conv_b):
    B, L, _ = hy.shape
    nct = HY_WIDTH // LANES
    zs = lambda part: pl.BlockSpec((1, L, LANES), lambda b, j: (b, 0, part * nct + j))
    ws = lambda part: pl.BlockSpec((3, LANES), lambda b, j: (0, part * nct + j))
    bs = lambda part: pl.BlockSpec((1, LANES), lambda b, j: (0, part * nct + j))
    out = pl.BlockSpec((1, L, LANES), lambda b, j: (b, 0, j))
    return pl.pallas_call(
        _sconv_kernel,
        grid=(B, nct),
        in_specs=[zs(0), zs(1), zs(2), ws(0), ws(1), ws(2), bs(0), bs(1), bs(2)],
        out_specs=[pl.BlockSpec((1, 2, L // 2, LANES), lambda b, j: (b, 0, 0, j)), out],
        out_shape=[jax.ShapeDtypeStruct((B, 2, L // 2, HY_WIDTH), BF16),
                   jax.ShapeDtypeStruct((B, L, HY_WIDTH), BF16)],
        scratch_shapes=[pltpu.VMEM((L, LANES), F32)],
        compiler_params=_params(("parallel", "arbitrary")),
        name="sconv",
    )(hy, hy, hy, conv_w, conv_w, conv_w, conv_b, conv_b, conv_b)


def _lconv_kernel(u_ref, hs_ref, hd_ref, fp_ref, fq_ref, cw_ref, sw_ref, skip_ref, y_ref, rhs_ref, acce_ref, acco_ref,
                  par_ref):
    nbat = u_ref.shape[0]
    H = u_ref.shape[2]
    j = pl.program_id(1)
    nj = pl.num_programs(1)
    tk = fp_ref.shape[0]
    W = nbat * LANES
    blk = lambda i: slice(i * LANES, (i + 1) * LANES)
    HS_E, HS_O, HD_E, HD_O = 2 * nbat, 2 * nbat + 1, 2 * nbat + 2, 2 * nbat + 3

    @pl.when(j == 0)
    def _():
        for b in range(nbat):
            rhs_ref[:, blk(b)] = u_ref[b, 0]
            rhs_ref[:, blk(nbat + b)] = u_ref[b, 1]
        rhs_ref[:, blk(HS_E)] = hs_ref[0]
        rhs_ref[:, blk(HS_O)] = hs_ref[1]
        rhs_ref[:, blk(HD_E)] = hd_ref[0]
        rhs_ref[:, blk(HD_O)] = hd_ref[1]
        acce_ref[...] = jnp.zeros_like(acce_ref)
        acco_ref[...] = jnp.zeros_like(acco_ref)

    fp = fp_ref[...]
    fq = fq_ref[...]
    ap = _dot(fp, rhs_ref[...])
    aq = _dot(fq, rhs_ref[...])
    first = jnp.logical_and(lax.broadcasted_iota(jnp.int32, (tk, LANES), 0) == 0, j == 0)
    cw = cw_ref[...]
    sw = sw_ref[...]
    csum = lambda i: ap[:, blk(i)]
    ssum = lambda i: jnp.where(first, 0.0, -aq[:, blk(i)])
    alt = lambda i: aq[:, blk(i)]

    gs = cw * csum(HS_O) - sw * ssum(HS_O)
    hdd = sw * csum(HD_O) + cw * ssum(HD_O)
    kr, ki = csum(HS_E) + gs, -(ssum(HD_E) + hdd)
    kr2, ki2 = csum(HS_E) - gs, ssum(HD_E) - hdd
    wgt = jnp.where(first, 0.25 / H, 0.5 / H)
    outs = [[], [], [], []]
    for b in range(nbat):
        a, bb, c, d = csum(b), ssum(b), csum(nbat + b), ssum(nbat + b)
        g = cw * c - sw * d
        h = sw * c + cw * d
        xr, xi = a + g, -(bb + h)
        xr2, xi2 = a - g, bb - h
        ar, ai = xr * kr - xi * ki, xr * ki + xi * kr
        br, bi = xr2 * kr2 - xi2 * ki2, xr2 * ki2 + xi2 * kr2
        yer, yei = ar + br, ai - bi
        dr, di = ar - br, ai + bi
        yor, yoi = dr * cw - di * sw, dr * sw + di * cw
        yei = jnp.where(first, 2.0 * (alt(b) * alt(HS_E) - alt(nbat + b) * alt(HD_O)), yei)
        yoi = jnp.where(first, 2.0 * (alt(b) * alt(HD_O) + alt(nbat + b) * alt(HS_E)), yoi)
        for lst, val in zip(outs, (yer, yei, yor, yoi)):
            lst.append((val * wgt).astype(BF16))
    yer, yei, yor, yoi = [jnp.concatenate(x, axis=1) for x in outs]
    acce_ref[...] += _dot_tn(fp, yer) + _dot_tn(fq, yei)
    acco_ref[...] += _dot_tn(fp, yor) + _dot_tn(fq, yoi)

    @pl.when(j == nj - 1)
    def _():
        for b in range(nbat):
            par_ref[pl.ds(0, H, stride=2), :] = acce_ref[:, blk(b)] + u_ref[b, 0].astype(F32) * skip_ref[...]
            par_ref[pl.ds(1, H, stride=2), :] = acco_ref[:, blk(b)] + u_ref[b, 1].astype(F32) * skip_ref[...]
            y_ref[b] = par_ref[...].astype(BF16)


def _lconv_call(u, hs, hd, ftab, cw, sw, skip, tk):
    B, _, H, _ = u.shape
    nct = HY_WIDTH // LANES
    return pl.pallas_call(
        _lconv_kernel,
        grid=(nct, H // tk),
        in_specs=[pl.BlockSpec((B, 2, H, LANES), lambda c, j: (0, 0, 0, c)),
                  pl.BlockSpec((2, H, LANES), lambda c, j: (0, 0, c)),
                  pl.BlockSpec((2, H, LANES), lambda c, j: (0, 0, c)),
                  pl.BlockSpec((tk, H), lambda c, j: (j, 0)),
                  pl.BlockSpec((tk, H), lambda c, j: (j, 0)),
                  pl.BlockSpec((tk, LANES), lambda c, j: (j, 0)),
                  pl.BlockSpec((tk, LANES), lambda c, j: (j, 0)),
                  pl.BlockSpec((1, LANES), lambda c, j: (0, c))],
        out_specs=pl.BlockSpec((B, 2 * H, LANES), lambda c, j: (0, 0, c)),
        out_shape=jax.ShapeDtypeStruct((B, 2 * H, HY_WIDTH), BF16),
        scratch_shapes=[pltpu.VMEM((H, (2 * B + 4) * LANES), BF16),
                        pltpu.VMEM((H, B * LANES), F32),
                        pltpu.VMEM((H, B * LANES), F32),
                        pltpu.VMEM((2 * H, LANES), F32)],
        compiler_params=_params(("parallel", "arbitrary")),
        name="lconv",
    )(u, hs, hd, ftab[0], ftab[1], cw, sw, skip)


def _merge_kernel(x_ref, mod_ref, g1_ref, g2_ref, ya_ref, yc_ref, x0_ref, wg_ref, woa_ref, woh_ref, wout_ref,
                  wq_ref, x1_ref, h2t_ref, qp_ref):
    m = mod_ref[0]
    x = x_ref[0]
    h = _norm_mod(x, g1_ref[...], m[0:1], m[1:2]).astype(BF16)
    D = x.shape[-1]
    gate_a = jax.nn.sigmoid(_dot(h, wg_ref[:, :D]))
    gate_h = jax.nn.sigmoid(_dot(h, wg_ref[:, D:]))
    y_hy = yc_ref[0] * x0_ref[0]
    merged = gate_a * _dot(ya_ref[0], woa_ref[...]) + gate_h * _dot(y_hy, woh_ref[...])
    x1 = x + m[2:3] * _dot(merged.astype(BF16), wout_ref[...])
    x1_ref[0] = x1
    h2 = _norm_mod(x1, g2_ref[...], m[3:4], m[4:5])
    h2t_ref[...] = h2.T.astype(BF16)
    qp_ref[0] = _dot(h2.astype(BF16), wq_ref[...])


def _merge_call(x, mod3, g1, g2, ya, yc, x0c, wg, woa, woh, wout, wq, tm):
    B, L, D = x.shape
    tok = lambda w: pl.BlockSpec((1, tm, w), lambda b, i: (b, i, 0))
    full = lambda a: pl.BlockSpec(a.shape, lambda b, i: (0, 0))
    return pl.pallas_call(
        _merge_kernel,
        grid=(B, L // tm),
        in_specs=[tok(D), pl.BlockSpec((1, 6, D), lambda b, i: (b, 0, 0)), full(g1), full(g2),
                  tok(Q_W), tok(HY_WIDTH), tok(HY_WIDTH), full(wg), full(woa), full(woh), full(wout), full(wq)],
        out_specs=[tok(D), pl.BlockSpec((D, tm), lambda b, i: (0, b * (L // tm) + i)), tok(D)],
        out_shape=[jax.ShapeDtypeStruct((B, L, D), F32),
                   jax.ShapeDtypeStruct((D, B * L), BF16),
                   jax.ShapeDtypeStruct((B, L, D), F32)],
        compiler_params=_params(("parallel", "arbitrary")),
        name="merge",
    )(x, mod3, g1, g2, ya, yc, x0c, wg, woa, woh, wout, wq)


def _topk_rank(s):
    n, t = s.shape
    rowi = lax.broadcasted_iota(jnp.int32, (n, t), 0).astype(F32)
    topi = lax.broadcasted_iota(jnp.int32, (PEER_TOPK, t), 0)

    def body(it, carry):
        s, rank, vals = carry
        mx = jnp.max(s, axis=0, keepdims=True)
        first = jnp.min(jnp.where(s == mx, rowi, float(n)), axis=0, keepdims=True)
        sel = rowi == first
        return (jnp.where(sel, NEG, s), jnp.where(sel, jnp.asarray(it, F32), rank), jnp.where(topi == it, mx, vals))

    init = (s, jnp.full((n, t), float(PEER_TOPK), F32), jnp.zeros((PEER_TOPK, t), F32))
    _, rank, vals = lax.fori_loop(0, PEER_TOPK, body, init)
    return rank, vals


def _cand_blocks():
    blocks = []
    for c in range(PEER_TOPK // 2):
        valid = PEER_TOPK // (c + 1)
        blocks.append((c, -(-valid // 8) * 8, valid))
    return blocks


def _cand_index(t):
    K = PEER_TOPK
    fis = []
    for c, rows, valid in _cand_blocks():
        ri = lax.broadcasted_iota(jnp.int32, (rows, t), 0)
        fis.append(jnp.where(ri < valid, ri * K + c, K * K))
    ci = lax.broadcasted_iota(jnp.int32, (8, t), 0)
    fis.append(ci + K // 2)
    return jnp.concatenate(fis, axis=0).astype(F32)


def _route_head_iterative(s1, s2, fi):
    K = PEER_TOPK
    t = s1.shape[1]
    blocks = _cand_blocks()
    ncand = fi.shape[0]
    rank1, a = _topk_rank(s1)
    rank2, b = _topk_rank(s2)
    cands = []
    for c, rows, valid in blocks:
        cands.append(a[0:rows] + b[c:c + 1])
    cands.append(a[0:1] + b[K // 2:K])
    cand = jnp.where(fi < K * K, jnp.concatenate(cands, axis=0), NEG)

    def body(it, carry):
        cand, sel_all = carry
        mx = jnp.max(cand, axis=0, keepdims=True)
        first = jnp.min(jnp.where(cand == mx, fi, float(K * K)), axis=0, keepdims=True)
        sel = fi == first
        return jnp.where(sel, NEG, cand), jnp.where(sel, 1.0, sel_all)

    _, sel = lax.fori_loop(0, K, body, (cand, jnp.zeros((ncand, t), F32)))
    ea = jnp.exp(a - a[0:1])
    eb = jnp.exp(b - b[0:1])
    cnt = jnp.zeros((K, t), F32)
    zsum = jnp.zeros((1, t), F32)
    off = 0
    for c, rows, valid in blocks:
        blk = sel[off:off + rows]
        off += rows
        if rows < K:
            blk_full = jnp.concatenate([blk, jnp.zeros((K - rows, t), F32)], axis=0)
        else:
            blk_full = blk
        cnt = cnt + blk_full
        zsum = zsum + jnp.sum(blk * ea[0:rows], axis=0, keepdims=True) * eb[c:c + 1]
    tail = sel[off:off + 8]
    tcount = jnp.sum(tail, axis=0, keepdims=True)
    row0 = lax.broadcasted_iota(jnp.int32, (K, t), 0) == 0
    cnt = cnt + jnp.where(row0, tcount, 0.0)
    zsum = zsum + jnp.sum(tail * eb[K // 2:K], axis=0, keepdims=True)
    c1 = jnp.zeros((PEER_KEYS, t), F32)
    for r in range(K):
        c1 = jnp.where(rank1 == r, cnt[r:r + 1], c1)
    e1 = jnp.where(rank1 < K, jnp.exp(s1 - a[0:1]), 0.0) * (0.5 / zsum)
    e2 = jnp.where(rank2 < K, jnp.exp(s2 - b[0:1]), 0.0)
    return rank2, e2, c1, e1


def _sort_network(n):
    def merge(lo, hi, r):
        step = r * 2
        if step < hi - lo:
            yield from merge(lo, hi, step)
            yield from merge(lo + r, hi, step)
            for i in range(lo + r, hi - r, step):
                yield (i, i + r)
        else:
            yield (lo, lo + r)

    def sort(lo, hi):
        if hi - lo >= 1:
            mid = lo + (hi - lo) // 2
            yield from sort(lo, mid)
            yield from sort(mid + 1, hi)
            yield from merge(lo, hi, 1)

    return list(sort(0, n - 1))


def _exchange(xs, i, j):
    hi = jnp.maximum(xs[i], xs[j])
    xs[j] = jnp.minimum(xs[i], xs[j])
    xs[i] = hi


def _bitonic_finish(xs):
    n = len(xs)
    d = n // 2
    while d >= 1:
        for i in range(n):
            if i & d == 0:
                _exchange(xs, i, i + d)
        d //= 2


def _merge_sublanes(xs):
    n = len(xs)
    for shift in (4, 2, 1):
        other = [pltpu.roll(x, shift, 0) for x in xs]
        xs = [jnp.maximum(xs[i], other[n - 1 - i]) for i in range(n)]
        _bitonic_finish(xs)
    return xs


def _top_sorted(s):
    xs = [s[8 * v:8 * v + 8] for v in range(s.shape[0] // 8)]
    for i, j in _sort_network(len(xs)):
        _exchange(xs, i, j)
    return _merge_sublanes(xs)


def _route_head_sorted(s1, s2):
    K = PEER_TOPK
    t = s1.shape[1]
    a = _top_sorted(s1)
    b = _top_sorted(s2)
    sub = lax.broadcasted_iota(jnp.int32, (8, t), 0)
    a8 = a[7]
    for r in range(6, -1, -1):
        a8 = jnp.where(sub == r, a[r], a8)
    main = [jnp.where((sub + 1) * (c + 1) <= K, a8 + b[c], NEG) for c in range(K)]
    top = _merge_sublanes(list(main))
    single = [a[8 + i] + b[0] for i in range(K // 2)] + [jnp.full((8, t), NEG, F32)] * (K // 2)
    top = [jnp.maximum(top[i], single[K - 1 - i]) for i in range(K)]
    _bitonic_finish(top)
    tau = top[K - 1]
    ea8 = jnp.exp(a8 - a[0])
    cnt8 = jnp.zeros((8, t), F32)
    z8 = jnp.zeros((8, t), F32)
    for c in range(K):
        hit = main[c] >= tau
        cnt8 = cnt8 + jnp.where(hit, 1.0, 0.0)
        z8 = z8 + jnp.where(hit, ea8 * jnp.exp(b[c] - b[0]), 0.0)
    cnt = [jnp.broadcast_to(cnt8[r:r + 1], (8, t)) for r in range(8)]
    zsum = jnp.sum(z8, axis=0, keepdims=True)
    total = jnp.sum(cnt8, axis=0, keepdims=True)
    for i in range(K // 2):
        hit = jnp.where(single[i] >= tau, 1.0, 0.0)
        cnt.append(hit)
        zsum = zsum + hit[0:1] * jnp.exp(a[8 + i][0:1] - a[0][0:1])
        total = total + hit[0:1]
    in1 = jnp.zeros((8, t), F32)
    in2 = jnp.zeros((8, t), F32)
    c1s, e1s, r2s, e2s = [], [], [], []
    scale = 0.5 / zsum
    for v in range(PEER_KEYS // 8):
        x1 = s1[8 * v:8 * v + 8]
        x2 = s2[8 * v:8 * v + 8]
        c1 = jnp.zeros((8, t), F32)
        r2 = jnp.full((8, t), float(K), F32)
        for r in range(K):
            c1 = jnp.where(x1 == a[r], cnt[r], c1)
            r2 = jnp.where(x2 == b[r], float(r), r2)
        top1 = x1 >= a[K - 1]
        top2 = x2 >= b[K - 1]
        in1 = in1 + jnp.where(top1, 1.0, 0.0)
        in2 = in2 + jnp.where(top2, 1.0, 0.0)
        c1s.append(c1)
        r2s.append(r2)
        e1s.append(jnp.where(top1, jnp.exp(x1 - a[0]), 0.0) * scale)
        e2s.append(jnp.where(top2, jnp.exp(x2 - b[0]), 0.0))
    in1 = jnp.sum(in1, axis=0, keepdims=True)
    in2 = jnp.sum(in2, axis=0, keepdims=True)
    bad = (in1 != float(K)) | (in2 != float(K)) | (total != float(K))
    for r in range(K - 1):
        bad = bad | (a[r][0:1] == a[r + 1][0:1]) | (b[r][0:1] == b[r + 1][0:1])
    cat = lambda xs: jnp.concatenate(xs, axis=0)
    return cat(r2s), cat(e2s), cat(c1s), cat(e1s), jnp.where(bad, 1.0, 0.0)


def _route_kernel(qp_ref, kb_ref, r2_ref, e2_ref, c1_ref, e1_ref):
    t = qp_ref.shape[0]
    for hh in range(PEER_HEADS):
        q = qp_ref[:, hh * PEER_DKEY:(hh + 1) * PEER_DKEY]
        st = _dot3(kb_ref[hh], q, _dot_nt)
        s1 = st[:PEER_KEYS]
        s2 = st[PEER_KEYS:]
        r2, e2, c1, e1, bad = _route_head_sorted(s1, s2)
        r2_ref[hh] = r2.astype(BF16)
        e2_ref[hh] = e2.astype(BF16)
        c1_ref[hh] = c1
        e1_ref[hh] = e1

        @pl.when(jnp.max(bad) > 0.0)
        def _():
            for sb in range(t // LANES):
                lanes = slice(sb * LANES, (sb + 1) * LANES)

                @pl.when(jnp.max(bad[:, lanes]) > 0.0)
                def _():
                    r2, e2, c1, e1 = _route_head_iterative(s1[:, lanes], s2[:, lanes], _cand_index(LANES))
                    r2_ref[hh, :, lanes] = r2.astype(BF16)
                    e2_ref[hh, :, lanes] = e2.astype(BF16)
                    c1_ref[hh, :, lanes] = c1
                    e1_ref[hh, :, lanes] = e1


def _route_call(qp, kb, tn):
    T = qp.shape[0]
    out = pl.BlockSpec((PEER_HEADS, PEER_KEYS, tn), lambda i: (0, 0, i))
    shp = jax.ShapeDtypeStruct((PEER_HEADS, PEER_KEYS, T), F32)
    shp16 = jax.ShapeDtypeStruct((PEER_HEADS, PEER_KEYS, T), BF16)
    return pl.pallas_call(
        _route_kernel,
        grid=(T // tn,),
        in_specs=[pl.BlockSpec((tn, PEER_HEADS * PEER_DKEY), lambda i: (i, 0)),
                  pl.BlockSpec(kb.shape, lambda i: (0, 0, 0))],
        out_specs=[out, out, out, out],
        out_shape=[shp16, shp16, shp, shp],
        compiler_params=_params(("parallel",)),
        name="route",
    )(qp, kb)


PEER_SUB = 1024
ROWS16 = 16


def _gelu_half(x):
    k = math.sqrt(2.0 / math.pi)
    return x + x * jnp.tanh(x * (k + (k * 0.044715) * (x * x)))


def _peer_kernel(h2_ref, u_ref, v_ref, r2_ref, e2_ref, c1_ref, e1_ref, x1_ref, g2_ref, fg_ref, o_ref, acc_ref,
                 w_ref, at_ref):
    e = pl.program_id(1)
    ne = pl.num_programs(1)
    te = u_ref.shape[0]

    @pl.when(e == 0)
    def _():
        acc_ref[...] = jnp.zeros_like(acc_ref)

    zero = jnp.zeros((), BF16)
    nsub = h2_ref.shape[1] // PEER_SUB
    toks = [slice(th * PEER_SUB, (th + 1) * PEER_SUB) for th in range(nsub)]

    def gates(th):
        tok = toks[th]
        for ii in range(te // PEER_KEYS):
            gs = [None] * (PEER_KEYS // ROWS16)
            for hh in range(PEER_HEADS):
                c1 = jnp.broadcast_to(c1_ref[hh, ii:ii + 1, tok], (ROWS16, PEER_SUB)).astype(BF16)
                e1 = jnp.broadcast_to(e1_ref[hh, ii:ii + 1, tok], (ROWS16, PEER_SUB)).astype(BF16)
                for k in range(PEER_KEYS // ROWS16):
                    rows = slice(k * ROWS16, (k + 1) * ROWS16)
                    term = jnp.where(r2_ref[hh, rows, tok] < c1, e2_ref[hh, rows, tok], zero) * e1
                    gs[k] = term if gs[k] is None else gs[k] + term
            for k in range(PEER_KEYS // ROWS16):
                lo = ii * PEER_KEYS + k * ROWS16
                w_ref[th, lo:lo + ROWS16, :] = gs[k]

    def project(th):
        at_ref[th] = _dot(u_ref[...], h2_ref[:, toks[th]])

    def activate(th):
        w_ref[th] = w_ref[th] * _gelu_half(at_ref[th].astype(BF16))

    def combine(th):
        acc_ref[:, toks[th]] += _dot_tn(v_ref[...], w_ref[th])

    gates(0)
    project(0)
    for th in range(nsub):
        if th + 1 < nsub:
            gates(th + 1)
        activate(th)
        if th + 1 < nsub:
            project(th + 1)
        combine(th)

    @pl.when(e == ne - 1)
    def _():
        x2 = x1_ref[...] + g2_ref[0] * acc_ref[...].T
        y = x2 * lax.rsqrt(jnp.mean(x2 * x2, axis=-1, keepdims=True) + NORM_EPS) * fg_ref[...]
        o_ref[...] = y


def _peer_call(h2, u_tab, v_tab, r2, e2, c1, e1, x1, g2rows, fg, tn, te, toks_per_batch):
    D, T = h2.shape
    E = u_tab.shape[0]
    rows = te // PEER_KEYS
    tab = pl.BlockSpec((PEER_HEADS, PEER_KEYS, tn), lambda i, e: (0, 0, i))
    sel = pl.BlockSpec((PEER_HEADS, rows, tn), lambda i, e: (0, e, i))
    return pl.pallas_call(
        _peer_kernel,
        grid=(T // tn, E // te),
        in_specs=[pl.BlockSpec((D, tn), lambda i, e: (0, i)),
                  pl.BlockSpec((te, D), lambda i, e: (e, 0)),
                  pl.BlockSpec((te, D), lambda i, e: (e, 0)),
                  tab, tab, sel, sel,
                  pl.BlockSpec((tn, D), lambda i, e: (i, 0)),
                  pl.BlockSpec((1, 1, D), lambda i, e: ((i * tn) // toks_per_batch, 0, 0)),
                  pl.BlockSpec((1, D), lambda i, e: (0, 0))],
        out_specs=pl.BlockSpec((tn, D), lambda i, e: (i, 0)),
        out_shape=jax.ShapeDtypeStruct((T, D), F32),
        scratch_shapes=[pltpu.VMEM((D, tn), F32), pltpu.VMEM((tn // PEER_SUB, te, PEER_SUB), BF16),
                        pltpu.VMEM((tn // PEER_SUB, te, PEER_SUB), F32)],
        compiler_params=_params(("parallel", "arbitrary")),
        name="peer",
    )(h2, u_tab, v_tab, r2, e2, c1, e1, x1, g2rows, fg)


def _rope_tables(L):
    rows = L // GRID_W
    row = jnp.repeat(jnp.arange(rows, dtype=jnp.int32), GRID_W).astype(F32)
    col = jnp.tile(jnp.arange(GRID_W, dtype=jnp.int32), rows).astype(F32)
    f = HEAD_DIM // 4
    inv = ROPE_BASE ** (-jnp.arange(f, dtype=F32) / f)
    ang_r = row[:, None] * inv[None, :]
    ang_c = col[:, None] * inv[None, :]
    ang = jnp.concatenate([ang_r, ang_r, ang_c, ang_c], axis=-1)
    reps = ROT_W // HEAD_DIM
    return jnp.tile(jnp.cos(ang), (1, reps)), jnp.tile(jnp.sin(ang), (1, reps))


def _rot_cols(w):
    rows, n = w.shape
    f = HEAD_DIM // 4
    w4 = w.reshape(rows, n // (2 * f), 2, f)
    return jnp.stack([-w4[:, :, 1], w4[:, :, 0]], axis=2).reshape(rows, n)


def _dup_cols(w):
    return jnp.concatenate([w[:, :HEAD_DIM], w[:, :HEAD_DIM], w[:, HEAD_DIM:], w[:, HEAD_DIM:]], axis=1)


def _dft_table(L):
    N = 2 * L
    s = 64
    kh = jnp.arange(L // s, dtype=jnp.int32)
    kl = jnp.arange(s, dtype=jnp.int32)
    n = jnp.arange(L, dtype=jnp.int32)
    ph1 = ((kh[:, None] * s * n[None, :]) % N).astype(F32) * (2.0 * math.pi / N)
    ph2 = ((kl[:, None] * n[None, :]) % N).astype(F32) * (2.0 * math.pi / N)
    c1, s1 = jnp.cos(ph1)[:, None, :], jnp.sin(ph1)[:, None, :]
    c2, s2 = jnp.cos(ph2)[None, :, :], jnp.sin(ph2)[None, :, :]
    cosm = (c1 * c2 - s1 * s2).reshape(L, L)
    sinm = (s1 * c2 + c1 * s2).reshape(L, L)
    nyq = jnp.where(n % 2 == 0, 1.0, -1.0).astype(F32)
    k = jnp.arange(L, dtype=jnp.int32)
    q = jnp.where(k[:, None] == 0, nyq[None, :], -sinm)
    return cosm.astype(BF16), q.astype(BF16)


def _filter_features(L):
    t = jnp.arange(L, dtype=F32) / L
    bands = jnp.arange(1, HY_EMB_BANDS + 1, dtype=F32)
    ang = 2.0 * math.pi * t[:, None] * bands[None, :]
    z = jnp.concatenate([t[:, None], jnp.cos(ang), jnp.sin(ang)], axis=-1)
    return jnp.pad(z, ((0, 0), (0, LANES - z.shape[1])))


def _tile(n, pref):
    return pref if n % pref == 0 else n


def kernel(x, c, ctx, c_ctx, w_mod, b_mod, norm1_g, w_in, attn_sink, hy_conv_w, hy_conv_b, hy_fw1, hy_fb1, hy_fw2, hy_fb2, hy_fw3, hy_fb3, hy_freq, hy_skip, w_o_attn, w_o_hy, w_out, norm2_g, peer_wq, peer_keys, peer_u, peer_v, final_g):
    B, L, D = x.shape
    assert B == 4 and D == D_MODEL and w_mod.shape[0] == 1
    T = B * L
    li = 0

    c8 = jnp.concatenate([c, c_ctx[None, :], jnp.zeros((3, D), F32)], axis=0)
    mod3 = _mod_call(c8, w_mod[li], b_mod[li]).reshape(8, 6, D)
    g1 = norm1_g[li].reshape(1, D)
    g2 = norm2_g[li].reshape(1, D)

    w = w_in[li]
    wq, wk, wv = w[:, :OFF_K], _dup_cols(w[:, OFF_K:OFF_V]), _dup_cols(w[:, OFF_V:OFF_HY])
    w_cat = jnp.concatenate([wq, wk, wv, w[:, OFF_HY:OFF_G], _rot_cols(wq), _rot_cols(wk)], axis=1).astype(BF16)
    cos_t, sin_t = _rope_tables(L)
    q, k, v, hy = _inproj_call(x, mod3, g1, w_cat, cos_t, sin_t, _tile(L, 512))
    kx, vx = _ctxproj_call(ctx, mod3, g1, jnp.concatenate([wk, wv], axis=1).astype(BF16))

    gsz = N_HEADS // N_KV_HEADS
    sink_b = jnp.broadcast_to(
        jnp.repeat(attn_sink[li].astype(F32).reshape(N_KV_HEADS, gsz), BLOCK, axis=1)[:, :, None],
        (N_KV_HEADS, gsz * BLOCK, LANES))
    y_attn = _attn_call(q, k, v, kx, vx, sink_b)

    H = HY_FILTER_HIDDEN
    w1p = jnp.pad(hy_fw1[li], ((0, LANES - hy_fw1.shape[1]), (0, 0)))
    deltas = jnp.abs(jnp.linspace(math.log(HY_DECAY_TARGET) / HY_SLOW_DECAY,
                                  math.log(HY_DECAY_TARGET) / HY_FAST_DECAY, HY_WIDTH, dtype=F32)).reshape(1, -1)
    hs, hd = _filt_call(_filter_features(L), w1p, hy_fb1[li].reshape(1, H), hy_freq[li].reshape(1, H),
                        hy_fw2[li], hy_fb2[li].reshape(1, H), hy_fw3[li], hy_fb3[li].reshape(1, -1), deltas)
    u, x0c = _sconv_call(hy, hy_conv_w[li], hy_conv_b[li].reshape(1, -1))
    kk = jnp.arange(L // 2, dtype=F32)[:, None] * (math.pi / L)
    cw = jnp.broadcast_to(jnp.cos(kk), (L // 2, LANES))
    sw = jnp.broadcast_to(jnp.sin(kk), (L // 2, LANES))
    yc = _lconv_call(u, hs, hd, _dft_table(L // 2), cw, sw, hy_skip[li].reshape(1, -1), _tile(L // 2, 256))

    x1, h2t, qp = _merge_call(x, mod3, g1, g2, y_attn, yc, x0c, w[:, OFF_G:].astype(BF16),
                             w_o_attn[li].astype(BF16), w_o_hy[li].astype(BF16), w_out[li].astype(BF16),
                             peer_wq[li].astype(BF16), _tile(L, 512))

    keys = peer_keys[li]
    zk = jnp.zeros_like(keys[:, 0])
    kb = jnp.concatenate([jnp.concatenate([keys[:, 0], zk], axis=2),
                          jnp.concatenate([zk, keys[:, 1]], axis=2)], axis=1)
    r2, e2, c1, e1 = _route_call(qp.reshape(T, D), kb, _tile(T, 512))

    out = _peer_call(h2t, peer_u[li].astype(BF16), peer_v[li].astype(BF16), r2, e2, c1, e1,
                     x1.reshape(T, D), mod3[:B, 5:6, :], final_g.reshape(1, D),
                     _tile(T, 1024), 1024, L)
    return out.reshape(B, L, D)
```

```python
import functools
import math

import jax
import jax.numpy as jnp
from jax import lax
from jax.experimental import pallas as pl
from jax.experimental.pallas import tpu as pltpu

F32 = jnp.float32
BF16 = jnp.bfloat16

D_MODEL = 1024
GRID_W = 64
NORM_EPS = 1e-6
N_HEADS = 8
N_KV_HEADS = 2
HEAD_DIM = 64
BLOCK = 128
ROPE_BASE = 10000.0
HY_WIDTH = 512
HY_EMB_BANDS = 16
HY_FILTER_HIDDEN = 64
HY_FAST_DECAY = 0.3
HY_SLOW_DECAY = 1.5
HY_DECAY_TARGET = 1e-2
PEER_HEADS = 8
PEER_KEYS = 128
PEER_TOPK = 16
PEER_DKEY = 128
Q_W = N_HEADS * HEAD_DIM
KV_W = N_KV_HEADS * HEAD_DIM
HY_IN = 3 * HY_WIDTH
OFF_K = Q_W
OFF_V = OFF_K + KV_W
OFF_HY = OFF_V + KV_W
OFF_G = OFF_HY + HY_IN

LANES = 128
VMEM_LIMIT = 56 * 1024 * 1024
NEG = -1e30
KV_DUP = 2 * KV_W


def _params(sem):
    return pltpu.CompilerParams(dimension_semantics=sem, vmem_limit_bytes=VMEM_LIMIT)


def _dot(a, b):
    return lax.dot_general(a, b, (((1,), (0,)), ((), ())), preferred_element_type=F32)


def _dot_nt(a, b):
    return lax.dot_general(a, b, (((1,), (1,)), ((), ())), preferred_element_type=F32)


def _dot_tn(a, b):
    return lax.dot_general(a, b, (((0,), (0,)), ((), ())), preferred_element_type=F32)


def _split(a):
    hi = a.astype(BF16)
    lo = (a - hi.astype(F32)).astype(BF16)
    return hi, lo


def _dot3(a, b, dot=_dot):
    ah, al = _split(a)
    bh, bl = _split(b)
    return dot(ah, bh) + dot(ah, bl) + dot(al, bh)


def _norm_mod(x, g, shift, scale):
    y = x * lax.rsqrt(jnp.mean(x * x, axis=-1, keepdims=True) + NORM_EPS) * g
    return y * (1.0 + scale) + shift


def _mod_kernel(c_ref, w_ref, b_ref, o_ref):
    c = c_ref[...]
    s = c * jax.nn.sigmoid(c)
    o_ref[...] = _dot3(s, w_ref[...]) + b_ref[...]


def _mod_call(c8, w_mod, b_mod):
    n = w_mod.shape[1] // D_MODEL
    return pl.pallas_call(
        _mod_kernel,
        grid=(n,),
        in_specs=[pl.BlockSpec((8, D_MODEL), lambda j: (0, 0)),
                  pl.BlockSpec((D_MODEL, D_MODEL), lambda j: (0, j)),
                  pl.BlockSpec((1, D_MODEL), lambda j: (0, j))],
        out_specs=pl.BlockSpec((8, D_MODEL), lambda j: (0, j)),
        out_shape=jax.ShapeDtypeStruct((8, w_mod.shape[1]), F32),
        compiler_params=_params(("arbitrary",)),
        name="mod",
    )(c8, w_mod, b_mod.reshape(1, -1))


ROT_W = Q_W + KV_DUP
CAT_W = Q_W + 2 * KV_DUP + HY_IN + ROT_W


def _inproj_kernel(x_ref, mod_ref, g_ref, w_ref, cos_ref, sin_ref, q_ref, k_ref, v_ref, hy_ref):
    m = mod_ref[0]
    h = _norm_mod(x_ref[0], g_ref[...], m[0:1], m[1:2]).astype(BF16)
    o_v = ROT_W
    o_hy = o_v + KV_DUP
    o_rot = o_hy + HY_IN
    p = _dot(h, w_ref[:, 0:ROT_W])
    pr = _dot(h, w_ref[:, o_rot:o_rot + ROT_W])
    qk = p * cos_ref[...] + pr * sin_ref[...]
    q_ref[0] = qk[:, :Q_W].astype(BF16)
    k_ref[0] = qk[:, Q_W:].astype(BF16)
    v_ref[0] = _dot(h, w_ref[:, o_v:o_hy]).astype(BF16)
    for j in range(HY_IN // HY_WIDTH):
        lo = o_hy + j * HY_WIDTH
        hy_ref[0, :, j * HY_WIDTH:(j + 1) * HY_WIDTH] = _dot(h, w_ref[:, lo:lo + HY_WIDTH]).astype(BF16)


def _inproj_call(x, mod3, g, w_cat, cos_t, sin_t, tm):
    B, L, D = x.shape
    return pl.pallas_call(
        _inproj_kernel,
        grid=(B, L // tm),
        in_specs=[pl.BlockSpec((1, tm, D), lambda b, i: (b, i, 0)),
                  pl.BlockSpec((1, 6, D), lambda b, i: (b, 0, 0)),
                  pl.BlockSpec((1, D), lambda b, i: (0, 0)),
                  pl.BlockSpec((D, CAT_W), lambda b, i: (0, 0)),
                  pl.BlockSpec((tm, ROT_W), lambda b, i: (i, 0)),
                  pl.BlockSpec((tm, ROT_W), lambda b, i: (i, 0))],
        out_specs=[pl.BlockSpec((1, tm, Q_W), lambda b, i: (b, i, 0)),
                   pl.BlockSpec((1, tm, KV_DUP), lambda b, i: (b, i, 0)),
                   pl.BlockSpec((1, tm, KV_DUP), lambda b, i: (b, i, 0)),
                   pl.BlockSpec((1, tm, HY_IN), lambda b, i: (b, i, 0))],
        out_shape=[jax.ShapeDtypeStruct((B, L, Q_W), BF16),
                   jax.ShapeDtypeStruct((B, L, KV_DUP), BF16),
                   jax.ShapeDtypeStruct((B, L, KV_DUP), BF16),
                   jax.ShapeDtypeStruct((B, L, HY_IN), BF16)],
        compiler_params=_params(("parallel", "arbitrary")),
        name="inproj",
    )(x, mod3, g, w_cat, cos_t, sin_t)


def _ctxproj_kernel(x_ref, mod_ref, g_ref, w_ref, k_ref, v_ref):
    m = mod_ref[0]
    h = _norm_mod(x_ref[0], g_ref[...], m[0:1], m[1:2]).astype(BF16)
    k_ref[0] = _dot(h, w_ref[:, :KV_DUP]).astype(BF16)
    v_ref[0] = _dot(h, w_ref[:, KV_DUP:]).astype(BF16)


def _ctxproj_call(ctx, mod3, g, w_kv):
    B, C, D = ctx.shape
    return pl.pallas_call(
        _ctxproj_kernel,
        grid=(B,),
        in_specs=[pl.BlockSpec((1, C, D), lambda b: (b, 0, 0)),
                  pl.BlockSpec((1, 6, D), lambda b: (4, 0, 0)),
                  pl.BlockSpec((1, D), lambda b: (0, 0)),
                  pl.BlockSpec((D, 2 * KV_DUP), lambda b: (0, 0))],
        out_specs=[pl.BlockSpec((1, C, KV_DUP), lambda b: (b, 0, 0)),
                   pl.BlockSpec((1, C, KV_DUP), lambda b: (b, 0, 0))],
        out_shape=[jax.ShapeDtypeStruct((B, C, KV_DUP), BF16),
                   jax.ShapeDtypeStruct((B, C, KV_DUP), BF16)],
        compiler_params=_params(("arbitrary",)),
        name="ctxproj",
    )(ctx, mod3, g, w_kv)


def _attn_kernel(q_ref, kp_ref, kc_ref, kn_ref, vp_ref, vc_ref, vn_ref, kx_ref, vx_ref, sink_ref, o_ref):
    n = pl.program_id(1)
    nb = pl.num_programs(1)
    q = q_ref[0]
    rows = (N_HEADS // N_KV_HEADS) * BLOCK
    lo = lax.broadcasted_iota(jnp.int32, (BLOCK, LANES), 1) < HEAD_DIM
    r = lax.broadcasted_iota(jnp.int32, (rows, BLOCK), 0) % BLOCK
    c = lax.broadcasted_iota(jnp.int32, (rows, BLOCK), 1)
    ok_prev = jnp.logical_and(c >= r, n > 0)
    ok_next = jnp.logical_and(c <= r, n < nb - 1)
    scale = HEAD_DIM ** -0.5
    zero = jnp.zeros((BLOCK, LANES), BF16)
    for g in range(N_KV_HEADS):
        sl = slice(g * LANES, (g + 1) * LANES)
        qa = q[:, 2 * g * LANES:(2 * g + 1) * LANES]
        qb = q[:, (2 * g + 1) * LANES:(2 * g + 2) * LANES]
        lhs = jnp.concatenate([jnp.where(lo, qa, zero), jnp.where(lo, zero, qa),
                               jnp.where(lo, qb, zero), jnp.where(lo, zero, qb)], axis=0)
        s_p = jnp.where(ok_prev, _dot_nt(lhs, kp_ref[0, :, sl]) * scale, NEG)
        s_c = _dot_nt(lhs, kc_ref[0, :, sl]) * scale
        s_n = jnp.where(ok_next, _dot_nt(lhs, kn_ref[0, :, sl]) * scale, NEG)
        s_x = _dot_nt(lhs, kx_ref[0, :, sl]) * scale
        sink = sink_ref[g][:, 0:1]
        half = s_x.shape[1] // 2
        m = jnp.max(jnp.maximum(jnp.maximum(s_p, s_c), jnp.maximum(s_n, jnp.maximum(s_x[:, :half], s_x[:, half:]))),
                    axis=-1, keepdims=True)
        m = jnp.maximum(m, sink)
        p_p = jnp.exp(s_p - m)
        p_c = jnp.exp(s_c - m)
        p_n = jnp.exp(s_n - m)
        p_x = jnp.exp(s_x - m)
        den = (jnp.sum((p_p + p_c) + (p_n + (p_x[:, :half] + p_x[:, half:])), axis=-1, keepdims=True)
               + jnp.exp(sink - m))
        o = (_dot(p_p.astype(BF16), vp_ref[0, :, sl]) + _dot(p_c.astype(BF16), vc_ref[0, :, sl])
             + _dot(p_n.astype(BF16), vn_ref[0, :, sl]) + _dot(p_x.astype(BF16), vx_ref[0, :, sl]))
        o = o / den
        o_ref[0, :, 2 * g * LANES:(2 * g + 1) * LANES] = jnp.where(
            lo, o[0:BLOCK], o[BLOCK:2 * BLOCK]).astype(BF16)
        o_ref[0, :, (2 * g + 1) * LANES:(2 * g + 2) * LANES] = jnp.where(
            lo, o[2 * BLOCK:3 * BLOCK], o[3 * BLOCK:4 * BLOCK]).astype(BF16)


def _attn_call(q, k, v, kx, vx, sink_b):
    B, L, _ = q.shape
    C = kx.shape[1]
    nb = L // BLOCK
    kv = lambda f: pl.BlockSpec((1, BLOCK, KV_DUP), f)
    prev = lambda b, n: (b, jnp.maximum(n - 1, 0), 0)
    cur = lambda b, n: (b, n, 0)
    nxt = lambda b, n: (b, jnp.minimum(n + 1, nb - 1), 0)
    rows = (N_HEADS // N_KV_HEADS) * BLOCK
    return pl.pallas_call(
        _attn_kernel,
        grid=(B, nb),
        in_specs=[pl.BlockSpec((1, BLOCK, Q_W), cur),
                  kv(prev), kv(cur), kv(nxt), kv(prev), kv(cur), kv(nxt),
                  pl.BlockSpec((1, C, KV_DUP), lambda b, n: (b, 0, 0)),
                  pl.BlockSpec((1, C, KV_DUP), lambda b, n: (b, 0, 0)),
                  pl.BlockSpec((N_KV_HEADS, rows, LANES), lambda b, n: (0, 0, 0))],
        out_specs=pl.BlockSpec((1, BLOCK, Q_W), cur),
        out_shape=jax.ShapeDtypeStruct((B, L, Q_W), BF16),
        compiler_params=_params(("parallel", "arbitrary")),
        name="attn",
    )(q, k, k, k, v, v, v, kx, vx, sink_b)


def _filt_kernel(z_ref, w1_ref, b1_ref, fr_ref, w2_ref, b2_ref, w3f_ref, w3b_ref, b3f_ref, b3b_ref,
                 dl_ref, hs_ref, hd_ref, h_ref, par_ref):
    L = z_ref.shape[0]
    z = z_ref[...]

    @pl.when(pl.program_id(0) == 0)
    def _():
        fr = fr_ref[...]
        h1 = jnp.sin(fr * (_dot3(z, w1_ref[...]) + b1_ref[...]))
        h_ref[...] = jnp.sin(fr * (_dot3(h1, w2_ref[...]) + b2_ref[...]))

    h = h_ref[...]
    decay = jnp.exp(-z[:, 0:1] * dl_ref[...])
    hf = (_dot3(h, w3f_ref[...]) + b3f_ref[...]) * decay
    hb = (_dot3(h, w3b_ref[...]) + b3b_ref[...]) * decay
    row = lax.broadcasted_iota(jnp.int32, hb.shape, 0)
    hb = jnp.where(row < L - 1, hb, 0.0)
    norm = jnp.sum(jnp.abs(hf), axis=0, keepdims=True) + jnp.sum(jnp.abs(hb), axis=0, keepdims=True)
    inv = 1.0 / norm
    hf = hf * inv
    hbs = jnp.where(row >= 1, pltpu.roll(hb, 1, 0), 0.0) * inv
    H = L // 2
    for out_ref, val in ((hs_ref, hf + hbs), (hd_ref, hf - hbs)):
        par_ref[...] = val
        out_ref[0] = par_ref[pl.ds(0, H, stride=2), :].astype(BF16)
        out_ref[1] = par_ref[pl.ds(1, H, stride=2), :].astype(BF16)


def _filt_call(zf, w1p, b1, fr, w2, b2, w3, b3, absdelta):
    L = zf.shape[0]
    nct = HY_WIDTH // LANES
    H = HY_FILTER_HIDDEN
    full = lambda shape: pl.BlockSpec(shape, lambda j: (0, 0))
    return pl.pallas_call(
        _filt_kernel,
        grid=(nct,),
        in_specs=[full((L, LANES)), full((LANES, H)), full((1, H)), full((1, H)), full((H, H)), full((1, H)),
                  pl.BlockSpec((H, LANES), lambda j: (0, j)),
                  pl.BlockSpec((H, LANES), lambda j: (0, nct + j)),
                  pl.BlockSpec((1, LANES), lambda j: (0, j)),
                  pl.BlockSpec((1, LANES), lambda j: (0, nct + j)),
                  pl.BlockSpec((1, LANES), lambda j: (0, j))],
        out_specs=[pl.BlockSpec((2, L // 2, LANES), lambda j: (0, 0, j)),
                   pl.BlockSpec((2, L // 2, LANES), lambda j: (0, 0, j))],
        out_shape=[jax.ShapeDtypeStruct((2, L // 2, HY_WIDTH), BF16),
                   jax.ShapeDtypeStruct((2, L // 2, HY_WIDTH), BF16)],
        scratch_shapes=[pltpu.VMEM((L, H), F32), pltpu.VMEM((L, LANES), F32)],
        compiler_params=_params(("arbitrary",)),
        name="filt",
    )(zf, w1p, b1, fr, w2, b2, w3, w3, b3, b3, absdelta)


def _sconv_kernel(x0_ref, x1_ref, v_ref, w0_ref, w1_ref, w2_ref, b0_ref, b1_ref, b2_ref, u_ref, g_ref, par_ref):
    L = x0_ref.shape[1]
    row = lax.broadcasted_iota(jnp.int32, (L, LANES), 0)

    def conv(z_ref, w_ref, b_ref):
        z = z_ref[0].astype(F32)
        w = w_ref[...]
        zp = jnp.where(row >= 1, pltpu.roll(z, 1, 0), 0.0)
        zn = jnp.where(row < L - 1, pltpu.roll(z, L - 1, 0), 0.0)
        return zp * w[0:1] + z * w[1:2] + zn * w[2:3] + b_ref[...]

    g_ref[0] = conv(x0_ref, w0_ref, b0_ref).astype(BF16)
    par_ref[...] = conv(v_ref, w2_ref, b2_ref) * conv(x1_ref, w1_ref, b1_ref)
    u_ref[0, 0] = par_ref[pl.ds(0, L // 2, stride=2), :].astype(BF16)
    u_ref[0, 1] = par_ref[pl.ds(1, L // 2, stride=2), :].astype(BF16)


def _sconv_call(hy, conv_w, conv_b):
    B, L, _ = hy.shape
    nct = HY_WIDTH // LANES
    zs = lambda part: pl.BlockSpec((1, L, LANES), lambda b, j: (b, 0, part * nct + j))
    ws = lambda part: pl.BlockSpec((3, LANES), lambda b, j: (0, part * nct + j))
    bs = lambda part: pl.BlockSpec((1, LANES), lambda b, j: (0, part * nct + j))
    out = pl.BlockSpec((1, L, LANES), lambda b, j: (b, 0, j))
    return pl.pallas_call(
        _sconv_kernel,
        grid=(B, nct),
        in_specs=[zs(0), zs(1), zs(2), ws(0), ws(1), ws(2), bs(0), bs(1), bs(2)],
        out_specs=[pl.BlockSpec((1, 2, L // 2, LANES), lambda b, j: (b, 0, 0, j)), out],
        out_shape=[jax.ShapeDtypeStruct((B, 2, L // 2, HY_WIDTH), BF16),
                   jax.ShapeDtypeStruct((B, L, HY_WIDTH), BF16)],
        scratch_shapes=[pltpu.VMEM((L, LANES), F32)],
        compiler_params=_params(("parallel", "arbitrary")),
        name="sconv",
    )(hy, hy, hy, conv_w, conv_w, conv_w, conv_b, conv_b, conv_b)


def _lconv_kernel(u_ref, hs_ref, hd_ref, fp_ref, fq_ref, cw_ref, sw_ref, skip_ref, y_ref, rhs_ref, acce_ref, acco_ref,
                  par_ref):
    nbat = u_ref.shape[0]
    H = u_ref.shape[2]
    j = pl.program_id(1)
    nj = pl.num_programs(1)
    tk = fp_ref.shape[0]
    W = nbat * LANES
    blk = lambda i: slice(i * LANES, (i + 1) * LANES)
    HS_E, HS_O, HD_E, HD_O = 2 * nbat, 2 * nbat + 1, 2 * nbat + 2, 2 * nbat + 3

    @pl.when(j == 0)
    def _():
        for b in range(nbat):
            rhs_ref[:, blk(b)] = u_ref[b, 0]
            rhs_ref[:, blk(nbat + b)] = u_ref[b, 1]
        rhs_ref[:, blk(HS_E)] = hs_ref[0]
        rhs_ref[:, blk(HS_O)] = hs_ref[1]
        rhs_ref[:, blk(HD_E)] = hd_ref[0]
        rhs_ref[:, blk(HD_O)] = hd_ref[1]
        acce_ref[...] = jnp.zeros_like(acce_ref)
        acco_ref[...] = jnp.zeros_like(acco_ref)

    fp = fp_ref[...]
    fq = fq_ref[...]
    ap = _dot(fp, rhs_ref[...])
    aq = _dot(fq, rhs_ref[...])
    first = jnp.logical_and(lax.broadcasted_iota(jnp.int32, (tk, LANES), 0) == 0, j == 0)
    cw = cw_ref[...]
    sw = sw_ref[...]
    csum = lambda i: ap[:, blk(i)]
    ssum = lambda i: jnp.where(first, 0.0, -aq[:, blk(i)])
    alt = lambda i: aq[:, blk(i)]

    gs = cw * csum(HS_O) - sw * ssum(HS_O)
    hdd = sw * csum(HD_O) + cw * ssum(HD_O)
    kr, ki = csum(HS_E) + gs, -(ssum(HD_E) + hdd)
    kr2, ki2 = csum(HS_E) - gs, ssum(HD_E) - hdd
    wgt = jnp.where(first, 0.25 / H, 0.5 / H)
    outs = [[], [], [], []]
    for b in range(nbat):
        a, bb, c, d = csum(b), ssum(b), csum(nbat + b), ssum(nbat + b)
        g = cw * c - sw * d
        h = sw * c + cw * d
        xr, xi = a + g, -(bb + h)
        xr2, xi2 = a - g, bb - h
        ar, ai = xr * kr - xi * ki, xr * ki + xi * kr
        br, bi = xr2 * kr2 - xi2 * ki2, xr2 * ki2 + xi2 * kr2
        yer, yei = ar + br, ai - bi
        dr, di = ar - br, ai + bi
        yor, yoi = dr * cw - di * sw, dr * sw + di * cw
        yei = jnp.where(first, 2.0 * (alt(b) * alt(HS_E) - alt(nbat + b) * alt(HD_O)), yei)
        yoi = jnp.where(first, 2.0 * (alt(b) * alt(HD_O) + alt(nbat + b) * alt(HS_E)), yoi)
        for lst, val in zip(outs, (yer, yei, yor, yoi)):
            lst.append((val * wgt).astype(BF16))
    yer, yei, yor, yoi = [jnp.concatenate(x, axis=1) for x in outs]
    acce_ref[...] += _dot_tn(fp, yer) + _dot_tn(fq, yei)
    acco_ref[...] += _dot_tn(fp, yor) + _dot_tn(fq, yoi)

    @pl.when(j == nj - 1)
    def _():
        for b in range(nbat):
            par_ref[pl.ds(0, H, stride=2), :] = acce_ref[:, blk(b)] + u_ref[b, 0].astype(F32) * skip_ref[...]
            par_ref[pl.ds(1, H, stride=2), :] = acco_ref[:, blk(b)] + u_ref[b, 1].astype(F32) * skip_ref[...]
            y_ref[b] = par_ref[...].astype(BF16)


def _lconv_call(u, hs, hd, ftab, cw, sw, skip, tk):
    B, _, H, _ = u.shape
    nct = HY_WIDTH // LANES
    return pl.pallas_call(
        _lconv_kernel,
        grid=(nct, H // tk),
        in_specs=[pl.BlockSpec((B, 2, H, LANES), lambda c, j: (0, 0, 0, c)),
                  pl.BlockSpec((2, H, LANES), lambda c, j: (0, 0, c)),
                  pl.BlockSpec((2, H, LANES), lambda c, j: (0, 0, c)),
                  pl.BlockSpec((tk, H), lambda c, j: (j, 0)),
                  pl.BlockSpec((tk, H), lambda c, j: (j, 0)),
                  pl.BlockSpec((tk, LANES), lambda c, j: (j, 0)),
                  pl.BlockSpec((tk, LANES), lambda c, j: (j, 0)),
                  pl.BlockSpec((1, LANES), lambda c, j: (0, c))],
        out_specs=pl.BlockSpec((B, 2 * H, LANES), lambda c, j: (0, 0, c)),
        out_shape=jax.ShapeDtypeStruct((B, 2 * H, HY_WIDTH), BF16),
        scratch_shapes=[pltpu.VMEM((H, (2 * B + 4) * LANES), BF16),
                        pltpu.VMEM((H, B * LANES), F32),
                        pltpu.VMEM((H, B * LANES), F32),
                        pltpu.VMEM((2 * H, LANES), F32)],
        compiler_params=_params(("parallel", "arbitrary")),
        name="lconv",
    )(u, hs, hd, ftab[0], ftab[1], cw, sw, skip)


def _merge_kernel(x_ref, mod_ref, g1_ref, g2_ref, ya_ref, yc_ref, x0_ref, wg_ref, woa_ref, woh_ref, wout_ref,
                  wq_ref, x1_ref, h2t_ref, qp_ref):
    m = mod_ref[0]
    x = x_ref[0]
    h = _norm_mod(x, g1_ref[...], m[0:1], m[1:2]).astype(BF16)
    D = x.shape[-1]
    gate_a = jax.nn.sigmoid(_dot(h, wg_ref[:, :D]))
    gate_h = jax.nn.sigmoid(_dot(h, wg_ref[:, D:]))
    y_hy = yc_ref[0] * x0_ref[0]
    merged = gate_a * _dot(ya_ref[0], woa_ref[...]) + gate_h * _dot(y_hy, woh_ref[...])
    x1 = x + m[2:3] * _dot(merged.astype(BF16), wout_ref[...])
    x1_ref[0] = x1
    h2 = _norm_mod(x1, g2_ref[...], m[3:4], m[4:5])
    h2t_ref[...] = h2.T.astype(BF16)
    qp_ref[0] = _dot(h2.astype(BF16), wq_ref[...])


def _merge_call(x, mod3, g1, g2, ya, yc, x0c, wg, woa, woh, wout, wq, tm):
    B, L, D = x.shape
    tok = lambda w: pl.BlockSpec((1, tm, w), lambda b, i: (b, i, 0))
    full = lambda a: pl.BlockSpec(a.shape, lambda b, i: (0, 0))
    return pl.pallas_call(
        _merge_kernel,
        grid=(B, L // tm),
        in_specs=[tok(D), pl.BlockSpec((1, 6, D), lambda b, i: (b, 0, 0)), full(g1), full(g2),
                  tok(Q_W), tok(HY_WIDTH), tok(HY_WIDTH), full(wg), full(woa), full(woh), full(wout), full(wq)],
        out_specs=[tok(D), pl.BlockSpec((D, tm), lambda b, i: (0, b * (L // tm) + i)), tok(D)],
        out_shape=[jax.ShapeDtypeStruct((B, L, D), F32),
                   jax.ShapeDtypeStruct((D, B * L), BF16),
                   jax.ShapeDtypeStruct((B, L, D), F32)],
        compiler_params=_params(("parallel", "arbitrary")),
        name="merge",
    )(x, mod3, g1, g2, ya, yc, x0c, wg, woa, woh, wout, wq)


def _topk_rank(s):
    n, t = s.shape
    rowi = lax.broadcasted_iota(jnp.int32, (n, t), 0).astype(F32)
    topi = lax.broadcasted_iota(jnp.int32, (PEER_TOPK, t), 0)

    def body(it, carry):
        s, rank, vals = carry
        mx = jnp.max(s, axis=0, keepdims=True)
        first = jnp.min(jnp.where(s == mx, rowi, float(n)), axis=0, keepdims=True)
        sel = rowi == first
        return (jnp.where(sel, NEG, s), jnp.where(sel, jnp.asarray(it, F32), rank), jnp.where(topi == it, mx, vals))

    init = (s, jnp.full((n, t), float(PEER_TOPK), F32), jnp.zeros((PEER_TOPK, t), F32))
    _, rank, vals = lax.fori_loop(0, PEER_TOPK, body, init)
    return rank, vals


def _cand_blocks():
    blocks = []
    for c in range(PEER_TOPK // 2):
        valid = PEER_TOPK // (c + 1)
        blocks.append((c, -(-valid // 8) * 8, valid))
    return blocks


def _cand_index(t):
    K = PEER_TOPK
    fis = []
    for c, rows, valid in _cand_blocks():
        ri = lax.broadcasted_iota(jnp.int32, (rows, t), 0)
        fis.append(jnp.where(ri < valid, ri * K + c, K * K))
    ci = lax.broadcasted_iota(jnp.int32, (8, t), 0)
    fis.append(ci + K // 2)
    return jnp.concatenate(fis, axis=0).astype(F32)


def _route_head_iterative(s1, s2, fi):
    K = PEER_TOPK
    t = s1.shape[1]
    blocks = _cand_blocks()
    ncand = fi.shape[0]
    rank1, a = _topk_rank(s1)
    rank2, b = _topk_rank(s2)
    cands = []
    for c, rows, valid in blocks:
        cands.append(a[0:rows] + b[c:c + 1])
    cands.append(a[0:1] + b[K // 2:K])
    cand = jnp.where(fi < K * K, jnp.concatenate(cands, axis=0), NEG)

    def body(it, carry):
        cand, sel_all = carry
        mx = jnp.max(cand, axis=0, keepdims=True)
        first = jnp.min(jnp.where(cand == mx, fi, float(K * K)), axis=0, keepdims=True)
        sel = fi == first
        return jnp.where(sel, NEG, cand), jnp.where(sel, 1.0, sel_all)

    _, sel = lax.fori_loop(0, K, body, (cand, jnp.zeros((ncand, t), F32)))
    ea = jnp.exp(a - a[0:1])
    eb = jnp.exp(b - b[0:1])
    cnt = jnp.zeros((K, t), F32)
    zsum = jnp.zeros((1, t), F32)
    off = 0
    for c, rows, valid in blocks:
        blk = sel[off:off + rows]
        off += rows
        if rows < K:
            blk_full = jnp.concatenate([blk, jnp.zeros((K - rows, t), F32)], axis=0)
        else:
            blk_full = blk
        cnt = cnt + blk_full
        zsum = zsum + jnp.sum(blk * ea[0:rows], axis=0, keepdims=True) * eb[c:c + 1]
    tail = sel[off:off + 8]
    tcount = jnp.sum(tail, axis=0, keepdims=True)
    row0 = lax.broadcasted_iota(jnp.int32, (K, t), 0) == 0
    cnt = cnt + jnp.where(row0, tcount, 0.0)
    zsum = zsum + jnp.sum(tail * eb[K // 2:K], axis=0, keepdims=True)
    c1 = jnp.zeros((PEER_KEYS, t), F32)
    for r in range(K):
        c1 = jnp.where(rank1 == r, cnt[r:r + 1], c1)
    e1 = jnp.where(rank1 < K, jnp.exp(s1 - a[0:1]), 0.0) * (0.5 / zsum)
    e2 = jnp.where(rank2 < K, jnp.exp(s2 - b[0:1]), 0.0)
    return rank2, e2, c1, e1


def _sort_network(n):
    def merge(lo, hi, r):
        step = r * 2
        if step < hi - lo:
            yield from merge(lo, hi, step)
            yield from merge(lo + r, hi, step)
            for i in range(lo + r, hi - r, step):
                yield (i, i + r)
        else:
            yield (lo, lo + r)

    def sort(lo, hi):
        if hi - lo >= 1:
            mid = lo + (hi - lo) // 2
            yield from sort(lo, mid)
            yield from sort(mid + 1, hi)
            yield from merge(lo, hi, 1)

    return list(sort(0, n - 1))


def _exchange(xs, i, j):
    hi = jnp.maximum(xs[i], xs[j])
    xs[j] = jnp.minimum(xs[i], xs[j])
    xs[i] = hi


def _bitonic_finish(xs):
    n = len(xs)
    d = n // 2
    while d >= 1:
        for i in range(n):
            if i & d == 0:
                _exchange(xs, i, i + d)
        d //= 2


def _merge_sublanes(xs):
    n = len(xs)
    for shift in (4, 2, 1):
        other = [pltpu.roll(x, shift, 0) for x in xs]
        xs = [jnp.maximum(xs[i], other[n - 1 - i]) for i in range(n)]
        _bitonic_finish(xs)
    return xs


def _top_sorted(s):
    xs = [s[8 * v:8 * v + 8] for v in range(s.shape[0] // 8)]
    for i, j in _sort_network(len(xs)):
        _exchange(xs, i, j)
    return _merge_sublanes(xs)


def _route_head_sorted(s1, s2):
    K = PEER_TOPK
    t = s1.shape[1]
    a = _top_sorted(s1)
    b = _top_sorted(s2)
    sub = lax.broadcasted_iota(jnp.int32, (8, t), 0)
    a8 = a[7]
    for r in range(6, -1, -1):
        a8 = jnp.where(sub == r, a[r], a8)
    main = [jnp.where((sub + 1) * (c + 1) <= K, a8 + b[c], NEG) for c in range(K)]
    top = _merge_sublanes(list(main))
    single = [a[8 + i] + b[0] for i in range(K // 2)] + [jnp.full((8, t), NEG, F32)] * (K // 2)
    top = [jnp.maximum(top[i], single[K - 1 - i]) for i in range(K)]
    _bitonic_finish(top)
    tau = top[K - 1]
    ea8 = jnp.exp(a8 - a[0])
    cnt8 = jnp.zeros((8, t), F32)
    z8 = jnp.zeros((8, t), F32)
    for c in range(K):
        hit = main[c] >= tau
        cnt8 = cnt8 + jnp.where(hit, 1.0, 0.0)
        z8 = z8 + jnp.where(hit, ea8 * jnp.exp(b[c] - b[0]), 0.0)
    cnt = [jnp.broadcast_to(cnt8[r:r + 1], (8, t)) for r in range(8)]
    zsum = jnp.sum(z8, axis=0, keepdims=True)
    total = jnp.sum(cnt8, axis=0, keepdims=True)
    for i in range(K // 2):
        hit = jnp.where(single[i] >= tau, 1.0, 0.0)
        cnt.append(hit)
        zsum = zsum + hit[0:1] * jnp.exp(a[8 + i][0:1] - a[0][0:1])
        total = total + hit[0:1]
    in1 = jnp.zeros((8, t), F32)
    in2 = jnp.zeros((8, t), F32)
    c1s, e1s, r2s, e2s = [], [], [], []
    scale = 0.5 / zsum
    for v in range(PEER_KEYS // 8):
        x1 = s1[8 * v:8 * v + 8]
        x2 = s2[8 * v:8 * v + 8]
        c1 = jnp.zeros((8, t), F32)
        r2 = jnp.full((8, t), float(K), F32)
        for r in range(K):
            c1 = jnp.where(x1 == a[r], cnt[r], c1)
            r2 = jnp.where(x2 == b[r], float(r), r2)
        top1 = x1 >= a[K - 1]
        top2 = x2 >= b[K - 1]
        in1 = in1 + jnp.where(top1, 1.0, 0.0)
        in2 = in2 + jnp.where(top2, 1.0, 0.0)
        c1s.append(c1)
        r2s.append(r2)
        e1s.append(jnp.where(top1, jnp.exp(x1 - a[0]), 0.0) * scale)
        e2s.append(jnp.where(top2, jnp.exp(x2 - b[0]), 0.0))
    in1 = jnp.sum(in1, axis=0, keepdims=True)
    in2 = jnp.sum(in2, axis=0, keepdims=True)
    bad = (in1 != float(K)) | (in2 != float(K)) | (total != float(K))
    for r in range(K - 1):
        bad = bad | (a[r][0:1] == a[r + 1][0:1]) | (b[r][0:1] == b[r + 1][0:1])
    cat = lambda xs: jnp.concatenate(xs, axis=0)
    return cat(r2s), cat(e2s), cat(c1s), cat(e1s), jnp.where(bad, 1.0, 0.0)


def _route_kernel(qp_ref, kb_ref, r2_ref, e2_ref, c1_ref, e1_ref):
    t = qp_ref.shape[0]
    for hh in range(PEER_HEADS):
        q = qp_ref[:, hh * PEER_DKEY:(hh + 1) * PEER_DKEY]
        st = _dot3(kb_ref[hh], q, _dot_nt)
        s1 = st[:PEER_KEYS]
        s2 = st[PEER_KEYS:]
        r2, e2, c1, e1, bad = _route_head_sorted(s1, s2)
        r2_ref[hh] = r2.astype(BF16)
        e2_ref[hh] = e2.astype(BF16)
        c1_ref[hh] = c1
        e1_ref[hh] = e1

        @pl.when(jnp.max(bad) > 0.0)
        def _():
            for sb in range(t // LANES):
                lanes = slice(sb * LANES, (sb + 1) * LANES)

                @pl.when(jnp.max(bad[:, lanes]) > 0.0)
                def _():
                    r2, e2, c1, e1 = _route_head_iterative(s1[:, lanes], s2[:, lanes], _cand_index(LANES))
                    r2_ref[hh, :, lanes] = r2.astype(BF16)
                    e2_ref[hh, :, lanes] = e2.astype(BF16)
                    c1_ref[hh, :, lanes] = c1
                    e1_ref[hh, :, lanes] = e1


def _route_call(qp, kb, tn):
    T = qp.shape[0]
    out = pl.BlockSpec((PEER_HEADS, PEER_KEYS, tn), lambda i: (0, 0, i))
    shp = jax.ShapeDtypeStruct((PEER_HEADS, PEER_KEYS, T), F32)
    shp16 = jax.ShapeDtypeStruct((PEER_HEADS, PEER_KEYS, T), BF16)
    return pl.pallas_call(
        _route_kernel,
        grid=(T // tn,),
        in_specs=[pl.BlockSpec((tn, PEER_HEADS * PEER_DKEY), lambda i: (i, 0)),
                  pl.BlockSpec(kb.shape, lambda i: (0, 0, 0))],
        out_specs=[out, out, out, out],
        out_shape=[shp16, shp16, shp, shp],
        compiler_params=_params(("parallel",)),
        name="route",
    )(qp, kb)


PEER_SUB = 1024
ROWS16 = 16
GATE_LANES = 256


def _gelu_half(x):
    k = math.sqrt(2.0 / math.pi)
    return x + x * jnp.tanh(x * (k + (k * 0.044715) * (x * x)))


def _peer_kernel(h2_ref, u_ref, v_ref, r2_ref, e2_ref, c1_ref, e1_ref, x1_ref, g2_ref, fg_ref, o_ref, acc_ref,
                 w_ref, at_ref):
    e = pl.program_id(1)
    ne = pl.num_programs(1)
    te = u_ref.shape[0]

    @pl.when(e == 0)
    def _():
        acc_ref[...] = jnp.zeros_like(acc_ref)

    zero = jnp.zeros((), BF16)
    nsub = h2_ref.shape[1] // PEER_SUB
    toks = [slice(th * PEER_SUB, (th + 1) * PEER_SUB) for th in range(nsub)]

    def gates(th):
        for ii in range(te // PEER_KEYS):
            for lc in range(PEER_SUB // GATE_LANES):
                lo_t = th * PEER_SUB + lc * GATE_LANES
                tok = slice(lo_t, lo_t + GATE_LANES)
                wl = slice(lc * GATE_LANES, (lc + 1) * GATE_LANES)
                gs = [None] * (PEER_KEYS // ROWS16)
                for hh in range(PEER_HEADS):
                    c1 = jnp.broadcast_to(c1_ref[hh, ii:ii + 1, tok], (ROWS16, GATE_LANES)).astype(BF16)
                    e1 = jnp.broadcast_to(e1_ref[hh, ii:ii + 1, tok], (ROWS16, GATE_LANES)).astype(BF16)
                    for k in range(PEER_KEYS // ROWS16):
                        rows = slice(k * ROWS16, (k + 1) * ROWS16)
                        term = jnp.where(r2_ref[hh, rows, tok] < c1, e2_ref[hh, rows, tok], zero) * e1
                        gs[k] = term if gs[k] is None else gs[k] + term
                for k in range(PEER_KEYS // ROWS16):
                    lo = ii * PEER_KEYS + k * ROWS16
                    w_ref[th, lo:lo + ROWS16, wl] = gs[k]

    def project(th):
        at_ref[th] = _dot(u_ref[...], h2_ref[:, toks[th]])

    def activate(th):
        w_ref[th] = w_ref[th] * _gelu_half(at_ref[th].astype(BF16))

    def combine(th):
        acc_ref[:, toks[th]] += _dot_tn(v_ref[...], w_ref[th])

    gates(0)
    project(0)
    for th in range(nsub):
        if th + 1 < nsub:
            gates(th + 1)
        activate(th)
        if th + 1 < nsub:
            project(th + 1)
        combine(th)

    @pl.when(e == ne - 1)
    def _():
        x2 = x1_ref[...] + g2_ref[0] * acc_ref[...].T
        y = x2 * lax.rsqrt(jnp.mean(x2 * x2, axis=-1, keepdims=True) + NORM_EPS) * fg_ref[...]
        o_ref[...] = y


def _peer_call(h2, u_tab, v_tab, r2, e2, c1, e1, x1, g2rows, fg, tn, te, toks_per_batch):
    D, T = h2.shape
    E = u_tab.shape[0]
    rows = te // PEER_KEYS
    tab = pl.BlockSpec((PEER_HEADS, PEER_KEYS, tn), lambda i, e: (0, 0, i))
    sel = pl.BlockSpec((PEER_HEADS, rows, tn), lambda i, e: (0, e, i))
    return pl.pallas_call(
        _peer_kernel,
        grid=(T // tn, E // te),
        in_specs=[pl.BlockSpec((D, tn), lambda i, e: (0, i)),
                  pl.BlockSpec((te, D), lambda i, e: (e, 0)),
                  pl.BlockSpec((te, D), lambda i, e: (e, 0)),
                  tab, tab, sel, sel,
                  pl.BlockSpec((tn, D), lambda i, e: (i, 0)),
                  pl.BlockSpec((1, 1, D), lambda i, e: ((i * tn) // toks_per_batch, 0, 0)),
                  pl.BlockSpec((1, D), lambda i, e: (0, 0))],
        out_specs=pl.BlockSpec((tn, D), lambda i, e: (i, 0)),
        out_shape=jax.ShapeDtypeStruct((T, D), F32),
        scratch_shapes=[pltpu.VMEM((D, tn), F32), pltpu.VMEM((tn // PEER_SUB, te, PEER_SUB), BF16),
                        pltpu.VMEM((tn // PEER_SUB, te, PEER_SUB), F32)],
        compiler_params=_params(("parallel", "arbitrary")),
        name="peer",
    )(h2, u_tab, v_tab, r2, e2, c1, e1, x1, g2rows, fg)


def _rope_tables(L):
    rows = L // GRID_W
    row = jnp.repeat(jnp.arange(rows, dtype=jnp.int32), GRID_W).astype(F32)
    col = jnp.tile(jnp.arange(GRID_W, dtype=jnp.int32), rows).astype(F32)
    f = HEAD_DIM // 4
    inv = ROPE_BASE ** (-jnp.arange(f, dtype=F32) / f)
    ang_r = row[:, None] * inv[None, :]
    ang_c = col[:, None] * inv[None, :]
    ang = jnp.concatenate([ang_r, ang_r, ang_c, ang_c], axis=-1)
    reps = ROT_W // HEAD_DIM
    return jnp.tile(jnp.cos(ang), (1, reps)), jnp.tile(jnp.sin(ang), (1, reps))


def _rot_cols(w):
    rows, n = w.shape
    f = HEAD_DIM // 4
    w4 = w.reshape(rows, n // (2 * f), 2, f)
    return jnp.stack([-w4[:, :, 1], w4[:, :, 0]], axis=2).reshape(rows, n)


def _dup_cols(w):
    return jnp.concatenate([w[:, :HEAD_DIM], w[:, :HEAD_DIM], w[:, HEAD_DIM:], w[:, HEAD_DIM:]], axis=1)


def _dft_table(L):
    N = 2 * L
    s = 64
    kh = jnp.arange(L // s, dtype=jnp.int32)
    kl = jnp.arange(s, dtype=jnp.int32)
    n = jnp.arange(L, dtype=jnp.int32)
    ph1 = ((kh[:, None] * s * n[None, :]) % N).astype(F32) * (2.0 * math.pi / N)
    ph2 = ((kl[:, None] * n[None, :]) % N).astype(F32) * (2.0 * math.pi / N)
    c1, s1 = jnp.cos(ph1)[:, None, :], jnp.sin(ph1)[:, None, :]
    c2, s2 = jnp.cos(ph2)[None, :, :], jnp.sin(ph2)[None, :, :]
    cosm = (c1 * c2 - s1 * s2).reshape(L, L)
    sinm = (s1 * c2 + c1 * s2).reshape(L, L)
    nyq = jnp.where(n % 2 == 0, 1.0, -1.0).astype(F32)
    k = jnp.arange(L, dtype=jnp.int32)
    q = jnp.where(k[:, None] == 0, nyq[None, :], -sinm)
    return cosm.astype(BF16), q.astype(BF16)


def _filter_features(L):
    t = jnp.arange(L, dtype=F32) / L
    bands = jnp.arange(1, HY_EMB_BANDS + 1, dtype=F32)
    ang = 2.0 * math.pi * t[:, None] * bands[None, :]
    z = jnp.concatenate([t[:, None], jnp.cos(ang), jnp.sin(ang)], axis=-1)
    return jnp.pad(z, ((0, 0), (0, LANES - z.shape[1])))


def _tile(n, pref):
    return pref if n % pref == 0 else n


def kernel(x, c, ctx, c_ctx, w_mod, b_mod, norm1_g, w_in, attn_sink, hy_conv_w, hy_conv_b, hy_fw1, hy_fb1, hy_fw2, hy_fb2, hy_fw3, hy_fb3, hy_freq, hy_skip, w_o_attn, w_o_hy, w_out, norm2_g, peer_wq, peer_keys, peer_u, peer_v, final_g):
    B, L, D = x.shape
    assert B == 4 and D == D_MODEL and w_mod.shape[0] == 1
    T = B * L
    li = 0

    c8 = jnp.concatenate([c, c_ctx[None, :], jnp.zeros((3, D), F32)], axis=0)
    mod3 = _mod_call(c8, w_mod[li], b_mod[li]).reshape(8, 6, D)
    g1 = norm1_g[li].reshape(1, D)
    g2 = norm2_g[li].reshape(1, D)

    w = w_in[li]
    wq, wk, wv = w[:, :OFF_K], _dup_cols(w[:, OFF_K:OFF_V]), _dup_cols(w[:, OFF_V:OFF_HY])
    w_cat = jnp.concatenate([wq, wk, wv, w[:, OFF_HY:OFF_G], _rot_cols(wq), _rot_cols(wk)], axis=1).astype(BF16)
    cos_t, sin_t = _rope_tables(L)
    q, k, v, hy = _inproj_call(x, mod3, g1, w_cat, cos_t, sin_t, _tile(L, 512))
    kx, vx = _ctxproj_call(ctx, mod3, g1, jnp.concatenate([wk, wv], axis=1).astype(BF16))

    gsz = N_HEADS // N_KV_HEADS
    sink_b = jnp.broadcast_to(
        jnp.repeat(attn_sink[li].astype(F32).reshape(N_KV_HEADS, gsz), BLOCK, axis=1)[:, :, None],
        (N_KV_HEADS, gsz * BLOCK, LANES))
    y_attn = _attn_call(q, k, v, kx, vx, sink_b)

    H = HY_FILTER_HIDDEN
    w1p = jnp.pad(hy_fw1[li], ((0, LANES - hy_fw1.shape[1]), (0, 0)))
    deltas = jnp.abs(jnp.linspace(math.log(HY_DECAY_TARGET) / HY_SLOW_DECAY,
                                  math.log(HY_DECAY_TARGET) / HY_FAST_DECAY, HY_WIDTH, dtype=F32)).reshape(1, -1)
    hs, hd = _filt_call(_filter_features(L), w1p, hy_fb1[li].reshape(1, H), hy_freq[li].reshape(1, H),
                        hy_fw2[li], hy_fb2[li].reshape(1, H), hy_fw3[li], hy_fb3[li].reshape(1, -1), deltas)
    u, x0c = _sconv_call(hy, hy_conv_w[li], hy_conv_b[li].reshape(1, -1))
    kk = jnp.arange(L // 2, dtype=F32)[:, None] * (math.pi / L)
    cw = jnp.broadcast_to(jnp.cos(kk), (L // 2, LANES))
    sw = jnp.broadcast_to(jnp.sin(kk), (L // 2, LANES))
    yc = _lconv_call(u, hs, hd, _dft_table(L // 2), cw, sw, hy_skip[li].reshape(1, -1), _tile(L // 2, 256))

    x1, h2t, qp = _merge_call(x, mod3, g1, g2, y_attn, yc, x0c, w[:, OFF_G:].astype(BF16),
                             w_o_attn[li].astype(BF16), w_o_hy[li].astype(BF16), w_out[li].astype(BF16),
                             peer_wq[li].astype(BF16), _tile(L, 512))

    keys = peer_keys[li]
    zk = jnp.zeros_like(keys[:, 0])
    kb = jnp.concatenate([jnp.concatenate([keys[:, 0], zk], axis=2),
                          jnp.concatenate([zk, keys[:, 1]], axis=2)], axis=1)
    r2, e2, c1, e1 = _route_call(qp.reshape(T, D), kb, _tile(T, 512))

    out = _peer_call(h2t, peer_u[li].astype(BF16), peer_v[li].astype(BF16), r2, e2, c1, e1,
                     x1.reshape(T, D), mod3[:B, 5:6, :], final_g.reshape(1, D),
                     _tile(T, 1024), 1024, L)
    return out.reshape(B, L, D)
```
